```python
import math
import jax
import jax.numpy as jnp
from jax import lax
import numpy as np

D_MODEL = 1024
BATCH = 4
SEQ = 4096
DEPTH = 2
DEC_BATCH = 32
DEC_SEQ = 8
PAST_LEN = 8192
PAGE_SIZE = 128

N_MIXERS = 2
N_CONV_LAYERS = (DEPTH + 1) // 2
N_NSA_LAYERS = DEPTH // 2
D_CONV = D_MODEL
CONV_W = 31
N_HEADS = 16
N_KV = 4
N_REP = N_HEADS // N_KV
HEAD_DIM = D_MODEL // N_HEADS
ROT_DIM = HEAD_DIM // 4
ROPE_THETA = 500000.0
L_CMP = 32
STRIDE = L_CMP // 2
CMP_HID = 2 * HEAD_DIM
L_SEL = 64
N_TOP = 16
WINDOW = 512
Q_BLOCK = 64
Q_DIM = N_HEADS * HEAD_DIM
KV_DIM = N_KV * HEAD_DIM
GATE_DIM = 3 * N_HEADS
D_IN_NSA = Q_DIM + 6 * KV_DIM + GATE_DIM
D_FF = 4 * D_MODEL
D_PLE = 256
EPS = 1e-6
NEG = -1e30
FORCE = 1e6

kernel_name = 'hybrid_conformer_conv_nsa_decoder_step'


def rms_norm(x, g):
    x32 = x.astype(jnp.float32)
    y = x32 * lax.rsqrt(jnp.mean(x32 * x32, axis=-1, keepdims=True) + EPS)
    return (y * g.astype(jnp.float32)).astype(x.dtype)


def layer_norm(x, g, b):
    x32 = x.astype(jnp.float32)
    xc = x32 - jnp.mean(x32, axis=-1, keepdims=True)
    var = jnp.mean(xc * xc, axis=-1, keepdims=True)
    return (xc * lax.rsqrt(var + EPS) * g.astype(jnp.float32) + b.astype(jnp.float32)).astype(x.dtype)


def rope(x, pos):
    half = ROT_DIM // 2
    inv = jnp.float32(ROPE_THETA) ** (-jnp.arange(half, dtype=jnp.float32) * (2.0 / ROT_DIM))
    ang = pos.astype(jnp.float32)[:, None] * inv[None, :]
    cos = jnp.cos(ang)[None, :, None, :]
    sin = jnp.sin(ang)[None, :, None, :]
    x1 = x[..., :half].astype(jnp.float32)
    x2 = x[..., half:ROT_DIM].astype(jnp.float32)
    rot = jnp.concatenate([x1 * cos - x2 * sin, x2 * cos + x1 * sin], axis=-1).astype(x.dtype)
    return jnp.concatenate([rot, x[..., ROT_DIM:]], axis=-1)


def squared_relu_mlp(h, w1, w2):
    return jnp.square(jax.nn.relu(h @ w1)) @ w2


def conv_module(h, state, w_pw1, b_pw1, w_dw, b_dw, ln_g, ln_b, w_pw2, b_pw2):
    u = h @ w_pw1 + b_pw1
    u = u[..., :D_CONV] * jax.nn.sigmoid(u[..., D_CONV:])
    ext = jnp.concatenate([state.astype(u.dtype), u], axis=1)
    y = lax.conv_general_dilated(ext, w_dw[:, None, :].astype(u.dtype), (1,), 'VALID',
                                 dimension_numbers=('NWC', 'WIO', 'NWC'),
                                 feature_group_count=D_CONV) + b_dw
    y = jax.nn.silu(layer_norm(y, ln_g, ln_b))
    return y @ w_pw2 + b_pw2, ext[:, -(CONV_W - 1):]


def nsa_project(h, pos, w_in):
    B, T, _ = h.shape
    z = h @ w_in
    cuts = np.cumsum([Q_DIM] + [KV_DIM] * 6).tolist()
    q, kc, vc, ks, vs, kw, vw, g = jnp.split(z, cuts, axis=-1)
    q = q.reshape(B, T, N_HEADS, HEAD_DIM)
    q_rot = rope(q, pos).reshape(B, T, N_KV, N_REP, HEAD_DIM)
    q = q.reshape(B, T, N_KV, N_REP, HEAD_DIM)
    kv = lambda a: a.reshape(B, T, N_KV, HEAD_DIM)
    gates = jax.nn.sigmoid(g.reshape(B, T, N_KV, N_REP, 3))
    return q, q_rot, kv(kc), kv(vc), rope(kv(ks), pos), kv(vs), rope(kv(kw), pos), kv(vw), gates


def compress(k, pe, w1, w2):
    B, S, G, Dh = k.shape
    n_ch = -(-S // STRIDE)
    kp = jnp.pad(k, ((0, 0), (0, n_ch * STRIDE - S), (0, 0), (0, 0)))
    chunks = kp.reshape(B, n_ch, STRIDE, G, Dh)
    a = jnp.einsum('bnsgd,sdh->bngh', chunks, w1[:STRIDE])
    b = jnp.einsum('bnsgd,sdh->bngh', chunks, w1[STRIDE:])
    c = jnp.einsum('ld,ldh->h', pe, w1)
    hid = jax.nn.silu(a[:, :-1] + b[:, 1:] + c)
    return jnp.einsum('bngh,hd->bngd', hid, w2)


def to_sel_blocks(k):
    B, S, G, Dh = k.shape
    n_sel = -(-S // L_SEL)
    kp = jnp.pad(k, ((0, 0), (0, n_sel * L_SEL - S), (0, 0), (0, 0)))
    return kp.reshape(B, n_sel, L_SEL, G, Dh).transpose(0, 3, 1, 2, 4)


def nsa_core(q, q_rot, gates, q_pos, kc, vc, ksb, vsb, kw, vw, kw_pos):
    B = q.shape[0]
    n_cmp = kc.shape[1]
    n_sel = ksb.shape[2]
    scale = HEAD_DIM ** -0.5
    cmp_end = jnp.arange(n_cmp) * STRIDE + (L_CMP - 1)
    cmask = (cmp_end[None, :] <= q_pos[:, None])[None, :, None, None, :]
    lc = jnp.einsum('bqgrd,bcgd->bqgrc', q, kc).astype(jnp.float32) * scale
    pc = jax.nn.softmax(jnp.where(cmask, lc, NEG), axis=-1) * cmask
    o_cmp = jnp.einsum('bqgrc,bcgd->bqgrd', pc.astype(vc.dtype), vc)
    ci = jnp.arange(n_cmp)[:, None] * STRIDE
    sj = jnp.arange(n_sel)[None, :] * L_SEL
    overlap = ((ci < sj + L_SEL) & (ci + L_CMP > sj)).astype(jnp.float32)
    imp = jnp.einsum('bqgrc,cs->bqgs', pc, overlap)
    blk = jnp.arange(n_sel)[None, :]
    cur = (q_pos // L_SEL)[:, None]
    forced = ((blk == 0) | (blk == cur) | (blk == cur - 1))[None, :, None, :]
    future = (blk * L_SEL > q_pos[:, None])[None, :, None, :]
    score = jnp.where(future, -jnp.inf, jnp.where(forced, FORCE, imp))
    _, idx = lax.top_k(score, min(N_TOP, n_sel))
    bi = jnp.arange(B)[:, None, None, None]
    gi = jnp.arange(N_KV)[None, None, :, None]
    k_g = ksb[bi, gi, idx]
    v_g = vsb[bi, gi, idx]
    sel_pos = idx[..., None] * L_SEL + jnp.arange(L_SEL)
    smask = (sel_pos <= q_pos[None, :, None, None, None])[:, :, :, None]
    ls = jnp.einsum('bqgrd,bqgkld->bqgrkl', q_rot, k_g).astype(jnp.float32) * scale
    ls = jnp.where(smask, ls, NEG)
    shp = ls.shape
    ps = jax.nn.softmax(ls.reshape(shp[:4] + (-1,)), axis=-1).reshape(shp)
    o_sel = jnp.einsum('bqgrkl,bqgkld->bqgrd', ps.astype(v_g.dtype), v_g)
    wmask = ((kw_pos[None, :] <= q_pos[:, None]) & (kw_pos[None, :] > q_pos[:, None] - WINDOW)
             & (kw_pos[None, :] >= 0))[None, :, None, None, :]
    lw = jnp.einsum('bqgrd,bkgd->bqgrk', q_rot, kw).astype(jnp.float32) * scale
    pw = jax.nn.softmax(jnp.where(wmask, lw, NEG), axis=-1)
    o_win = jnp.einsum('bqgrk,bkgd->bqgrd', pw.astype(vw.dtype), vw)
    return gates[..., 0:1] * o_cmp + gates[..., 1:2] * o_sel + gates[..., 2:3] * o_win


def nsa_prompt(h, w_in, w_out, cmp_pe, cmp_w1, cmp_w2):
    B, T, _ = h.shape
    pos = jnp.arange(T)
    q, q_rot, kc, vc, ks, vs, kw, vw, gates = nsa_project(h, pos, w_in)
    kcc = compress(kc, cmp_pe[0], cmp_w1[0], cmp_w2[0])
    vcc = compress(vc, cmp_pe[1], cmp_w1[1], cmp_w2[1])
    ksb = to_sel_blocks(ks)
    vsb = to_sel_blocks(vs)
    kw_pad = jnp.pad(kw, ((0, 0), (WINDOW, 0), (0, 0), (0, 0)))
    vw_pad = jnp.pad(vw, ((0, 0), (WINDOW, 0), (0, 0), (0, 0)))
    qb = min(Q_BLOCK, T)

    def one_block(n):
        s = n * qb
        sl = lambda a: lax.dynamic_slice_in_dim(a, s, qb, axis=1)
        q_pos = s + jnp.arange(qb)
        kw_pos = s - WINDOW + jnp.arange(WINDOW + qb)
        kw_b = lax.dynamic_slice_in_dim(kw_pad, s, WINDOW + qb, axis=1)
        vw_b = lax.dynamic_slice_in_dim(vw_pad, s, WINDOW + qb, axis=1)
        return nsa_core(sl(q), sl(q_rot), sl(gates), q_pos, kcc, vcc, ksb, vsb, kw_b, vw_b, kw_pos)

    o = lax.map(one_block, jnp.arange(T // qb))
    o = jnp.moveaxis(o, 0, 1).reshape(B, T, Q_DIM)
    n_win = min(WINDOW, T)
    win_rows = jnp.stack([kw, vw], axis=2)[:, T - n_win:]
    return o @ w_out, jnp.stack([kc, vc], axis=2), jnp.stack([ks, vs], axis=2), win_rows


def nsa_sample(h, cache_cmp_l, cache_sel_l, cache_win_l, page_table, w_in, w_out, cmp_pe, cmp_w1, cmp_w2):
    B, T, _ = h.shape
    past_len = page_table.shape[1] * cache_cmp_l.shape[1]
    w_buf = cache_win_l.shape[1]
    pos = past_len + jnp.arange(T)
    q, q_rot, kc, vc, ks, vs, kw, vw, gates = nsa_project(h, pos, w_in)
    new_cmp = jnp.stack([kc, vc], axis=2)
    new_sel = jnp.stack([ks, vs], axis=2)
    new_win = jnp.stack([kw, vw], axis=2)
    past_cmp = cache_cmp_l[page_table].reshape(B, past_len, 2, N_KV, HEAD_DIM)
    past_sel = cache_sel_l[page_table].reshape(B, past_len, 2, N_KV, HEAD_DIM)
    full_cmp = jnp.concatenate([past_cmp.astype(h.dtype), new_cmp], axis=1)
    full_sel = jnp.concatenate([past_sel.astype(h.dtype), new_sel], axis=1)
    kcc = compress(full_cmp[:, :, 0], cmp_pe[0], cmp_w1[0], cmp_w2[0])
    vcc = compress(full_cmp[:, :, 1], cmp_pe[1], cmp_w1[1], cmp_w2[1])
    ksb = to_sel_blocks(full_sel[:, :, 0])
    vsb = to_sel_blocks(full_sel[:, :, 1])
    win_all = jnp.concatenate([cache_win_l.astype(h.dtype), new_win], axis=1)
    kw_pos = past_len - w_buf + jnp.arange(w_buf + T)
    o = nsa_core(q, q_rot, gates, pos, kcc, vcc, ksb, vsb, win_all[:, :, 0], win_all[:, :, 1], kw_pos)
    return o.reshape(B, T, Q_DIM) @ w_out, new_cmp, new_sel, win_all[:, -w_buf:]


def setup_inputs(seed: int = 0) -> dict:
    key = jax.random.key(seed)
    ks = jax.random.split(key, 32)

    def nrm(i, shape, scale=1.0):
        return jax.random.normal(ks[i], shape, jnp.float32) * scale

    n_pages = PAST_LEN // PAGE_SIZE
    n_pool = (DEC_BATCH * n_pages * 5) // 4
    w_buf = min(WINDOW, PAST_LEN)
    page_table = jax.random.permutation(ks[6], n_pool)[:DEC_BATCH * n_pages].reshape(DEC_BATCH, n_pages).astype(jnp.int32)
    return {
        'x_prompt': nrm(0, (BATCH, SEQ, D_MODEL)),
        'x_sample': nrm(1, (DEC_BATCH, DEC_SEQ, D_MODEL)),
        'cache_cmp': nrm(2, (N_NSA_LAYERS, n_pool, PAGE_SIZE, 2, N_KV, HEAD_DIM)),
        'cache_sel': nrm(3, (N_NSA_LAYERS, n_pool, PAGE_SIZE, 2, N_KV, HEAD_DIM)),
        'cache_win': nrm(4, (N_NSA_LAYERS, DEC_BATCH, w_buf, 2, N_KV, HEAD_DIM)),
        'state_conv': nrm(5, (N_CONV_LAYERS, DEC_BATCH, CONV_W - 1, D_CONV)),
        'page_table': page_table,
        'p_prompt': nrm(7, (DEPTH, BATCH, SEQ, D_PLE)),
        'p_sample': nrm(8, (DEPTH, DEC_BATCH, DEC_SEQ, D_PLE)),
        'norm_g': 1.0 + nrm(9, (DEPTH, 4, D_MODEL), 0.02),
        'w_ff1': nrm(10, (DEPTH, D_MODEL, D_FF), D_MODEL ** -0.5),
        'w_ff2': nrm(11, (DEPTH, D_FF, D_MODEL), D_FF ** -0.5),
        'w_ple': nrm(12, (DEPTH, D_PLE, D_MODEL), D_PLE ** -0.5),
        'w_ple_gate': nrm(13, (DEPTH, D_MODEL, D_MODEL), D_MODEL ** -0.5),
        'b_ple_gate': nrm(14, (DEPTH, D_MODEL), 0.01),
        'conv_w_pw1': nrm(15, (N_CONV_LAYERS, D_MODEL, 2 * D_CONV), D_MODEL ** -0.5),
        'conv_b_pw1': nrm(16, (N_CONV_LAYERS, 2 * D_CONV), 0.01),
        'conv_w_dw': nrm(17, (N_CONV_LAYERS, CONV_W, D_CONV), CONV_W ** -0.5),
        'conv_b_dw': nrm(18, (N_CONV_LAYERS, D_CONV), 0.01),
        'conv_ln_g': 1.0 + nrm(19, (N_CONV_LAYERS, D_CONV), 0.02),
        'conv_ln_b': nrm(20, (N_CONV_LAYERS, D_CONV), 0.01),
        'conv_w_pw2': nrm(21, (N_CONV_LAYERS, D_CONV, D_MODEL), D_CONV ** -0.5),
        'conv_b_pw2': nrm(22, (N_CONV_LAYERS, D_MODEL), 0.01),
        'nsa_w_in': nrm(23, (N_NSA_LAYERS, D_MODEL, D_IN_NSA), D_MODEL ** -0.5),
        'nsa_w_out': nrm(24, (N_NSA_LAYERS, Q_DIM, D_MODEL), Q_DIM ** -0.5),
        'nsa_cmp_pe': nrm(25, (N_NSA_LAYERS, 2, L_CMP, HEAD_DIM), 0.1),
        'nsa_cmp_w1': nrm(26, (N_NSA_LAYERS, 2, L_CMP, HEAD_DIM, CMP_HID), (L_CMP * HEAD_DIM) ** -0.5),
        'nsa_cmp_w2': nrm(27, (N_NSA_LAYERS, 2, CMP_HID, HEAD_DIM), CMP_HID ** -0.5),
    }


def reference(x_prompt, x_sample, cache_cmp, cache_sel, cache_win, state_conv, page_table, p_prompt, p_sample,
              norm_g, w_ff1, w_ff2, w_ple, w_ple_gate, b_ple_gate,
              conv_w_pw1, conv_b_pw1, conv_w_dw, conv_b_dw, conv_ln_g, conv_ln_b, conv_w_pw2, conv_b_pw2,
              nsa_w_in, nsa_w_out, nsa_cmp_pe, nsa_cmp_w1, nsa_cmp_w2):

    def run(x, p, sample):
        B = x.shape[0]
        cmp_rows, sel_rows, win_rows, conv_rows = [], [], [], []
        for i in range(DEPTH):
            h = rms_norm(x, norm_g[i, 0])
            if i % N_MIXERS == 0:
                c = i // N_MIXERS
                st = state_conv[c] if sample else jnp.zeros((B, CONV_W - 1, D_CONV), x.dtype)
                m, st_new = conv_module(h, st, conv_w_pw1[c], conv_b_pw1[c], conv_w_dw[c], conv_b_dw[c],
                                        conv_ln_g[c], conv_ln_b[c], conv_w_pw2[c], conv_b_pw2[c])
                conv_rows.append(st_new)
            else:
                a = i // N_MIXERS
                if sample:
                    m, r_c, r_s, r_w = nsa_sample(h, cache_cmp[a], cache_sel[a], cache_win[a], page_table,
                                                  nsa_w_in[a], nsa_w_out[a], nsa_cmp_pe[a], nsa_cmp_w1[a], nsa_cmp_w2[a])
                else:
                    m, r_c, r_s, r_w = nsa_prompt(h, nsa_w_in[a], nsa_w_out[a], nsa_cmp_pe[a], nsa_cmp_w1[a], nsa_cmp_w2[a])
                cmp_rows.append(r_c)
                sel_rows.append(r_s)
                win_rows.append(r_w)
            x = x + rms_norm(m, norm_g[i, 1])
            f = squared_relu_mlp(rms_norm(x, norm_g[i, 2]), w_ff1[i], w_ff2[i])
            x = x + rms_norm(f, norm_g[i, 3])
            gate = jax.nn.sigmoid(x @ w_ple_gate[i] + b_ple_gate[i])
            x = x + gate * (p[i] @ w_ple[i])
        return x, jnp.stack(cmp_rows), jnp.stack(sel_rows), jnp.stack(win_rows), jnp.stack(conv_rows)

    y_prompt, cmp_p, sel_p, win_p, conv_p = run(x_prompt, p_prompt, False)
    y_sample, cmp_s, sel_s, win_s, conv_s = run(x_sample, p_sample, True)
    return (y_prompt, y_sample, cmp_p, cmp_s, sel_p, sel_s, win_p, win_s, conv_p, conv_s)
```

```python
import functools
import math

import numpy as np
import jax
import jax.numpy as jnp
from jax import lax
from jax.experimental import pallas as pl
from jax.experimental.pallas import tpu as pltpu

F32 = jnp.float32
BF16 = jnp.bfloat16

D_MODEL = 1024
N_HEADS = 16
N_KV = 4
N_REP = N_HEADS // N_KV
HEAD_DIM = 64
ROT_DIM = HEAD_DIM // 4
ROPE_THETA = 500000.0
L_CMP = 32
STRIDE = 16
CMP_HID = 2 * HEAD_DIM
L_SEL = 64
N_TOP = 16
WINDOW = 512
CONV_W = 31
Q_DIM = N_HEADS * HEAD_DIM
KV_DIM = N_KV * HEAD_DIM
GATE_DIM = 3 * N_HEADS
D_FF = 4 * D_MODEL
EPS = 1e-6
NEG = -1e30
FORCE = 1e6
SCALE = HEAD_DIM ** -0.5

LANES = 128
CONV_HALO = 32
VMEM_LIMIT = 56 * 1024 * 1024


def _pick(n, cands):
    for c in cands:
        if n % c == 0:
            return c
    raise ValueError(f"no tile in {cands} divides {n}")


def _const_spec(shape):
    nd = len(shape)
    return pl.BlockSpec(shape, lambda *_: (0,) * nd, pipeline_mode=pl.Buffered(1))


def _params(sem):
    return pltpu.CompilerParams(dimension_semantics=sem, vmem_limit_bytes=VMEM_LIMIT)


def _sigmoid(x):
    return 1.0 / (1.0 + jnp.exp(-x))


def _rms(x, g):
    return x * lax.rsqrt(jnp.mean(x * x, axis=-1, keepdims=True) + EPS) * g


def _dot(a, b):
    return jnp.dot(a, b, preferred_element_type=F32)


def _dot_nt(a, b):
    return lax.dot_general(a, b, (((1,), (1,)), ((), ())), preferred_element_type=F32)


def _split_bf16(x):
    hi = x.astype(BF16)
    lo = (x - hi.astype(F32)).astype(BF16)
    return hi, lo


def _conv_front_body(x_ref, g_ref, w_ref, b_ref, u_ref):
    h = _rms(x_ref[...], g_ref[...]).astype(BF16)
    z = _dot(h, w_ref[...]) + b_ref[...]
    u_ref[...] = z[:, :D_MODEL] * _sigmoid(z[:, D_MODEL:])


def _conv_front(x2d, g, w_bf, b):
    n = x2d.shape[0]
    tm = _pick(n, (512, 256, 128, 64, 32, 16, 8))
    return pl.pallas_call(
        _conv_front_body,
        grid=(n // tm,),
        in_specs=[pl.BlockSpec((tm, D_MODEL), lambda i: (i, 0)),
                  _const_spec((1, D_MODEL)),
                  _const_spec((D_MODEL, 2 * D_MODEL)),
                  _const_spec((1, 2 * D_MODEL))],
        out_specs=pl.BlockSpec((tm, D_MODEL), lambda i: (i, 0)),
        out_shape=jax.ShapeDtypeStruct((n, D_MODEL), F32),
        compiler_params=_params(("parallel",)),
        name="conv_front",
    )(x2d, g, w_bf, b)


_CONV_ROWS = 16


def _conv_back_body(main_ref, tail_ref, wdw_ref, bdw_ref, lng_ref, lnb_ref, w2_ref, b2_ref, x_ref, g1_ref,
                    o_ref, win_ref, y_ref):
    tt = main_ref.shape[1]
    win_ref[0:tt, :] = main_ref[0]
    win_ref[tt:tt + CONV_HALO, :] = tail_ref[0]
    first = CONV_HALO - (CONV_W - 1)

    for r0 in range(0, tt, _CONV_ROWS):
        acc = jnp.zeros((_CONV_ROWS, D_MODEL), F32)
        for k in range(CONV_W):
            acc = acc + win_ref[r0 + first + k:r0 + first + k + _CONV_ROWS, :] * wdw_ref[k:k + 1, :]
        y_ref[r0:r0 + _CONV_ROWS, :] = acc + bdw_ref[...]
    y = y_ref[...]
    yc = y - jnp.mean(y, axis=-1, keepdims=True)
    var = jnp.mean(yc * yc, axis=-1, keepdims=True)
    ln = yc * lax.rsqrt(var + EPS) * lng_ref[...] + lnb_ref[...]
    act = (ln * _sigmoid(ln)).astype(BF16)
    m = _dot(act, w2_ref[...]) + b2_ref[...]
    o_ref[0] = x_ref[0] + _rms(m, g1_ref[...])


def _conv_back(ext, x3d, wdw, bdw, lng, lnb, w2_bf, b2, g1):
    b, t, _ = x3d.shape
    tt = _pick(t, (256, 128, 64, 32))
    halo_blocks = tt // CONV_HALO
    return pl.pallas_call(
        _conv_back_body,
        grid=(b, t // tt),
        in_specs=[pl.BlockSpec((1, tt, D_MODEL), lambda bi, i: (bi, i, 0)),
                  pl.BlockSpec((1, CONV_HALO, D_MODEL), lambda bi, i: (bi, (i + 1) * halo_blocks, 0)),
                  _const_spec((CONV_HALO, D_MODEL)),
                  _const_spec((1, D_MODEL)), _const_spec((1, D_MODEL)), _const_spec((1, D_MODEL)),
                  _const_spec((D_MODEL, D_MODEL)), _const_spec((1, D_MODEL)),
                  pl.BlockSpec((1, tt, D_MODEL), lambda bi, i: (bi, i, 0)),
                  _const_spec((1, D_MODEL))],
        out_specs=pl.BlockSpec((1, tt, D_MODEL), lambda bi, i: (bi, i, 0)),
        out_shape=jax.ShapeDtypeStruct((b, t, D_MODEL), F32),
        scratch_shapes=[pltpu.VMEM((tt + CONV_HALO, D_MODEL), F32), pltpu.VMEM((tt, D_MODEL), F32)],
        compiler_params=_params(("parallel", "parallel")),
        name="conv_back",
    )(ext, ext, wdw, bdw, lng, lnb, w2_bf, b2, x3d, g1)


_FF_CHUNK = 1024


def _ffn_body(x_ref, g2_ref, w1_ref, w2_ref, g3_ref, wg_ref, bg_ref, p_ref, wp_ref, o_ref):
    x = x_ref[...]
    h = _rms(x, g2_ref[...]).astype(BF16)
    f = jnp.zeros(x.shape, F32)
    for c in range(D_FF // _FF_CHUNK):
        a = jnp.maximum(_dot(h, w1_ref[:, c * _FF_CHUNK:(c + 1) * _FF_CHUNK]), 0.0)
        f = f + _dot((a * a).astype(BF16), w2_ref[c * _FF_CHUNK:(c + 1) * _FF_CHUNK, :])
    x2 = x + _rms(f, g3_ref[...])
    gate = _sigmoid(_dot(x2.astype(BF16), wg_ref[...]) + bg_ref[...])
    o_ref[...] = x2 + gate * _dot(p_ref[...].astype(BF16), wp_ref[...])


def _ffn(x2d, g2, w1_bf, w2_bf, g3, wg_bf, bg, p2d, wp_bf):
    n = x2d.shape[0]
    d_ple = p2d.shape[1]
    tm = _pick(n, (256, 128, 64, 32, 16, 8))
    return pl.pallas_call(
        _ffn_body,
        grid=(n // tm,),
        in_specs=[pl.BlockSpec((tm, D_MODEL), lambda i: (i, 0)),
                  _const_spec((1, D_MODEL)),
                  _const_spec((D_MODEL, D_FF)), _const_spec((D_FF, D_MODEL)),
                  _const_spec((1, D_MODEL)),
                  _const_spec((D_MODEL, D_MODEL)), _const_spec((1, D_MODEL)),
                  pl.BlockSpec((tm, d_ple), lambda i: (i, 0)),
                  _const_spec((d_ple, D_MODEL))],
        out_specs=pl.BlockSpec((tm, D_MODEL), lambda i: (i, 0)),
        out_shape=jax.ShapeDtypeStruct((n, D_MODEL), F32),
        compiler_params=_params(("parallel",)),
        name="ffn_ple",
    )(x2d, g2, w1_bf, w2_bf, g3, wg_bf, bg, p2d, wp_bf)


_PROJ_W = Q_DIM + 6 * KV_DIM + LANES


def _rope_block(blk, c, su, sd):
    return blk * c + pltpu.roll(blk, ROT_DIM // 2, 1) * su + pltpu.roll(blk, LANES - ROT_DIM // 2, 1) * sd


def _nsa_proj_body(x_ref, g_ref, w_ref, c_ref, su_ref, sd_ref,
                   q_ref, qr_ref, cmp_ref, sel_ref, win_ref, gate_ref):
    h = _rms(x_ref[...], g_ref[...]).astype(BF16)
    z = _dot(h, w_ref[...])
    c, su, sd = c_ref[...], su_ref[...], sd_ref[...]
    for i in range(Q_DIM // LANES):
        blk = z[:, i * LANES:(i + 1) * LANES]
        q_ref[:, i * LANES:(i + 1) * LANES] = (blk * SCALE).astype(BF16)
        qr_ref[:, i * LANES:(i + 1) * LANES] = (_rope_block(blk, c, su, sd) * SCALE).astype(BF16)
    o = Q_DIM
    cmp_ref[...] = z[:, o:o + 2 * KV_DIM]
    o += 2 * KV_DIM
    for dst in (sel_ref, win_ref):
        for i in range(KV_DIM // LANES):
            dst[:, i * LANES:(i + 1) * LANES] = _rope_block(z[:, o + i * LANES:o + (i + 1) * LANES], c, su, sd)
        dst[:, KV_DIM:] = z[:, o + KV_DIM:o + 2 * KV_DIM]
        o += 2 * KV_DIM
    gate_ref[...] = _sigmoid(z[:, o:o + LANES])


def _rope_tables(pos, rows):
    half = ROT_DIM // 2
    inv = jnp.float32(ROPE_THETA) ** (-jnp.arange(half, dtype=F32) * (2.0 / ROT_DIM))
    ang = pos.astype(F32)[:, None] * inv[None, :]
    lane = np.arange(LANES)
    within = lane % HEAD_DIM
    cos = jnp.cos(ang)[:, lane % half]
    sin = jnp.sin(ang)[:, lane % half]
    c = jnp.where(within[None, :] < ROT_DIM, cos, 1.0)
    su = jnp.where((within[None, :] >= half) & (within[None, :] < ROT_DIM), sin, 0.0)
    sd = jnp.where(within[None, :] < half, -sin, 0.0)
    reps = rows // pos.shape[0]
    return tuple(jnp.tile(a, (reps, 1)) for a in (c, su, sd))


def _nsa_proj(x2d, g, w_bf, pos):
    n = x2d.shape[0]
    t = pos.shape[0]
    tm = _pick(n, (256, 128, 64, 32, 16, 8))
    if t >= tm:
        assert t % tm == 0
        tabs = _rope_tables(pos, t)
        nt = t // tm
        tab_spec = pl.BlockSpec((tm, LANES), lambda i: (i % nt, 0))
    else:
        assert tm % t == 0
        tabs = _rope_tables(pos, tm)
        tab_spec = _const_spec((tm, LANES))
    row = lambda w: pl.BlockSpec((tm, w), lambda i: (i, 0))
    return pl.pallas_call(
        _nsa_proj_body,
        grid=(n // tm,),
        in_specs=[row(D_MODEL), _const_spec((1, D_MODEL)), _const_spec((D_MODEL, _PROJ_W)),
                  tab_spec, tab_spec, tab_spec],
        out_specs=[row(Q_DIM), row(Q_DIM), row(2 * KV_DIM), row(2 * KV_DIM), row(2 * KV_DIM), row(LANES)],
        out_shape=[jax.ShapeDtypeStruct((n, Q_DIM), BF16), jax.ShapeDtypeStruct((n, Q_DIM), BF16),
                   jax.ShapeDtypeStruct((n, 2 * KV_DIM), F32), jax.ShapeDtypeStruct((n, 2 * KV_DIM), F32),
                   jax.ShapeDtypeStruct((n, 2 * KV_DIM), F32), jax.ShapeDtypeStruct((n, LANES), F32)],
        compiler_params=_params(("parallel",)),
        name="nsa_proj",
    )(x2d, g, w_bf, *tabs)


_CHUNK_W = STRIDE * 2 * KV_DIM
_AB_W = 2 * N_KV * CMP_HID


def _compress_body(x_ref, xl_ref, wk_ref, wv_ref, pek_ref, pev_ref, w1k_ref, w1v_ref, w2k_ref, w2v_ref,
                   kcc_ref, vcc_ref, abk_ref, abv_ref):
    rt = x_ref.shape[1]
    i = pl.program_id(1)
    n_rt = pl.num_programs(1)
    n_rows = kcc_ref.shape[1]

    def first_layer(src_ref, kv, w_ref):
        acc = jnp.zeros((src_ref.shape[1], _AB_W), F32)
        for s in range(STRIDE):
            lo = s * 2 * KV_DIM + kv * KV_DIM
            acc = acc + _dot(src_ref[0, :, lo:lo + KV_DIM].astype(BF16), w_ref[s])
        return acc

    r0 = pl.multiple_of(i * rt, rt)
    abk_ref[pl.ds(r0, rt), :] = first_layer(x_ref, 0, wk_ref)
    abv_ref[pl.ds(r0, rt), :] = first_layer(x_ref, 1, wv_ref)

    @pl.when(i == n_rt - 1)
    def _():
        half = _AB_W // 2
        for kv, (ab_ref, w_ref, pe_ref, w1_ref, w2_ref, out_ref) in enumerate((
                (abk_ref, wk_ref, pek_ref, w1k_ref, w2k_ref, kcc_ref),
                (abv_ref, wv_ref, pev_ref, w1v_ref, w2v_ref, vcc_ref))):
            ab_ref[n_rows:n_rows + 8, :] = first_layer(xl_ref, kv, w_ref)
            c = jnp.sum(pe_ref[...] * w1_ref[...], axis=0, keepdims=True)
            c = jnp.concatenate([c] * N_KV, axis=1)
            pre = ab_ref[0:n_rows, 0:half] + ab_ref[1:n_rows + 1, half:] + c
            hid = (pre * _sigmoid(pre)).astype(BF16)
            out_ref[0] = _dot(hid, w2_ref[...])


def _compress(x_chunks, x_last, wk, wv, pek, pev, w1k, w1v, w2k, w2v):
    b, n, _ = x_chunks.shape
    rt = _pick(n, (128, 64, 32, 16, 8)) if n % 8 == 0 else n
    flat = L_CMP * HEAD_DIM
    out = jax.ShapeDtypeStruct((b, n, KV_DIM), F32)
    return pl.pallas_call(
        _compress_body,
        grid=(b, n // rt),
        in_specs=[pl.BlockSpec((1, rt, _CHUNK_W), lambda bi, i: (bi, i, 0)),
                  pl.BlockSpec((1, 8, _CHUNK_W), lambda bi, i: (bi, 0, 0)),
                  _const_spec((STRIDE, KV_DIM, _AB_W)), _const_spec((STRIDE, KV_DIM, _AB_W)),
                  _const_spec((flat, 1)), _const_spec((flat, 1)),
                  _const_spec((flat, CMP_HID)), _const_spec((flat, CMP_HID)),
                  _const_spec((N_KV * CMP_HID, KV_DIM)), _const_spec((N_KV * CMP_HID, KV_DIM))],
        out_specs=[pl.BlockSpec((1, n, KV_DIM), lambda bi, i: (bi, 0, 0))] * 2,
        out_shape=[out, out],
        scratch_shapes=[pltpu.VMEM((n + 8, _AB_W), F32), pltpu.VMEM((n + 8, _AB_W), F32)],
        compiler_params=_params(("parallel", "arbitrary")),
        name="compress",
    )(x_chunks, x_last, wk, wv, pek, pev, w1k, w1v, w2k, w2v)


def _compress_weights(cmp_pe, cmp_w1, cmp_w2):
    eye = jnp.eye(N_KV, dtype=F32)
    outs = []
    for kv in range(2):
        w1 = cmp_w1[kv].reshape(2, STRIDE, HEAD_DIM, CMP_HID)
        big = jnp.einsum('gk,asdh->sgdakh', eye, w1).reshape(STRIDE, KV_DIM, _AB_W)
        w2 = jnp.einsum('gk,hd->ghkd', eye, cmp_w2[kv]).reshape(N_KV * CMP_HID, KV_DIM)
        outs.append((big.astype(BF16), cmp_pe[kv].reshape(L_CMP * HEAD_DIM, 1),
                     cmp_w1[kv].reshape(L_CMP * HEAD_DIM, CMP_HID), w2.astype(BF16)))
    return outs


def _select_blocks(imp_t, blk, qpos, n_real, k):
    cur = jnp.right_shift(qpos, int(math.log2(L_SEL)))
    forced = (blk == 0) | (blk == cur) | (blk == cur - 1)
    future = blk * L_SEL > qpos
    score = jnp.where(future, -jnp.inf, jnp.where(forced, FORCE, imp_t))
    score = jnp.where(blk < n_real, score, -jnp.inf)
    rank = jnp.zeros(score.shape, jnp.int32)
    for j in range(n_real):
        sj = score[j:j + 1, :]
        rank = rank + jnp.where(blk > j, jnp.where(sj >= score, 1, 0), jnp.where(sj > score, 1, 0))
    return (rank < k) & (blk < n_real)


def _overlap_t(n_sel_pad, n_cmp_pad, n_sel, n_cmp):
    ci = np.arange(n_cmp_pad)[None, :] * STRIDE
    sj = np.arange(n_sel_pad)[:, None] * L_SEL
    ov = (ci < sj + L_SEL) & (ci + L_CMP > sj)
    ov &= (np.arange(n_cmp_pad)[None, :] < n_cmp) & (np.arange(n_sel_pad)[:, None] < n_sel)
    return jnp.asarray(ov, dtype=BF16)


def _cmp_topk_body(n_sel, q_ref, kcc_ref, vcc_ref, ovt_ref, o_ref, bias_ref):
    tq = q_ref.shape[2]
    nc = kcc_ref.shape[2]
    nsp = ovt_ref.shape[0]
    t0 = pl.program_id(1) * tq
    qpos_col = t0 + lax.broadcasted_iota(jnp.int32, (tq, 1), 0)
    cmp_end = lax.broadcasted_iota(jnp.int32, (1, nc), 1) * STRIDE + (L_CMP - 1)
    cmask = cmp_end <= qpos_col
    blk = lax.broadcasted_iota(jnp.int32, (nsp, tq), 0)
    qpos_t = t0 + lax.broadcasted_iota(jnp.int32, (nsp, tq), 1)
    ovt = ovt_ref[...]
    for g in range(N_KV):
        kg = kcc_ref[0, g]
        vg = vcc_ref[0, g]
        pg = jnp.zeros((tq, nc), F32)
        for r in range(N_REP):
            h = g * N_REP + r
            lm = jnp.where(cmask, _dot_nt(q_ref[0, h], kg), NEG)
            e = jnp.exp(lm - jnp.max(lm, axis=-1, keepdims=True))
            pc = jnp.where(cmask, e / jnp.sum(e, axis=-1, keepdims=True), 0.0)
            o_ref[0, h] = _dot(pc.astype(BF16), vg)
            pg = pg + pc
        hi, lo = _split_bf16(pg)
        imp_t = _dot_nt(ovt, hi) + _dot_nt(ovt, lo)
        sel = _select_blocks(imp_t, blk, qpos_t, n_sel, min(N_TOP, n_sel))
        bias_ref[0, g] = jnp.where(sel, 0.0, NEG).astype(BF16)


def _cmp_topk(q_hm, kcc_gm, vcc_gm, n_cmp, n_sel, nsp):
    b, _, t, _ = q_hm.shape
    nc = kcc_gm.shape[2]
    tq = _pick(t, (256, 128))
    ovt = _overlap_t(nsp, nc, n_sel, n_cmp)
    return pl.pallas_call(
        functools.partial(_cmp_topk_body, n_sel),
        grid=(b, t // tq),
        in_specs=[pl.BlockSpec((1, N_HEADS, tq, HEAD_DIM), lambda bi, i: (bi, 0, i, 0)),
                  pl.BlockSpec((1, N_KV, nc, HEAD_DIM), lambda bi, i: (bi, 0, 0, 0)),
                  pl.BlockSpec((1, N_KV, nc, HEAD_DIM), lambda bi, i: (bi, 0, 0, 0)),
                  _const_spec((nsp, nc))],
        out_specs=[pl.BlockSpec((1, N_HEADS, tq, HEAD_DIM), lambda bi, i: (bi, 0, i, 0)),
                   pl.BlockSpec((1, N_KV, nsp, tq), lambda bi, i: (bi, 0, 0, i))],
        out_shape=[jax.ShapeDtypeStruct((b, N_HEADS, t, HEAD_DIM), F32),
                   jax.ShapeDtypeStruct((b, N_KV, nsp, t), BF16)],
        compiler_params=_params(("parallel", "parallel")),
        name="cmp_topk",
    )(q_hm, kcc_gm, vcc_gm, ovt)


def _flash_step(s, v, m_ref, l_ref, acc_ref):
    m_prev = m_ref[...]
    m_new = jnp.maximum(m_prev, jnp.max(s, axis=-1, keepdims=True))
    alpha = jnp.exp(m_prev - m_new)
    p = jnp.exp(s - m_new)
    l_ref[...] = alpha * l_ref[...] + jnp.sum(p, axis=-1, keepdims=True)
    acc_ref[...] = alpha * acc_ref[...] + _dot(p.astype(BF16), v)
    m_ref[...] = m_new


def _flash_init(m_ref, l_ref, acc_ref):
    m_ref[...] = jnp.full(m_ref.shape, NEG, F32)
    l_ref[...] = jnp.zeros(l_ref.shape, F32)
    acc_ref[...] = jnp.zeros(acc_ref.shape, F32)


def _sel_win_body(qp_ref, qr_ref, kp_ref, vs_ref, kw_ref, vw_ref, osel_ref, owin_ref, m_ref, l_ref, acc_ref):
    tq = qp_ref.shape[2]
    rows = N_REP * tq
    qt = pl.program_id(2)
    t0 = qt * tq
    qpos = t0 + (lax.broadcasted_iota(jnp.int32, (rows, 1), 0) & (tq - 1))
    kiota = lax.broadcasted_iota(jnp.int32, (1, tq), 1)

    q = qp_ref[0].reshape(rows, qp_ref.shape[3])
    _flash_init(m_ref, l_ref, acc_ref)

    def sel_chunk(j, carry):
        k0 = pl.multiple_of(j * tq, tq)
        s = _dot_nt(q, kp_ref[0, 0, pl.ds(k0, tq), :])
        s = jnp.where(k0 + kiota <= qpos, s, NEG)
        _flash_step(s, vs_ref[0, 0, pl.ds(k0, tq), :], m_ref, l_ref, acc_ref)
        return carry

    lax.fori_loop(0, qt + 1, sel_chunk, 0)
    osel_ref[0] = (acc_ref[...] / l_ref[...]).reshape(N_REP, tq, HEAD_DIM)

    q = qr_ref[0].reshape(rows, HEAD_DIM)
    _flash_init(m_ref, l_ref, acc_ref)

    def win_chunk(k0):
        s = _dot_nt(q, kw_ref[0, 0, pl.ds(k0, tq), :])
        kpos = k0 + kiota
        s = jnp.where((kpos <= qpos) & (kpos > qpos - WINDOW), s, NEG)
        _flash_step(s, vw_ref[0, 0, pl.ds(k0, tq), :], m_ref, l_ref, acc_ref)

    win_chunk(pl.multiple_of(t0, tq))
    for back in range(1, WINDOW // tq + 1):
        @pl.when(qt >= back)
        def _():
            win_chunk(pl.multiple_of(t0 - back * tq, tq))

    owin_ref[0] = (acc_ref[...] / l_ref[...]).reshape(N_REP, tq, HEAD_DIM)


def _sel_win(qp_hm, qr_hm, kp_gm, vs_gm, kw_gm, vw_gm):
    b, _, t, cw = qp_hm.shape
    tq = _pick(t, (256, 128))
    assert WINDOW % tq == 0
    rows = N_REP * tq
    qspec = lambda w: pl.BlockSpec((1, N_REP, tq, w), lambda bi, g, i: (bi, g, i, 0))
    kspec = lambda w: pl.BlockSpec((1, 1, t, w), lambda bi, g, i: (bi, g, 0, 0))
    out = jax.ShapeDtypeStruct((b, N_HEADS, t, HEAD_DIM), F32)
    return pl.pallas_call(
        _sel_win_body,
        grid=(b, N_KV, t // tq),
        in_specs=[qspec(cw), qspec(HEAD_DIM), kspec(cw), kspec(HEAD_DIM), kspec(HEAD_DIM), kspec(HEAD_DIM)],
        out_specs=[qspec(HEAD_DIM), qspec(HEAD_DIM)],
        out_shape=[out, out],
        scratch_shapes=[pltpu.VMEM((rows, 1), F32), pltpu.VMEM((rows, 1), F32), pltpu.VMEM((rows, HEAD_DIM), F32)],
        compiler_params=_params(("parallel", "parallel", "parallel")),
        name="sel_win",
    )(qp_hm, qr_hm, kp_gm, vs_gm, kw_gm, vw_gm)


def _nsa_out_body(oc_ref, os_ref, ow_ref, gate_ref, e_ref, w_ref, x_ref, g1_ref, o_ref):
    hi, lo = _split_bf16(gate_ref[...])
    o = jnp.zeros(oc_ref.shape, F32)
    for c, src in enumerate((oc_ref, os_ref, ow_ref)):
        o = o + (_dot(hi, e_ref[c]) + _dot(lo, e_ref[c])) * src[...]
    m = _dot(o.astype(BF16), w_ref[...])
    o_ref[...] = x_ref[...] + _rms(m, g1_ref[...])


def _gate_expand():
    e = np.zeros((3, LANES, Q_DIM), np.float32)
    for h in range(N_HEADS):
        for c in range(3):
            e[c, h * 3 + c, h * HEAD_DIM:(h + 1) * HEAD_DIM] = 1.0
    return jnp.asarray(e, dtype=BF16)


def _nsa_out(o_cmp, o_sel, o_win, gates, w_bf, x2d, g1):
    n = x2d.shape[0]
    tm = _pick(n, (256, 128, 64, 32, 16, 8))
    row = lambda w: pl.BlockSpec((tm, w), lambda i: (i, 0))
    return pl.pallas_call(
        _nsa_out_body,
        grid=(n // tm,),
        in_specs=[row(Q_DIM), row(Q_DIM), row(Q_DIM), row(LANES), _const_spec((3, LANES, Q_DIM)),
                  _const_spec((Q_DIM, D_MODEL)), row(D_MODEL), _const_spec((1, D_MODEL))],
        out_specs=row(D_MODEL),
        out_shape=jax.ShapeDtypeStruct((n, D_MODEL), F32),
        compiler_params=_params(("parallel",)),
        name="nsa_out",
    )(o_cmp, o_sel, o_win, gates, _gate_expand(), w_bf, x2d, g1)


def _gather_body(pt_ref, cmp_hbm, sel_hbm, cmp_out, sel_out, sem):
    b = pl.program_id(0)
    n_pages = cmp_out.shape[1]

    def copies(p):
        page = pt_ref[b, p]
        return (pltpu.make_async_copy(cmp_hbm.at[page], cmp_out.at[b, p], sem.at[0]),
                pltpu.make_async_copy(sel_hbm.at[page], sel_out.at[b, p], sem.at[1]))

    def start(p, carry):
        for c in copies(p):
            c.start()
        return carry

    def wait(p, carry):
        for c in copies(p):
            c.wait()
        return carry

    lax.fori_loop(0, n_pages, start, 0)
    lax.fori_loop(0, n_pages, wait, 0)


def _gather_pages(page_table, cache_cmp_l, cache_sel_l):
    b, n_pages = page_table.shape
    _, page, w = cache_cmp_l.shape
    out = jax.ShapeDtypeStruct((b, n_pages, page, w), F32)
    return pl.pallas_call(
        _gather_body,
        grid_spec=pltpu.PrefetchScalarGridSpec(
            num_scalar_prefetch=1,
            grid=(b,),
            in_specs=[pl.BlockSpec(memory_space=pl.ANY), pl.BlockSpec(memory_space=pl.ANY)],
            out_specs=[pl.BlockSpec(memory_space=pl.ANY), pl.BlockSpec(memory_space=pl.ANY)],
            scratch_shapes=[pltpu.SemaphoreType.DMA((2,))]),
        out_shape=[out, out],
        compiler_params=pltpu.CompilerParams(dimension_semantics=("arbitrary",)),
        name="page_gather",
    )(page_table, cache_cmp_l, cache_sel_l)


def _softmax_rows(s):
    e = jnp.exp(s - jnp.max(s, axis=-1, keepdims=True))
    return e / jnp.sum(e, axis=-1, keepdims=True)


def _sample_cmp_body(n_sel, past_len, t_new, q_ref, kcc_ref, vcc_ref, ovt_ref, rep_ref, o_ref, sel_ref):
    rows = q_ref.shape[1]
    nc = kcc_ref.shape[1]
    nsp = ovt_ref.shape[0]
    gq = N_KV * t_new
    tq_col = lax.broadcasted_iota(jnp.int32, (rows, 1), 0) & (t_new - 1)
    cmp_end = lax.broadcasted_iota(jnp.int32, (1, nc), 1) * STRIDE + (L_CMP - 1)
    cmask = cmp_end <= past_len + tq_col
    lm = jnp.where(cmask, _dot_nt(q_ref[0], kcc_ref[0].astype(BF16)), NEG)
    pc = jnp.where(cmask, _softmax_rows(lm), 0.0)
    o_ref[0] = _dot(pc.astype(BF16), vcc_ref[0].astype(BF16))
    pg = pc.reshape(N_KV, N_REP, t_new, nc).sum(axis=1).reshape(gq, nc)
    hi, lo = _split_bf16(pg)
    imp_t = _dot_nt(ovt_ref[...], hi) + _dot_nt(ovt_ref[...], lo)
    blk = lax.broadcasted_iota(jnp.int32, (nsp, gq), 0)
    qpos = past_len + (lax.broadcasted_iota(jnp.int32, (nsp, gq), 1) & (t_new - 1))
    sel = _select_blocks(imp_t, blk, qpos, n_sel, min(N_TOP, n_sel))
    sel_ref[0] = _dot_nt(rep_ref[...], jnp.where(sel, 1.0, 0.0).astype(BF16)).astype(BF16)


def _sample_cmp(q_bd, kcc, vcc, n_cmp, n_sel, nsp, past_len, t_new):
    b, rows, _ = q_bd.shape
    nc = kcc.shape[1]
    gq = N_KV * t_new
    ovt = _overlap_t(nsp, nc, n_sel, n_cmp)
    rep = np.zeros((rows, gq), np.float32)
    for h in range(N_HEADS):
        for t in range(t_new):
            rep[h * t_new + t, (h // N_REP) * t_new + t] = 1.0
    return pl.pallas_call(
        functools.partial(_sample_cmp_body, n_sel, past_len, t_new),
        grid=(b,),
        in_specs=[pl.BlockSpec((1, rows, KV_DIM), lambda bi: (bi, 0, 0)),
                  pl.BlockSpec((1, nc, KV_DIM), lambda bi: (bi, 0, 0)),
                  pl.BlockSpec((1, nc, KV_DIM), lambda bi: (bi, 0, 0)),
                  _const_spec((nsp, nc)), _const_spec((rows, gq))],
        out_specs=[pl.BlockSpec((1, rows, KV_DIM), lambda bi: (bi, 0, 0)),
                   pl.BlockSpec((1, rows, nsp), lambda bi: (bi, 0, 0))],
        out_shape=[jax.ShapeDtypeStruct((b, rows, KV_DIM), F32), jax.ShapeDtypeStruct((b, rows, nsp), BF16)],
        compiler_params=_params(("parallel",)),
        name="sample_cmp",
    )(q_bd, kcc, vcc, ovt, jnp.asarray(rep, dtype=BF16))


def _sample_sel_body(t_new, q_ref, sel_ref, e_ref, et_ref, kv_ref, tail_ref, o_ref, m_ref, l_ref, acc_ref):
    rows = q_ref.shape[1]
    c = pl.program_id(1)
    n_c = pl.num_programs(1)
    q = q_ref[0]

    @pl.when(c == 0)
    def _():
        _flash_init(m_ref, l_ref, acc_ref)

    @pl.when(c < n_c - 1)
    def _():
        kv = kv_ref[0]
        s = _dot_nt(q, kv[:, :KV_DIM].astype(BF16))
        s = jnp.where(_dot(sel_ref[0], e_ref[...]) > 0.5, s, NEG)
        _flash_step(s, kv[:, KV_DIM:].astype(BF16), m_ref, l_ref, acc_ref)

    @pl.when(c == n_c - 1)
    def _():
        kv = tail_ref[0]
        nk = kv.shape[0]
        s = _dot_nt(q, kv[:, :KV_DIM].astype(BF16))
        tq = lax.broadcasted_iota(jnp.int32, (rows, 1), 0) & (t_new - 1)
        ok = (_dot(sel_ref[0], et_ref[...]) > 0.5) & (lax.broadcasted_iota(jnp.int32, (1, nk), 1) <= tq)
        _flash_step(jnp.where(ok, s, NEG), kv[:, KV_DIM:].astype(BF16), m_ref, l_ref, acc_ref)
        o_ref[0] = acc_ref[...] / l_ref[...]


def _sample_sel(q_bd, sel01, past_sel, tail_sel, n_sel, t_new):
    b, rows, _ = q_bd.shape
    nsp = sel01.shape[2]
    past_len = past_sel.shape[1]
    kc = _pick(past_len, (1024, 512, 256, 128))
    n_c = past_len // kc
    nk = tail_sel.shape[1]
    key_blk = np.arange(past_len) // L_SEL
    e = (np.arange(nsp)[:, None] == key_blk[None, :]).astype(np.float32)
    et = np.zeros((nsp, nk), np.float32)
    et[n_sel - 1, :] = 1.0
    return pl.pallas_call(
        functools.partial(_sample_sel_body, t_new),
        grid=(b, n_c + 1),
        in_specs=[pl.BlockSpec((1, rows, KV_DIM), lambda bi, c: (bi, 0, 0)),
                  pl.BlockSpec((1, rows, nsp), lambda bi, c: (bi, 0, 0)),
                  pl.BlockSpec((nsp, kc), lambda bi, c: (0, jnp.minimum(c, n_c - 1))),
                  _const_spec((nsp, nk)),
                  pl.BlockSpec((1, kc, 2 * KV_DIM), lambda bi, c: (bi, jnp.minimum(c, n_c - 1), 0)),
                  pl.BlockSpec((1, nk, 2 * KV_DIM), lambda bi, c: (bi, 0, 0))],
        out_specs=pl.BlockSpec((1, rows, KV_DIM), lambda bi, c: (bi, 0, 0)),
        out_shape=jax.ShapeDtypeStruct((b, rows, KV_DIM), F32),
        scratch_shapes=[pltpu.VMEM((rows, 1), F32), pltpu.VMEM((rows, 1), F32), pltpu.VMEM((rows, KV_DIM), F32)],
        compiler_params=_params(("parallel", "arbitrary")),
        name="sample_sel",
    )(q_bd, sel01, jnp.asarray(e, dtype=BF16), jnp.asarray(et, dtype=BF16), past_sel, tail_sel)


def _sample_win_body(t_new, q_ref, kv_ref, tail_ref, o_ref):
    rows = q_ref.shape[1]
    wb = kv_ref.shape[1]
    nk = tail_ref.shape[1]
    q = q_ref[0]
    tq = lax.broadcasted_iota(jnp.int32, (rows, 1), 0) & (t_new - 1)
    kv = kv_ref[0]
    tail = tail_ref[0]
    s_old = jnp.where(lax.broadcasted_iota(jnp.int32, (1, wb), 1) > tq - WINDOW + wb,
                      _dot_nt(q, kv[:, :KV_DIM].astype(BF16)), NEG)
    s_new = jnp.where(lax.broadcasted_iota(jnp.int32, (1, nk), 1) <= tq,
                      _dot_nt(q, tail[:, :KV_DIM].astype(BF16)), NEG)
    m = jnp.maximum(jnp.max(s_old, axis=-1, keepdims=True), jnp.max(s_new, axis=-1, keepdims=True))
    p_old = jnp.exp(s_old - m)
    p_new = jnp.exp(s_new - m)
    den = jnp.sum(p_old, axis=-1, keepdims=True) + jnp.sum(p_new, axis=-1, keepdims=True)
    o = _dot(p_old.astype(BF16), kv[:, KV_DIM:].astype(BF16)) + _dot(p_new.astype(BF16), tail[:, KV_DIM:].astype(BF16))
    o_ref[0] = o / den


def _sample_win(q_bd, cache_win_l, tail_win, t_new):
    b, rows, _ = q_bd.shape
    wb = cache_win_l.shape[1]
    nk = tail_win.shape[1]
    return pl.pallas_call(
        functools.partial(_sample_win_body, t_new),
        grid=(b,),
        in_specs=[pl.BlockSpec((1, rows, KV_DIM), lambda bi: (bi, 0, 0)),
                  pl.BlockSpec((1, wb, 2 * KV_DIM), lambda bi: (bi, 0, 0)),
                  pl.BlockSpec((1, nk, 2 * KV_DIM), lambda bi: (bi, 0, 0))],
        out_specs=pl.BlockSpec((1, rows, KV_DIM), lambda bi: (bi, 0, 0)),
        out_shape=jax.ShapeDtypeStruct((b, rows, KV_DIM), F32),
        compiler_params=_params(("parallel",)),
        name="sample_win",
    )(q_bd, cache_win_l, tail_win)


def _heads_major(a2d, b, t, n):
    return a2d.reshape(b, t, n, HEAD_DIM).transpose(0, 2, 1, 3)


def _tokens_major(a_hm):
    b, n, t, d = a_hm.shape
    return a_hm.transpose(0, 2, 1, 3).reshape(b * t, n * d)


def _block_diag_q(q2d, b, t):
    q_hm = _heads_major(q2d, b, t, N_HEADS)
    onehot = jnp.asarray(np.eye(N_KV)[np.arange(N_HEADS) // N_REP], dtype=q2d.dtype)
    return jnp.einsum('bhtd,hg->bhtgd', q_hm, onehot).reshape(b, N_HEADS * t, KV_DIM)


def _own_group(o_bd, b, t):
    o = o_bd.reshape(b, N_KV, N_REP, t, N_KV, HEAD_DIM)
    o = jnp.stack([o[:, g, :, :, g, :] for g in range(N_KV)], axis=1)
    return o.transpose(0, 3, 1, 2, 4).reshape(b * t, Q_DIM)


def _nsa_prompt(x2d, b, t, g0, w_in_bf, cw):
    pos = jnp.arange(t)
    q, qr, cmp_rows, sel_rows, win_rows, gates = _nsa_proj(x2d, g0, w_in_bf, pos)
    assert t % STRIDE == 0 and t % L_SEL == 0
    n_ch = t // STRIDE
    n_cmp = n_ch - 1
    n_sel = t // L_SEL
    nsp = -(-n_sel // HEAD_DIM) * HEAD_DIM
    x_last = jnp.zeros((b, 8, _CHUNK_W), F32)
    (wk, pek, w1k, w2k), (wv, pev, w1v, w2v) = cw
    kcc, vcc = _compress(cmp_rows.reshape(b, n_ch, _CHUNK_W), x_last, wk, wv, pek, pev, w1k, w1v, w2k, w2v)
    gm = lambda a: a.reshape(b, -1, N_KV, HEAD_DIM).transpose(0, 2, 1, 3).astype(BF16)
    o_cmp, bias_t = _cmp_topk(_heads_major(q, b, t, N_HEADS), gm(kcc), gm(vcc), n_cmp, n_sel, nsp)
    qr_hm = _heads_major(qr, b, t, N_HEADS)
    bias = jnp.repeat(bias_t.transpose(0, 1, 3, 2), N_REP, axis=1)
    qp_hm = jnp.concatenate([qr_hm, bias], axis=-1)
    sel3 = sel_rows.reshape(b, t, 2 * KV_DIM)
    win3 = win_rows.reshape(b, t, 2 * KV_DIM)
    onehot = jnp.asarray(np.arange(t)[:, None] // L_SEL == np.arange(nsp)[None, :], dtype=BF16)
    kp_gm = jnp.concatenate([gm(sel3[..., :KV_DIM]), jnp.broadcast_to(onehot, (b, N_KV, t, nsp))], axis=-1)
    o_sel, o_win = _sel_win(qp_hm, qr_hm, kp_gm, gm(sel3[..., KV_DIM:]),
                            gm(win3[..., :KV_DIM]), gm(win3[..., KV_DIM:]))
    rows5 = lambda a: a.reshape(1, b, t, 2, N_KV, HEAD_DIM)
    n_win = min(WINDOW, t)
    caches = (rows5(cmp_rows), rows5(sel_rows), rows5(win_rows)[:, :, t - n_win:])
    return _tokens_major(o_cmp), _tokens_major(o_sel), _tokens_major(o_win), gates, caches


def _nsa_sample(x2d, b, t, g0, w_in_bf, cw, cache_cmp_l, cache_sel_l, cache_win_l, page_table):
    n_pages = page_table.shape[1]
    page = cache_cmp_l.shape[1]
    past_len = n_pages * page
    assert page % L_SEL == 0 and page % STRIDE == 0 and t <= STRIDE and t & (t - 1) == 0
    pos = past_len + jnp.arange(t)
    q, qr, cmp_rows, sel_rows, win_rows, gates = _nsa_proj(x2d, g0, w_in_bf, pos)
    row_w = 2 * KV_DIM
    past_cmp, past_sel = _gather_pages(page_table, cache_cmp_l.reshape(-1, page, row_w),
                                       cache_sel_l.reshape(-1, page, row_w))
    n_past_ch = past_len // STRIDE
    n_cmp = n_past_ch
    n_sel = past_len // L_SEL + 1
    nsp = -(-n_sel // LANES) * LANES
    new3 = lambda a: a.reshape(b, t, row_w)
    x_last = jnp.pad(new3(cmp_rows), ((0, 0), (0, STRIDE - t), (0, 0))).reshape(b, 1, _CHUNK_W)
    x_last = jnp.pad(x_last, ((0, 0), (0, 7), (0, 0)))
    (wk, pek, w1k, w2k), (wv, pev, w1v, w2v) = cw
    kcc, vcc = _compress(past_cmp.reshape(b, n_past_ch, _CHUNK_W), x_last, wk, wv, pek, pev, w1k, w1v, w2k, w2v)
    o_cmp_bd, sel01 = _sample_cmp(_block_diag_q(q, b, t), kcc, vcc, n_cmp, n_sel, nsp, past_len, t)
    qr_bd = _block_diag_q(qr, b, t)
    tail = lambda a: jnp.pad(new3(a), ((0, 0), (0, LANES - t), (0, 0)))
    o_sel_bd = _sample_sel(qr_bd, sel01, past_sel.reshape(b, past_len, row_w), tail(sel_rows), n_sel, t)
    win_cache = cache_win_l.reshape(b, -1, row_w)
    o_win_bd = _sample_win(qr_bd, win_cache, tail(win_rows), t)
    w_buf = win_cache.shape[1]
    new_win = jnp.concatenate([win_cache, new3(win_rows)], axis=1)[:, -w_buf:]
    rows5 = lambda a: a.reshape(1, b, -1, 2, N_KV, HEAD_DIM)
    caches = (rows5(cmp_rows), rows5(sel_rows), rows5(new_win))
    return _own_group(o_cmp_bd, b, t), _own_group(o_sel_bd, b, t), _own_group(o_win_bd, b, t), gates, caches


def kernel(x_prompt, x_sample, cache_cmp, cache_sel, cache_win, state_conv, page_table, p_prompt, p_sample,
           norm_g, w_ff1, w_ff2, w_ple, w_ple_gate, b_ple_gate,
           conv_w_pw1, conv_b_pw1, conv_w_dw, conv_b_dw, conv_ln_g, conv_ln_b, conv_w_pw2, conv_b_pw2,
           nsa_w_in, nsa_w_out, nsa_cmp_pe, nsa_cmp_w1, nsa_cmp_w2):
    depth = norm_g.shape[0]
    bf = lambda a: a.astype(BF16)
    row = lambda a: a.reshape(1, -1)
    w_ff1_bf, w_ff2_bf, w_ple_bf, w_gate_bf = bf(w_ff1), bf(w_ff2), bf(w_ple), bf(w_ple_gate)
    w_pw1_bf, w_pw2_bf, w_out_bf = bf(conv_w_pw1), bf(conv_w_pw2), bf(nsa_w_out)
    w_in_bf = bf(jnp.pad(nsa_w_in, ((0, 0), (0, 0), (0, _PROJ_W - nsa_w_in.shape[2]))))
    w_dw = jnp.pad(conv_w_dw, ((0, 0), (0, CONV_HALO - CONV_W), (0, 0)))

    def run(x, p, sample):
        b, t, _ = x.shape
        n = b * t
        x2d = x.reshape(n, D_MODEL)
        cmp_o, sel_o, win_o, conv_o = [], [], [], []
        for i in range(depth):
            g = lambda j: row(norm_g[i, j])
            if i % 2 == 0:
                c = i // 2
                u = _conv_front(x2d, g(0), w_pw1_bf[c], row(conv_b_pw1[c])).reshape(b, t, D_MODEL)
                if sample:
                    hist = jnp.pad(state_conv[c], ((0, 0), (CONV_HALO - (CONV_W - 1), 0), (0, 0)))
                else:
                    hist = jnp.zeros((b, CONV_HALO, D_MODEL), F32)
                tp = -(-t // CONV_HALO) * CONV_HALO
                ext = jnp.concatenate([hist, jnp.pad(u, ((0, 0), (0, tp - t), (0, 0)))], axis=1)
                x3 = jnp.pad(x2d.reshape(b, t, D_MODEL), ((0, 0), (0, tp - t), (0, 0)))
                x1 = _conv_back(ext, x3, w_dw[c], row(conv_b_dw[c]), row(conv_ln_g[c]), row(conv_ln_b[c]),
                                w_pw2_bf[c], row(conv_b_pw2[c]), g(1))[:, :t].reshape(n, D_MODEL)
                conv_o.append(ext[:, CONV_HALO + t - (CONV_W - 1):CONV_HALO + t][None])
            else:
                a = i // 2
                cw = _compress_weights(nsa_cmp_pe[a], nsa_cmp_w1[a], nsa_cmp_w2[a])
                if sample:
                    oc, osel, ow, gates, caches = _nsa_sample(x2d, b, t, g(0), w_in_bf[a], cw, cache_cmp[a],
                                                              cache_sel[a], cache_win[a], page_table)
                else:
                    oc, osel, ow, gates, caches = _nsa_prompt(x2d, b, t, g(0), w_in_bf[a], cw)
                for dst, rows_ in zip((cmp_o, sel_o, win_o), caches):
                    dst.append(rows_)
                x1 = _nsa_out(oc, osel, ow, gates, w_out_bf[a], x2d, g(1))
            x2d = _ffn(x1, g(2), w_ff1_bf[i], w_ff2_bf[i], g(3), w_gate_bf[i], row(b_ple_gate[i]),
                       p[i].reshape(n, -1), w_ple_bf[i])
        cat = lambda parts: jnp.concatenate(parts, axis=0)
        return x2d.reshape(b, t, D_MODEL), cat(cmp_o), cat(sel_o), cat(win_o), cat(conv_o)

    y_p, cmp_p, sel_p, win_p, conv_p = run(x_prompt, p_prompt, False)
    y_s, cmp_s, sel_s, win_s, conv_s = run(x_sample, p_sample, True)
    return (y_p, y_s, cmp_p, cmp_s, sel_p, sel_s, win_p, win_s, conv_p, conv_s)
```

```python
import functools
import math

import numpy as np
import jax
import jax.numpy as jnp
from jax import lax
from jax.experimental import pallas as pl
from jax.experimental.pallas import tpu as pltpu

F32 = jnp.float32
BF16 = jnp.bfloat16

D_MODEL = 1024
N_HEADS = 16
N_KV = 4
N_REP = N_HEADS // N_KV
HEAD_DIM = 64
ROT_DIM = HEAD_DIM // 4
ROPE_THETA = 500000.0
L_CMP = 32
STRIDE = 16
CMP_HID = 2 * HEAD_DIM
L_SEL = 64
N_TOP = 16
WINDOW = 512
CONV_W = 31
Q_DIM = N_HEADS * HEAD_DIM
KV_DIM = N_KV * HEAD_DIM
GATE_DIM = 3 * N_HEADS
D_FF = 4 * D_MODEL
EPS = 1e-6
NEG = -1e30
FORCE = 1e6
SCALE = HEAD_DIM ** -0.5

LANES = 128
CONV_HALO = 32
VMEM_LIMIT = 56 * 1024 * 1024


def _pick(n, cands):
    for c in cands:
        if n % c == 0:
            return c
    raise ValueError(f"no tile in {cands} divides {n}")


def _const_spec(shape):
    nd = len(shape)
    return pl.BlockSpec(shape, lambda *_: (0,) * nd, pipeline_mode=pl.Buffered(1))


def _params(sem):
    return pltpu.CompilerParams(dimension_semantics=sem, vmem_limit_bytes=VMEM_LIMIT)


def _sigmoid(x):
    return 1.0 / (1.0 + jnp.exp(-x))


def _rms(x, g):
    return x * lax.rsqrt(jnp.mean(x * x, axis=-1, keepdims=True) + EPS) * g


def _dot(a, b):
    return jnp.dot(a, b, preferred_element_type=F32)


def _dot_nt(a, b):
    return lax.dot_general(a, b, (((1,), (1,)), ((), ())), preferred_element_type=F32)


def _split_bf16(x):
    hi = x.astype(BF16)
    lo = (x - hi.astype(F32)).astype(BF16)
    return hi, lo


def _conv_front_body(x_ref, g_ref, w_ref, b_ref, u_ref):
    h = _rms(x_ref[...], g_ref[...]).astype(BF16)
    z = _dot(h, w_ref[...]) + b_ref[...]
    u_ref[...] = z[:, :D_MODEL] * _sigmoid(z[:, D_MODEL:])


def _conv_front(x2d, g, w_bf, b):
    n = x2d.shape[0]
    tm = _pick(n, (512, 256, 128, 64, 32, 16, 8))
    return pl.pallas_call(
        _conv_front_body,
        grid=(n // tm,),
        in_specs=[pl.BlockSpec((tm, D_MODEL), lambda i: (i, 0)),
                  _const_spec((1, D_MODEL)),
                  _const_spec((D_MODEL, 2 * D_MODEL)),
                  _const_spec((1, 2 * D_MODEL))],
        out_specs=pl.BlockSpec((tm, D_MODEL), lambda i: (i, 0)),
        out_shape=jax.ShapeDtypeStruct((n, D_MODEL), F32),
        compiler_params=_params(("parallel",)),
        name="conv_front",
    )(x2d, g, w_bf, b)


_CONV_ROWS = 16


def _conv_back_body(main_ref, tail_ref, wdw_ref, bdw_ref, lng_ref, lnb_ref, w2_ref, b2_ref, x_ref, g1_ref,
                    o_ref, win_ref, y_ref):
    tt = main_ref.shape[1]
    win_ref[0:tt, :] = main_ref[0]
    win_ref[tt:tt + CONV_HALO, :] = tail_ref[0]
    first = CONV_HALO - (CONV_W - 1)

    for r0 in range(0, tt, _CONV_ROWS):
        acc = jnp.zeros((_CONV_ROWS, D_MODEL), F32)
        for k in range(CONV_W):
            acc = acc + win_ref[r0 + first + k:r0 + first + k + _CONV_ROWS, :] * wdw_ref[k:k + 1, :]
        y_ref[r0:r0 + _CONV_ROWS, :] = acc + bdw_ref[...]
    y = y_ref[...]
    yc = y - jnp.mean(y, axis=-1, keepdims=True)
    var = jnp.mean(yc * yc, axis=-1, keepdims=True)
    ln = yc * lax.rsqrt(var + EPS) * lng_ref[...] + lnb_ref[...]
    act = (ln * _sigmoid(ln)).astype(BF16)
    m = _dot(act, w2_ref[...]) + b2_ref[...]
    o_ref[0] = x_ref[0] + _rms(m, g1_ref[...])


def _conv_back(ext, x3d, wdw, bdw, lng, lnb, w2_bf, b2, g1):
    b, t, _ = x3d.shape
    tt = _pick(t, (256, 128, 64, 32))
    halo_blocks = tt // CONV_HALO
    return pl.pallas_call(
        _conv_back_body,
        grid=(b, t // tt),
        in_specs=[pl.BlockSpec((1, tt, D_MODEL), lambda bi, i: (bi, i, 0)),
                  pl.BlockSpec((1, CONV_HALO, D_MODEL), lambda bi, i: (bi, (i + 1) * halo_blocks, 0)),
                  _const_spec((CONV_HALO, D_MODEL)),
                  _const_spec((1, D_MODEL)), _const_spec((1, D_MODEL)), _const_spec((1, D_MODEL)),
                  _const_spec((D_MODEL, D_MODEL)), _const_spec((1, D_MODEL)),
                  pl.BlockSpec((1, tt, D_MODEL), lambda bi, i: (bi, i, 0)),
                  _const_spec((1, D_MODEL))],
        out_specs=pl.BlockSpec((1, tt, D_MODEL), lambda bi, i: (bi, i, 0)),
        out_shape=jax.ShapeDtypeStruct((b, t, D_MODEL), F32),
        scratch_shapes=[pltpu.VMEM((tt + CONV_HALO, D_MODEL), F32), pltpu.VMEM((tt, D_MODEL), F32)],
        compiler_params=_params(("parallel", "parallel")),
        name="conv_back",
    )(ext, ext, wdw, bdw, lng, lnb, w2_bf, b2, x3d, g1)


_FF_CHUNK = 1024


def _ffn_body(x_ref, g2_ref, w1_ref, w2_ref, g3_ref, wg_ref, bg_ref, p_ref, wp_ref, o_ref):
    x = x_ref[...]
    h = _rms(x, g2_ref[...]).astype(BF16)
    f = jnp.zeros(x.shape, F32)
    for c in range(D_FF // _FF_CHUNK):
        a = jnp.maximum(_dot(h, w1_ref[:, c * _FF_CHUNK:(c + 1) * _FF_CHUNK]), 0.0)
        f = f + _dot((a * a).astype(BF16), w2_ref[c * _FF_CHUNK:(c + 1) * _FF_CHUNK, :])
    x2 = x + _rms(f, g3_ref[...])
    gate = _sigmoid(_dot(x2.astype(BF16), wg_ref[...]) + bg_ref[...])
    o_ref[...] = x2 + gate * _dot(p_ref[...].astype(BF16), wp_ref[...])


def _ffn(x2d, g2, w1_bf, w2_bf, g3, wg_bf, bg, p2d, wp_bf):
    n = x2d.shape[0]
    d_ple = p2d.shape[1]
    tm = _pick(n, (256, 128, 64, 32, 16, 8))
    return pl.pallas_call(
        _ffn_body,
        grid=(n // tm,),
        in_specs=[pl.BlockSpec((tm, D_MODEL), lambda i: (i, 0)),
                  _const_spec((1, D_MODEL)),
                  _const_spec((D_MODEL, D_FF)), _const_spec((D_FF, D_MODEL)),
                  _const_spec((1, D_MODEL)),
                  _const_spec((D_MODEL, D_MODEL)), _const_spec((1, D_MODEL)),
                  pl.BlockSpec((tm, d_ple), lambda i: (i, 0)),
                  _const_spec((d_ple, D_MODEL))],
        out_specs=pl.BlockSpec((tm, D_MODEL), lambda i: (i, 0)),
        out_shape=jax.ShapeDtypeStruct((n, D_MODEL), F32),
        compiler_params=_params(("parallel",)),
        name="ffn_ple",
    )(x2d, g2, w1_bf, w2_bf, g3, wg_bf, bg, p2d, wp_bf)


_PROJ_W = Q_DIM + 6 * KV_DIM + LANES


def _rope_block(blk, c, su, sd):
    return blk * c + pltpu.roll(blk, ROT_DIM // 2, 1) * su + pltpu.roll(blk, LANES - ROT_DIM // 2, 1) * sd


def _nsa_proj_body(x_ref, g_ref, w_ref, c_ref, su_ref, sd_ref,
                   q_ref, qr_ref, cmp_ref, sel_ref, win_ref, gate_ref):
    h = _rms(x_ref[...], g_ref[...]).astype(BF16)
    z = _dot(h, w_ref[...])
    c, su, sd = c_ref[...], su_ref[...], sd_ref[...]
    for i in range(Q_DIM // LANES):
        blk = z[:, i * LANES:(i + 1) * LANES]
        q_ref[:, i * LANES:(i + 1) * LANES] = (blk * SCALE).astype(BF16)
        qr_ref[:, i * LANES:(i + 1) * LANES] = (_rope_block(blk, c, su, sd) * SCALE).astype(BF16)
    o = Q_DIM
    cmp_ref[...] = z[:, o:o + 2 * KV_DIM]
    o += 2 * KV_DIM
    for dst in (sel_ref, win_ref):
        for i in range(KV_DIM // LANES):
            dst[:, i * LANES:(i + 1) * LANES] = _rope_block(z[:, o + i * LANES:o + (i + 1) * LANES], c, su, sd)
        dst[:, KV_DIM:] = z[:, o + KV_DIM:o + 2 * KV_DIM]
        o += 2 * KV_DIM
    gate_ref[...] = _sigmoid(z[:, o:o + LANES])


def _rope_tables(pos, rows):
    half = ROT_DIM // 2
    inv = jnp.float32(ROPE_THETA) ** (-jnp.arange(half, dtype=F32) * (2.0 / ROT_DIM))
    ang = pos.astype(F32)[:, None] * inv[None, :]
    lane = np.arange(LANES)
    within = lane % HEAD_DIM
    cos = jnp.cos(ang)[:, lane % half]
    sin = jnp.sin(ang)[:, lane % half]
    c = jnp.where(within[None, :] < ROT_DIM, cos, 1.0)
    su = jnp.where((within[None, :] >= half) & (within[None, :] < ROT_DIM), sin, 0.0)
    sd = jnp.where(within[None, :] < half, -sin, 0.0)
    reps = rows // pos.shape[0]
    return tuple(jnp.tile(a, (reps, 1)) for a in (c, su, sd))


def _nsa_proj(x2d, g, w_bf, pos):
    n = x2d.shape[0]
    t = pos.shape[0]
    tm = _pick(n, (256, 128, 64, 32, 16, 8))
    if t >= tm:
        assert t % tm == 0
        tabs = _rope_tables(pos, t)
        nt = t // tm
        tab_spec = pl.BlockSpec((tm, LANES), lambda i: (i % nt, 0))
    else:
        assert tm % t == 0
        tabs = _rope_tables(pos, tm)
        tab_spec = _const_spec((tm, LANES))
    row = lambda w: pl.BlockSpec((tm, w), lambda i: (i, 0))
    return pl.pallas_call(
        _nsa_proj_body,
        grid=(n // tm,),
        in_specs=[row(D_MODEL), _const_spec((1, D_MODEL)), _const_spec((D_MODEL, _PROJ_W)),
                  tab_spec, tab_spec, tab_spec],
        out_specs=[row(Q_DIM), row(Q_DIM), row(2 * KV_DIM), row(2 * KV_DIM), row(2 * KV_DIM), row(LANES)],
        out_shape=[jax.ShapeDtypeStruct((n, Q_DIM), BF16), jax.ShapeDtypeStruct((n, Q_DIM), BF16),
                   jax.ShapeDtypeStruct((n, 2 * KV_DIM), F32), jax.ShapeDtypeStruct((n, 2 * KV_DIM), F32),
                   jax.ShapeDtypeStruct((n, 2 * KV_DIM), F32), jax.ShapeDtypeStruct((n, LANES), F32)],
        compiler_params=_params(("parallel",)),
        name="nsa_proj",
    )(x2d, g, w_bf, *tabs)


_CHUNK_W = STRIDE * 2 * KV_DIM
_AB_W = 2 * N_KV * CMP_HID


def _chunk_rows_first_layer(src_ref, kv, w_ref):
    acc = jnp.zeros((src_ref.shape[1], _AB_W), F32)
    for s in range(STRIDE):
        lo = s * 2 * KV_DIM + kv * KV_DIM
        acc = acc + _dot(src_ref[0, :, lo:lo + KV_DIM].astype(BF16), w_ref[s])
    return acc


def _compress_finish(xl_ref, per_kv, n_rows):
    half = _AB_W // 2
    for kv, (ab_ref, w_ref, pe_ref, w1_ref, w2_ref, out_ref) in enumerate(per_kv):
        ab_ref[n_rows:n_rows + 8, :] = _chunk_rows_first_layer(xl_ref, kv, w_ref)
        c = jnp.sum(pe_ref[...] * w1_ref[...], axis=0, keepdims=True)
        c = jnp.concatenate([c] * N_KV, axis=1)
        pre = ab_ref[0:n_rows, 0:half] + ab_ref[1:n_rows + 1, half:] + c
        hid = (pre * _sigmoid(pre)).astype(BF16)
        out_ref[0] = _dot(hid, w2_ref[...])


def _compress_body(x_ref, xl_ref, wk_ref, wv_ref, pek_ref, pev_ref, w1k_ref, w1v_ref, w2k_ref, w2v_ref,
                   kcc_ref, vcc_ref, abk_ref, abv_ref):
    rt = x_ref.shape[1]
    i = pl.program_id(1)
    n_rt = pl.num_programs(1)
    r0 = pl.multiple_of(i * rt, rt)
    abk_ref[pl.ds(r0, rt), :] = _chunk_rows_first_layer(x_ref, 0, wk_ref)
    abv_ref[pl.ds(r0, rt), :] = _chunk_rows_first_layer(x_ref, 1, wv_ref)

    @pl.when(i == n_rt - 1)
    def _():
        _compress_finish(xl_ref, ((abk_ref, wk_ref, pek_ref, w1k_ref, w2k_ref, kcc_ref),
                                  (abv_ref, wv_ref, pev_ref, w1v_ref, w2v_ref, vcc_ref)), kcc_ref.shape[1])


def _compress(x_chunks, x_last, wk, wv, pek, pev, w1k, w1v, w2k, w2v):
    b, n, _ = x_chunks.shape
    rt = _pick(n, (128, 64, 32, 16, 8)) if n % 8 == 0 else n
    flat = L_CMP * HEAD_DIM
    out = jax.ShapeDtypeStruct((b, n, KV_DIM), F32)
    return pl.pallas_call(
        _compress_body,
        grid=(b, n // rt),
        in_specs=[pl.BlockSpec((1, rt, _CHUNK_W), lambda bi, i: (bi, i, 0)),
                  pl.BlockSpec((1, 8, _CHUNK_W), lambda bi, i: (bi, 0, 0)),
                  _const_spec((STRIDE, KV_DIM, _AB_W)), _const_spec((STRIDE, KV_DIM, _AB_W)),
                  _const_spec((flat, 1)), _const_spec((flat, 1)),
                  _const_spec((flat, CMP_HID)), _const_spec((flat, CMP_HID)),
                  _const_spec((N_KV * CMP_HID, KV_DIM)), _const_spec((N_KV * CMP_HID, KV_DIM))],
        out_specs=[pl.BlockSpec((1, n, KV_DIM), lambda bi, i: (bi, 0, 0))] * 2,
        out_shape=[out, out],
        scratch_shapes=[pltpu.VMEM((n + 8, _AB_W), F32), pltpu.VMEM((n + 8, _AB_W), F32)],
        compiler_params=_params(("parallel", "arbitrary")),
        name="compress",
    )(x_chunks, x_last, wk, wv, pek, pev, w1k, w1v, w2k, w2v)


def _compress_weights(cmp_pe, cmp_w1, cmp_w2):
    eye = jnp.eye(N_KV, dtype=F32)
    outs = []
    for kv in range(2):
        w1 = cmp_w1[kv].reshape(2, STRIDE, HEAD_DIM, CMP_HID)
        big = jnp.einsum('gk,asdh->sgdakh', eye, w1).reshape(STRIDE, KV_DIM, _AB_W)
        w2 = jnp.einsum('gk,hd->ghkd', eye, cmp_w2[kv]).reshape(N_KV * CMP_HID, KV_DIM)
        outs.append((big.astype(BF16), cmp_pe[kv].reshape(L_CMP * HEAD_DIM, 1),
                     cmp_w1[kv].reshape(L_CMP * HEAD_DIM, CMP_HID), w2.astype(BF16)))
    return outs


def _select_blocks(imp_t, blk, qpos, n_real, k):
    cur = jnp.right_shift(qpos, int(math.log2(L_SEL)))
    forced = (blk == 0) | (blk == cur) | (blk == cur - 1)
    future = blk * L_SEL > qpos
    score = jnp.where(future, -jnp.inf, jnp.where(forced, FORCE, imp_t))
    score = jnp.where(blk < n_real, score, -jnp.inf)
    rank = jnp.zeros(score.shape, jnp.int32)
    for j in range(n_real):
        sj = score[j:j + 1, :]
        rank = rank + jnp.where(blk > j, jnp.where(sj >= score, 1, 0), jnp.where(sj > score, 1, 0))
    return (rank < k) & (blk < n_real)


def _overlap_t(n_sel_pad, n_cmp_pad, n_sel, n_cmp):
    ci = np.arange(n_cmp_pad)[None, :] * STRIDE
    sj = np.arange(n_sel_pad)[:, None] * L_SEL
    ov = (ci < sj + L_SEL) & (ci + L_CMP > sj)
    ov &= (np.arange(n_cmp_pad)[None, :] < n_cmp) & (np.arange(n_sel_pad)[:, None] < n_sel)
    return jnp.asarray(ov, dtype=BF16)


def _cmp_topk_body(n_sel, q_ref, kcc_ref, vcc_ref, ovt_ref, o_ref, bias_ref):
    tq = q_ref.shape[2]
    nc = kcc_ref.shape[2]
    nsp = ovt_ref.shape[0]
    t0 = pl.program_id(1) * tq
    qpos_col = t0 + lax.broadcasted_iota(jnp.int32, (tq, 1), 0)
    cmp_end = lax.broadcasted_iota(jnp.int32, (1, nc), 1) * STRIDE + (L_CMP - 1)
    cmask = cmp_end <= qpos_col
    blk = lax.broadcasted_iota(jnp.int32, (nsp, tq), 0)
    qpos_t = t0 + lax.broadcasted_iota(jnp.int32, (nsp, tq), 1)
    ovt = ovt_ref[...]
    for g in range(N_KV):
        kg = kcc_ref[0, g]
        vg = vcc_ref[0, g]
        pg = jnp.zeros((tq, nc), F32)
        for r in range(N_REP):
            h = g * N_REP + r
            lm = jnp.where(cmask, _dot_nt(q_ref[0, h], kg), NEG)
            e = jnp.exp(lm - jnp.max(lm, axis=-1, keepdims=True))
            pc = jnp.where(cmask, e / jnp.sum(e, axis=-1, keepdims=True), 0.0)
            o_ref[0, h] = _dot(pc.astype(BF16), vg)
            pg = pg + pc
        hi, lo = _split_bf16(pg)
        imp_t = _dot_nt(ovt, hi) + _dot_nt(ovt, lo)
        sel = _select_blocks(imp_t, blk, qpos_t, n_sel, min(N_TOP, n_sel))
        bias_ref[0, g] = jnp.where(sel, 0.0, NEG).astype(BF16)


def _cmp_topk(q_hm, kcc_gm, vcc_gm, n_cmp, n_sel, nsp):
    b, _, t, _ = q_hm.shape
    nc = kcc_gm.shape[2]
    tq = _pick(t, (256, 128))
    ovt = _overlap_t(nsp, nc, n_sel, n_cmp)
    return pl.pallas_call(
        functools.partial(_cmp_topk_body, n_sel),
        grid=(b, t // tq),
        in_specs=[pl.BlockSpec((1, N_HEADS, tq, HEAD_DIM), lambda bi, i: (bi, 0, i, 0)),
                  pl.BlockSpec((1, N_KV, nc, HEAD_DIM), lambda bi, i: (bi, 0, 0, 0)),
                  pl.BlockSpec((1, N_KV, nc, HEAD_DIM), lambda bi, i: (bi, 0, 0, 0)),
                  _const_spec((nsp, nc))],
        out_specs=[pl.BlockSpec((1, N_HEADS, tq, HEAD_DIM), lambda bi, i: (bi, 0, i, 0)),
                   pl.BlockSpec((1, N_KV, nsp, tq), lambda bi, i: (bi, 0, 0, i))],
        out_shape=[jax.ShapeDtypeStruct((b, N_HEADS, t, HEAD_DIM), F32),
                   jax.ShapeDtypeStruct((b, N_KV, nsp, t), BF16)],
        compiler_params=_params(("parallel", "parallel")),
        name="cmp_topk",
    )(q_hm, kcc_gm, vcc_gm, ovt)


def _flash_step(s, v, m_ref, l_ref, acc_ref):
    m_prev = m_ref[...]
    m_new = jnp.maximum(m_prev, jnp.max(s, axis=-1, keepdims=True))
    alpha = jnp.exp(m_prev - m_new)
    p = jnp.exp(s - m_new)
    l_ref[...] = alpha * l_ref[...] + jnp.sum(p, axis=-1, keepdims=True)
    acc_ref[...] = alpha * acc_ref[...] + _dot(p.astype(BF16), v)
    m_ref[...] = m_new


def _flash_init(m_ref, l_ref, acc_ref):
    m_ref[...] = jnp.full(m_ref.shape, NEG, F32)
    l_ref[...] = jnp.zeros(l_ref.shape, F32)
    acc_ref[...] = jnp.zeros(acc_ref.shape, F32)


def _sel_win_body(qp_ref, qr_ref, kp_ref, vs_ref, kw_ref, vw_ref, osel_ref, owin_ref, m_ref, l_ref, acc_ref):
    tq = qp_ref.shape[2]
    rows = N_REP * tq
    qt = pl.program_id(2)
    t0 = qt * tq
    qpos = t0 + (lax.broadcasted_iota(jnp.int32, (rows, 1), 0) & (tq - 1))
    kiota = lax.broadcasted_iota(jnp.int32, (1, tq), 1)

    q = qp_ref[0].reshape(rows, qp_ref.shape[3])
    _flash_init(m_ref, l_ref, acc_ref)

    def sel_chunk(j, carry):
        k0 = pl.multiple_of(j * tq, tq)
        s = _dot_nt(q, kp_ref[0, 0, pl.ds(k0, tq), :])
        s = jnp.where(k0 + kiota <= qpos, s, NEG)
        _flash_step(s, vs_ref[0, 0, pl.ds(k0, tq), :], m_ref, l_ref, acc_ref)
        return carry

    lax.fori_loop(0, qt + 1, sel_chunk, 0)
    osel_ref[0] = (acc_ref[...] / l_ref[...]).reshape(N_REP, tq, HEAD_DIM)

    q = qr_ref[0].reshape(rows, HEAD_DIM)
    _flash_init(m_ref, l_ref, acc_ref)

    def win_chunk(k0):
        s = _dot_nt(q, kw_ref[0, 0, pl.ds(k0, tq), :])
        kpos = k0 + kiota
        s = jnp.where((kpos <= qpos) & (kpos > qpos - WINDOW), s, NEG)
        _flash_step(s, vw_ref[0, 0, pl.ds(k0, tq), :], m_ref, l_ref, acc_ref)

    win_chunk(pl.multiple_of(t0, tq))
    for back in range(1, WINDOW // tq + 1):
        @pl.when(qt >= back)
        def _():
            win_chunk(pl.multiple_of(t0 - back * tq, tq))

    owin_ref[0] = (acc_ref[...] / l_ref[...]).reshape(N_REP, tq, HEAD_DIM)


def _sel_win(qp_hm, qr_hm, kp_gm, vs_gm, kw_gm, vw_gm):
    b, _, t, cw = qp_hm.shape
    tq = _pick(t, (256, 128))
    assert WINDOW % tq == 0
    rows = N_REP * tq
    qspec = lambda w: pl.BlockSpec((1, N_REP, tq, w), lambda bi, g, i: (bi, g, i, 0))
    kspec = lambda w: pl.BlockSpec((1, 1, t, w), lambda bi, g, i: (bi, g, 0, 0))
    out = jax.ShapeDtypeStruct((b, N_HEADS, t, HEAD_DIM), F32)
    return pl.pallas_call(
        _sel_win_body,
        grid=(b, N_KV, t // tq),
        in_specs=[qspec(cw), qspec(HEAD_DIM), kspec(cw), kspec(HEAD_DIM), kspec(HEAD_DIM), kspec(HEAD_DIM)],
        out_specs=[qspec(HEAD_DIM), qspec(HEAD_DIM)],
        out_shape=[out, out],
        scratch_shapes=[pltpu.VMEM((rows, 1), F32), pltpu.VMEM((rows, 1), F32), pltpu.VMEM((rows, HEAD_DIM), F32)],
        compiler_params=_params(("parallel", "parallel", "parallel")),
        name="sel_win",
    )(qp_hm, qr_hm, kp_gm, vs_gm, kw_gm, vw_gm)


def _nsa_out_body(oc_ref, os_ref, ow_ref, gate_ref, e_ref, w_ref, x_ref, g1_ref, o_ref):
    hi, lo = _split_bf16(gate_ref[...])
    o = jnp.zeros(oc_ref.shape, F32)
    for c, src in enumerate((oc_ref, os_ref, ow_ref)):
        o = o + (_dot(hi, e_ref[c]) + _dot(lo, e_ref[c])) * src[...]
    m = _dot(o.astype(BF16), w_ref[...])
    o_ref[...] = x_ref[...] + _rms(m, g1_ref[...])


def _gate_expand():
    e = np.zeros((3, LANES, Q_DIM), np.float32)
    for h in range(N_HEADS):
        for c in range(3):
            e[c, h * 3 + c, h * HEAD_DIM:(h + 1) * HEAD_DIM] = 1.0
    return jnp.asarray(e, dtype=BF16)


def _nsa_out(o_cmp, o_sel, o_win, gates, w_bf, x2d, g1):
    n = x2d.shape[0]
    tm = _pick(n, (256, 128, 64, 32, 16, 8))
    row = lambda w: pl.BlockSpec((tm, w), lambda i: (i, 0))
    return pl.pallas_call(
        _nsa_out_body,
        grid=(n // tm,),
        in_specs=[row(Q_DIM), row(Q_DIM), row(Q_DIM), row(LANES), _const_spec((3, LANES, Q_DIM)),
                  _const_spec((Q_DIM, D_MODEL)), row(D_MODEL), _const_spec((1, D_MODEL))],
        out_specs=row(D_MODEL),
        out_shape=jax.ShapeDtypeStruct((n, D_MODEL), F32),
        compiler_params=_params(("parallel",)),
        name="nsa_out",
    )(o_cmp, o_sel, o_win, gates, _gate_expand(), w_bf, x2d, g1)


class _PageRing:
    def __init__(self, pt_ref, cache_hbm, buf_ref, sem_ref, n_b, n_c):
        self.pt, self.cache, self.buf, self.sem = pt_ref, cache_hbm, buf_ref, sem_ref
        self.n_b, self.n_c, self.pg = n_b, n_c, buf_ref.shape[1]

    def _copies(self, step):
        b, c, slot = step // self.n_c, step % self.n_c, step % 2
        return [pltpu.make_async_copy(self.cache.at[self.pt[b, c * self.pg + p]], self.buf.at[slot, p],
                                      self.sem.at[slot]) for p in range(self.pg)]

    def acquire(self, b, c):
        step = b * self.n_c + c

        @pl.when(step == 0)
        def _():
            for cp in self._copies(step):
                cp.start()

        @pl.when(step + 1 < self.n_b * self.n_c)
        def _():
            for cp in self._copies(step + 1):
                cp.start()

        for cp in self._copies(step):
            cp.wait()
        return step % 2


def _compress_paged_body(n_b, n_c, pt_ref, cache_hbm, xl_ref, wk_ref, wv_ref, pek_ref, pev_ref, w1k_ref, w1v_ref,
                         w2k_ref, w2v_ref, kcc_ref, vcc_ref, buf_ref, sem_ref, xt_ref, abk_ref, abv_ref):
    pg, page = buf_ref.shape[1], buf_ref.shape[4]
    rows = pg * page // STRIDE
    b, c = pl.program_id(0), pl.program_id(1)
    slot = _PageRing(pt_ref, cache_hbm, buf_ref, sem_ref, n_b, n_c).acquire(b, c)
    r0 = pl.multiple_of(c * rows, rows)
    for kv, (ab_ref, w_ref) in enumerate(((abk_ref, wk_ref), (abv_ref, wv_ref))):
        for p in range(pg):
            for j in range(KV_DIM // LANES):
                xt_ref[j, p * page:(p + 1) * page, :] = buf_ref[slot, p, kv, j * LANES:(j + 1) * LANES, :].T
        acc = jnp.zeros((rows, _AB_W), F32)
        for s in range(STRIDE):
            xs = jnp.concatenate([xt_ref[j, pl.ds(s, rows, stride=STRIDE), :] for j in range(KV_DIM // LANES)],
                                 axis=1)
            acc = acc + _dot(xs.astype(BF16), w_ref[s])
        ab_ref[pl.ds(r0, rows), :] = acc

    @pl.when(c == n_c - 1)
    def _():
        _compress_finish(xl_ref, ((abk_ref, wk_ref, pek_ref, w1k_ref, w2k_ref, kcc_ref),
                                  (abv_ref, wv_ref, pev_ref, w1v_ref, w2v_ref, vcc_ref)), kcc_ref.shape[1])


def _compress_paged(page_table, cache_fm, x_last, wk, wv, pek, pev, w1k, w1v, w2k, w2v):
    b, n_pages = page_table.shape
    page = cache_fm.shape[3]
    pg = _pick(n_pages, (16, 8, 4, 2, 1))
    n_c = n_pages // pg
    n = n_pages * page // STRIDE
    flat = L_CMP * HEAD_DIM
    out = jax.ShapeDtypeStruct((b, n, KV_DIM), F32)
    const = _const_spec
    return pl.pallas_call(
        functools.partial(_compress_paged_body, b, n_c),
        grid_spec=pltpu.PrefetchScalarGridSpec(
            num_scalar_prefetch=1,
            grid=(b, n_c),
            in_specs=[pl.BlockSpec(memory_space=pl.ANY),
                      pl.BlockSpec((1, 8, _CHUNK_W), lambda bi, c, pt: (bi, 0, 0)),
                      const((STRIDE, KV_DIM, _AB_W)), const((STRIDE, KV_DIM, _AB_W)),
                      const((flat, 1)), const((flat, 1)), const((flat, CMP_HID)), const((flat, CMP_HID)),
                      const((N_KV * CMP_HID, KV_DIM)), const((N_KV * CMP_HID, KV_DIM))],
            out_specs=[pl.BlockSpec((1, n, KV_DIM), lambda bi, c, pt: (bi, 0, 0))] * 2,
            scratch_shapes=[pltpu.VMEM((2, pg, 2, KV_DIM, page), F32), pltpu.SemaphoreType.DMA((2,)),
                            pltpu.VMEM((KV_DIM // LANES, pg * page, LANES), F32),
                            pltpu.VMEM((n + 8, _AB_W), F32), pltpu.VMEM((n + 8, _AB_W), F32)]),
        out_shape=[out, out],
        compiler_params=_params(("arbitrary", "arbitrary")),
        name="compress_paged",
    )(page_table, cache_fm, x_last, wk, wv, pek, pev, w1k, w1v, w2k, w2v)


def _softmax_rows(s):
    e = jnp.exp(s - jnp.max(s, axis=-1, keepdims=True))
    return e / jnp.sum(e, axis=-1, keepdims=True)


def _sample_cmp_body(n_sel, past_len, t_new, q_ref, kcc_ref, vcc_ref, ovt_ref, rep_ref, o_ref, sel_ref):
    rows = q_ref.shape[1]
    nc = kcc_ref.shape[1]
    nsp = ovt_ref.shape[0]
    gq = N_KV * t_new
    tq_col = lax.broadcasted_iota(jnp.int32, (rows, 1), 0) & (t_new - 1)
    cmp_end = lax.broadcasted_iota(jnp.int32, (1, nc), 1) * STRIDE + (L_CMP - 1)
    cmask = cmp_end <= past_len + tq_col
    lm = jnp.where(cmask, _dot_nt(q_ref[0], kcc_ref[0].astype(BF16)), NEG)
    pc = jnp.where(cmask, _softmax_rows(lm), 0.0)
    o_ref[0] = _dot(pc.astype(BF16), vcc_ref[0].astype(BF16))
    pg = pc.reshape(N_KV, N_REP, t_new, nc).sum(axis=1).reshape(gq, nc)
    hi, lo = _split_bf16(pg)
    imp_t = _dot_nt(ovt_ref[...], hi) + _dot_nt(ovt_ref[...], lo)
    blk = lax.broadcasted_iota(jnp.int32, (nsp, gq), 0)
    qpos = past_len + (lax.broadcasted_iota(jnp.int32, (nsp, gq), 1) & (t_new - 1))
    sel = _select_blocks(imp_t, blk, qpos, n_sel, min(N_TOP, n_sel))
    sel_ref[0] = _dot_nt(rep_ref[...], jnp.where(sel, 1.0, 0.0).astype(BF16)).astype(BF16)


def _sample_cmp(q_bd, kcc, vcc, n_cmp, n_sel, nsp, past_len, t_new):
    b, rows, _ = q_bd.shape
    nc = kcc.shape[1]
    gq = N_KV * t_new
    ovt = _overlap_t(nsp, nc, n_sel, n_cmp)
    rep = np.zeros((rows, gq), np.float32)
    for h in range(N_HEADS):
        for t in range(t_new):
            rep[h * t_new + t, (h // N_REP) * t_new + t] = 1.0
    return pl.pallas_call(
        functools.partial(_sample_cmp_body, n_sel, past_len, t_new),
        grid=(b,),
        in_specs=[pl.BlockSpec((1, rows, KV_DIM), lambda bi: (bi, 0, 0)),
                  pl.BlockSpec((1, nc, KV_DIM), lambda bi: (bi, 0, 0)),
                  pl.BlockSpec((1, nc, KV_DIM), lambda bi: (bi, 0, 0)),
                  _const_spec((nsp, nc)), _const_spec((rows, gq))],
        out_specs=[pl.BlockSpec((1, rows, KV_DIM), lambda bi: (bi, 0, 0)),
                   pl.BlockSpec((1, rows, nsp), lambda bi: (bi, 0, 0))],
        out_shape=[jax.ShapeDtypeStruct((b, rows, KV_DIM), F32), jax.ShapeDtypeStruct((b, rows, nsp), BF16)],
        compiler_params=_params(("parallel",)),
        name="sample_cmp",
    )(q_bd, kcc, vcc, ovt, jnp.asarray(rep, dtype=BF16))


def _flash_update(s, pv_fn, m_ref, l_ref, acc_ref):
    m_prev = m_ref[...]
    m_new = jnp.maximum(m_prev, jnp.max(s, axis=-1, keepdims=True))
    alpha = jnp.exp(m_prev - m_new)
    p = jnp.exp(s - m_new)
    l_ref[...] = alpha * l_ref[...] + jnp.sum(p, axis=-1, keepdims=True)
    acc_ref[...] = alpha * acc_ref[...] + pv_fn(p.astype(BF16))
    m_ref[...] = m_new


def _sample_sel_body(t_new, n_b, n_c, pt_ref, cache_hbm, q_ref, sel_ref, e_ref, et_ref, tail_ref, o_ref,
                     buf_ref, sem_ref, m_ref, l_ref, acc_ref):
    rows = q_ref.shape[1]
    pg, page = buf_ref.shape[1], buf_ref.shape[4]
    b, c = pl.program_id(0), pl.program_id(1)
    q = q_ref[0]

    @pl.when(c == 0)
    def _():
        _flash_init(m_ref, l_ref, acc_ref)

    @pl.when(c < n_c)
    def _():
        slot = _PageRing(pt_ref, cache_hbm, buf_ref, sem_ref, n_b, n_c).acquire(b, c)
        s = jnp.concatenate([_dot(q, buf_ref[slot, p, 0].astype(BF16)) for p in range(pg)], axis=1)
        s = jnp.where(_dot(sel_ref[0], e_ref[...]) > 0.5, s, NEG)

        def pv(p_bf):
            return sum(_dot_nt(p_bf[:, p * page:(p + 1) * page], buf_ref[slot, p, 1].astype(BF16))
                       for p in range(pg))

        _flash_update(s, pv, m_ref, l_ref, acc_ref)

    @pl.when(c == n_c)
    def _():
        kv = tail_ref[0]
        nk = kv.shape[0]
        s = _dot_nt(q, kv[:, :KV_DIM].astype(BF16))
        tq = lax.broadcasted_iota(jnp.int32, (rows, 1), 0) & (t_new - 1)
        ok = (_dot(sel_ref[0], et_ref[...]) > 0.5) & (lax.broadcasted_iota(jnp.int32, (1, nk), 1) <= tq)
        _flash_update(jnp.where(ok, s, NEG), lambda p_bf: _dot(p_bf, kv[:, KV_DIM:].astype(BF16)),
                      m_ref, l_ref, acc_ref)
        o_ref[0] = acc_ref[...] / l_ref[...]


def _sample_sel(page_table, cache_fm, q_bd, sel01, tail_sel, n_sel, t_new):
    b, rows, _ = q_bd.shape
    nsp = sel01.shape[2]
    n_pages = page_table.shape[1]
    page = cache_fm.shape[3]
    pg = _pick(n_pages, (8, 4, 2, 1))
    n_c = n_pages // pg
    kc = pg * page
    nk = tail_sel.shape[1]
    key_blk = np.arange(n_pages * page) // L_SEL
    e = (np.arange(nsp)[:, None] == key_blk[None, :]).astype(np.float32)
    et = np.zeros((nsp, nk), np.float32)
    et[n_sel - 1, :] = 1.0
    return pl.pallas_call(
        functools.partial(_sample_sel_body, t_new, b, n_c),
        grid_spec=pltpu.PrefetchScalarGridSpec(
            num_scalar_prefetch=1,
            grid=(b, n_c + 1),
            in_specs=[pl.BlockSpec(memory_space=pl.ANY),
                      pl.BlockSpec((1, rows, KV_DIM), lambda bi, c, pt: (bi, 0, 0)),
                      pl.BlockSpec((1, rows, nsp), lambda bi, c, pt: (bi, 0, 0)),
                      pl.BlockSpec((nsp, kc), lambda bi, c, pt: (0, jnp.minimum(c, n_c - 1))),
                      _const_spec((nsp, nk)),
                      pl.BlockSpec((1, nk, 2 * KV_DIM), lambda bi, c, pt: (bi, 0, 0))],
            out_specs=pl.BlockSpec((1, rows, KV_DIM), lambda bi, c, pt: (bi, 0, 0)),
            scratch_shapes=[pltpu.VMEM((2, pg, 2, KV_DIM, page), F32), pltpu.SemaphoreType.DMA((2,)),
                            pltpu.VMEM((rows, 1), F32), pltpu.VMEM((rows, 1), F32),
                            pltpu.VMEM((rows, KV_DIM), F32)]),
        out_shape=jax.ShapeDtypeStruct((b, rows, KV_DIM), F32),
        compiler_params=_params(("arbitrary", "arbitrary")),
        name="sample_sel",
    )(page_table, cache_fm, q_bd, sel01, jnp.asarray(e, dtype=BF16), jnp.asarray(et, dtype=BF16), tail_sel)


def _sample_win_body(t_new, q_ref, kv_ref, tail_ref, o_ref):
    rows = q_ref.shape[1]
    wb = kv_ref.shape[3]
    nk = tail_ref.shape[1]
    q = q_ref[0]
    tq = lax.broadcasted_iota(jnp.int32, (rows, 1), 0) & (t_new - 1)
    tail = tail_ref[0]
    s_old = jnp.where(lax.broadcasted_iota(jnp.int32, (1, wb), 1) > tq - WINDOW + wb,
                      _dot(q, kv_ref[0, 0].astype(BF16)), NEG)
    s_new = jnp.where(lax.broadcasted_iota(jnp.int32, (1, nk), 1) <= tq,
                      _dot_nt(q, tail[:, :KV_DIM].astype(BF16)), NEG)
    m = jnp.maximum(jnp.max(s_old, axis=-1, keepdims=True), jnp.max(s_new, axis=-1, keepdims=True))
    p_old = jnp.exp(s_old - m)
    p_new = jnp.exp(s_new - m)
    den = jnp.sum(p_old, axis=-1, keepdims=True) + jnp.sum(p_new, axis=-1, keepdims=True)
    o = _dot_nt(p_old.astype(BF16), kv_ref[0, 1].astype(BF16)) + _dot(p_new.astype(BF16), tail[:, KV_DIM:].astype(BF16))
    o_ref[0] = o / den


def _sample_win(q_bd, win_fm, tail_win, t_new):
    b, rows, _ = q_bd.shape
    wb = win_fm.shape[3]
    nk = tail_win.shape[1]
    return pl.pallas_call(
        functools.partial(_sample_win_body, t_new),
        grid=(b,),
        in_specs=[pl.BlockSpec((1, rows, KV_DIM), lambda bi: (bi, 0, 0)),
                  pl.BlockSpec((1, 2, KV_DIM, wb), lambda bi: (bi, 0, 0, 0)),
                  pl.BlockSpec((1, nk, 2 * KV_DIM), lambda bi: (bi, 0, 0))],
        out_specs=pl.BlockSpec((1, rows, KV_DIM), lambda bi: (bi, 0, 0)),
        out_shape=jax.ShapeDtypeStruct((b, rows, KV_DIM), F32),
        compiler_params=_params(("parallel",)),
        name="sample_win",
    )(q_bd, win_fm, tail_win)


def _heads_major(a2d, b, t, n):
    return a2d.reshape(b, t, n, HEAD_DIM).transpose(0, 2, 1, 3)


def _tokens_major(a_hm):
    b, n, t, d = a_hm.shape
    return a_hm.transpose(0, 2, 1, 3).reshape(b * t, n * d)


def _block_diag_q(q2d, b, t):
    q_hm = _heads_major(q2d, b, t, N_HEADS)
    onehot = jnp.asarray(np.eye(N_KV)[np.arange(N_HEADS) // N_REP], dtype=q2d.dtype)
    return jnp.einsum('bhtd,hg->bhtgd', q_hm, onehot).reshape(b, N_HEADS * t, KV_DIM)


def _own_group(o_bd, b, t):
    o = o_bd.reshape(b, N_KV, N_REP, t, N_KV, HEAD_DIM)
    o = jnp.stack([o[:, g, :, :, g, :] for g in range(N_KV)], axis=1)
    return o.transpose(0, 3, 1, 2, 4).reshape(b * t, Q_DIM)


def _nsa_prompt(x2d, b, t, g0, w_in_bf, cw):
    pos = jnp.arange(t)
    q, qr, cmp_rows, sel_rows, win_rows, gates = _nsa_proj(x2d, g0, w_in_bf, pos)
    assert t % STRIDE == 0 and t % L_SEL == 0
    n_ch = t // STRIDE
    n_cmp = n_ch - 1
    n_sel = t // L_SEL
    nsp = -(-n_sel // HEAD_DIM) * HEAD_DIM
    x_last = jnp.zeros((b, 8, _CHUNK_W), F32)
    (wk, pek, w1k, w2k), (wv, pev, w1v, w2v) = cw
    kcc, vcc = _compress(cmp_rows.reshape(b, n_ch, _CHUNK_W), x_last, wk, wv, pek, pev, w1k, w1v, w2k, w2v)
    gm = lambda a: a.reshape(b, -1, N_KV, HEAD_DIM).transpose(0, 2, 1, 3).astype(BF16)
    o_cmp, bias_t = _cmp_topk(_heads_major(q, b, t, N_HEADS), gm(kcc), gm(vcc), n_cmp, n_sel, nsp)
    qr_hm = _heads_major(qr, b, t, N_HEADS)
    bias = jnp.repeat(bias_t.transpose(0, 1, 3, 2), N_REP, axis=1)
    qp_hm = jnp.concatenate([qr_hm, bias], axis=-1)
    sel3 = sel_rows.reshape(b, t, 2 * KV_DIM)
    win3 = win_rows.reshape(b, t, 2 * KV_DIM)
    onehot = jnp.asarray(np.arange(t)[:, None] // L_SEL == np.arange(nsp)[None, :], dtype=BF16)
    kp_gm = jnp.concatenate([gm(sel3[..., :KV_DIM]), jnp.broadcast_to(onehot, (b, N_KV, t, nsp))], axis=-1)
    o_sel, o_win = _sel_win(qp_hm, qr_hm, kp_gm, gm(sel3[..., KV_DIM:]),
                            gm(win3[..., :KV_DIM]), gm(win3[..., KV_DIM:]))
    rows5 = lambda a: a.reshape(1, b, t, 2, N_KV, HEAD_DIM)
    n_win = min(WINDOW, t)
    caches = (rows5(cmp_rows), rows5(sel_rows), rows5(win_rows)[:, :, t - n_win:])
    return _tokens_major(o_cmp), _tokens_major(o_sel), _tokens_major(o_win), gates, caches


def _nsa_sample(x2d, b, t, g0, w_in_bf, cw, cache_cmp_l, cache_sel_l, cache_win_l, page_table):
    n_pages = page_table.shape[1]
    page = cache_cmp_l.shape[1]
    past_len = n_pages * page
    assert page % L_SEL == 0 and page % STRIDE == 0 and t <= STRIDE and t & (t - 1) == 0
    pos = past_len + jnp.arange(t)
    q, qr, cmp_rows, sel_rows, win_rows, gates = _nsa_proj(x2d, g0, w_in_bf, pos)
    row_w = 2 * KV_DIM
    fm = lambda a: a.transpose(0, 2, 3, 4, 1).reshape(a.shape[0], 2, KV_DIM, a.shape[1])
    n_past_ch = past_len // STRIDE
    n_cmp = n_past_ch
    n_sel = past_len // L_SEL + 1
    nsp = -(-n_sel // LANES) * LANES
    new3 = lambda a: a.reshape(b, t, row_w)
    x_last = jnp.pad(new3(cmp_rows), ((0, 0), (0, STRIDE - t), (0, 0))).reshape(b, 1, _CHUNK_W)
    x_last = jnp.pad(x_last, ((0, 0), (0, 7), (0, 0)))
    (wk, pek, w1k, w2k), (wv, pev, w1v, w2v) = cw
    kcc, vcc = _compress_paged(page_table, fm(cache_cmp_l), x_last, wk, wv, pek, pev, w1k, w1v, w2k, w2v)
    o_cmp_bd, sel01 = _sample_cmp(_block_diag_q(q, b, t), kcc, vcc, n_cmp, n_sel, nsp, past_len, t)
    qr_bd = _block_diag_q(qr, b, t)
    tail = lambda a: jnp.pad(new3(a), ((0, 0), (0, LANES - t), (0, 0)))
    o_sel_bd = _sample_sel(page_table, fm(cache_sel_l), qr_bd, sel01, tail(sel_rows), n_sel, t)
    win_fm = fm(cache_win_l)
    o_win_bd = _sample_win(qr_bd, win_fm, tail(win_rows), t)
    w_buf = win_fm.shape[3]
    new_fm = new3(win_rows).reshape(b, t, 2, KV_DIM).transpose(0, 2, 3, 1)
    new_win = jnp.concatenate([win_fm, new_fm], axis=3)[..., -w_buf:]
    new_win = new_win.reshape(b, 2, N_KV, HEAD_DIM, w_buf).transpose(0, 4, 1, 2, 3)
    rows5 = lambda a: a.reshape(1, b, -1, 2, N_KV, HEAD_DIM)
    caches = (rows5(cmp_rows), rows5(sel_rows), new_win[None])
    return _own_group(o_cmp_bd, b, t), _own_group(o_sel_bd, b, t), _own_group(o_win_bd, b, t), gates, caches


def kernel(x_prompt, x_sample, cache_cmp, cache_sel, cache_win, state_conv, page_table, p_prompt, p_sample,
           norm_g, w_ff1, w_ff2, w_ple, w_ple_gate, b_ple_gate,
           conv_w_pw1, conv_b_pw1, conv_w_dw, conv_b_dw, conv_ln_g, conv_ln_b, conv_w_pw2, conv_b_pw2,
           nsa_w_in, nsa_w_out, nsa_cmp_pe, nsa_cmp_w1, nsa_cmp_w2):
    depth = norm_g.shape[0]
    bf = lambda a: a.astype(BF16)
    row = lambda a: a.reshape(1, -1)
    w_ff1_bf, w_ff2_bf, w_ple_bf, w_gate_bf = bf(w_ff1), bf(w_ff2), bf(w_ple), bf(w_ple_gate)
    w_pw1_bf, w_pw2_bf, w_out_bf = bf(conv_w_pw1), bf(conv_w_pw2), bf(nsa_w_out)
    w_in_bf = bf(jnp.pad(nsa_w_in, ((0, 0), (0, 0), (0, _PROJ_W - nsa_w_in.shape[2]))))
    w_dw = jnp.pad(conv_w_dw, ((0, 0), (0, CONV_HALO - CONV_W), (0, 0)))

    def run(x, p, sample):
        b, t, _ = x.shape
        n = b * t
        x2d = x.reshape(n, D_MODEL)
        cmp_o, sel_o, win_o, conv_o = [], [], [], []
        for i in range(depth):
            g = lambda j: row(norm_g[i, j])
            if i % 2 == 0:
                c = i // 2
                u = _conv_front(x2d, g(0), w_pw1_bf[c], row(conv_b_pw1[c])).reshape(b, t, D_MODEL)
                if sample:
                    hist = jnp.pad(state_conv[c], ((0, 0), (CONV_HALO - (CONV_W - 1), 0), (0, 0)))
                else:
                    hist = jnp.zeros((b, CONV_HALO, D_MODEL), F32)
                tp = -(-t // CONV_HALO) * CONV_HALO
                ext = jnp.concatenate([hist, jnp.pad(u, ((0, 0), (0, tp - t), (0, 0)))], axis=1)
                x3 = jnp.pad(x2d.reshape(b, t, D_MODEL), ((0, 0), (0, tp - t), (0, 0)))
                x1 = _conv_back(ext, x3, w_dw[c], row(conv_b_dw[c]), row(conv_ln_g[c]), row(conv_ln_b[c]),
                                w_pw2_bf[c], row(conv_b_pw2[c]), g(1))[:, :t].reshape(n, D_MODEL)
                conv_o.append(ext[:, CONV_HALO + t - (CONV_W - 1):CONV_HALO + t][None])
            else:
                a = i // 2
                cw = _compress_weights(nsa_cmp_pe[a], nsa_cmp_w1[a], nsa_cmp_w2[a])
                if sample:
                    oc, osel, ow, gates, caches = _nsa_sample(x2d, b, t, g(0), w_in_bf[a], cw, cache_cmp[a],
                                                              cache_sel[a], cache_win[a], page_table)
                else:
                    oc, osel, ow, gates, caches = _nsa_prompt(x2d, b, t, g(0), w_in_bf[a], cw)
                for dst, rows_ in zip((cmp_o, sel_o, win_o), caches):
                    dst.append(rows_)
                x1 = _nsa_out(oc, osel, ow, gates, w_out_bf[a], x2d, g(1))
            x2d = _ffn(x1, g(2), w_ff1_bf[i], w_ff2_bf[i], g(3), w_gate_bf[i], row(b_ple_gate[i]),
                       p[i].reshape(n, -1), w_ple_bf[i])
        cat = lambda parts: jnp.concatenate(parts, axis=0)
        return x2d.reshape(b, t, D_MODEL), cat(cmp_o), cat(sel_o), cat(win_o), cat(conv_o)

    y_p, cmp_p, sel_p, win_p, conv_p = run(x_prompt, p_prompt, False)
    y_s, cmp_s, sel_s, win_s, conv_s = run(x_sample, p_sample, True)
    return (y_p, y_s, cmp_p, cmp_s, sel_p, sel_s, win_p, win_s, conv_p, conv_s)
```

```python
import functools
import math

import numpy as np
import jax
import jax.numpy as jnp
from jax import lax
from jax.experimental import pallas as pl
from jax.experimental.pallas import tpu as pltpu

F32 = jnp.float32
BF16 = jnp.bfloat16

D_MODEL = 1024
N_HEADS = 16
N_KV = 4
N_REP = N_HEADS // N_KV
HEAD_DIM = 64
ROT_DIM = HEAD_DIM // 4
ROPE_THETA = 500000.0
L_CMP = 32
STRIDE = 16
CMP_HID = 2 * HEAD_DIM
L_SEL = 64
N_TOP = 16
WINDOW = 512
CONV_W = 31
Q_DIM = N_HEADS * HEAD_DIM
KV_DIM = N_KV * HEAD_DIM
GATE_DIM = 3 * N_HEADS
D_FF = 4 * D_MODEL
EPS = 1e-6
NEG = -1e30
FORCE = 1e6
SCALE = HEAD_DIM ** -0.5

LANES = 128
CONV_HALO = 32
VMEM_LIMIT = 56 * 1024 * 1024


def _pick(n, cands):
    for c in cands:
        if n % c == 0:
            return c
    raise ValueError(f"no tile in {cands} divides {n}")


def _const_spec(shape):
    nd = len(shape)
    return pl.BlockSpec(shape, lambda *_: (0,) * nd, pipeline_mode=pl.Buffered(1))


def _params(sem):
    return pltpu.CompilerParams(dimension_semantics=sem, vmem_limit_bytes=VMEM_LIMIT)


def _sigmoid(x):
    return 1.0 / (1.0 + jnp.exp(-x))


def _rms(x, g):
    return x * lax.rsqrt(jnp.mean(x * x, axis=-1, keepdims=True) + EPS) * g


def _dot(a, b):
    return jnp.dot(a, b, preferred_element_type=F32)


def _dot_nt(a, b):
    return lax.dot_general(a, b, (((1,), (1,)), ((), ())), preferred_element_type=F32)


def _split_bf16(x):
    hi = x.astype(BF16)
    lo = (x - hi.astype(F32)).astype(BF16)
    return hi, lo


def _conv_front_body(x_ref, g_ref, w_ref, b_ref, u_ref):
    h = _rms(x_ref[...], g_ref[...]).astype(BF16)
    z = _dot(h, w_ref[...]) + b_ref[...]
    u_ref[...] = z[:, :D_MODEL] * _sigmoid(z[:, D_MODEL:])


def _conv_front(x2d, g, w_bf, b):
    n = x2d.shape[0]
    tm = _pick(n, (512, 256, 128, 64, 32, 16, 8))
    return pl.pallas_call(
        _conv_front_body,
        grid=(n // tm,),
        in_specs=[pl.BlockSpec((tm, D_MODEL), lambda i: (i, 0)),
                  _const_spec((1, D_MODEL)),
                  _const_spec((D_MODEL, 2 * D_MODEL)),
                  _const_spec((1, 2 * D_MODEL))],
        out_specs=pl.BlockSpec((tm, D_MODEL), lambda i: (i, 0)),
        out_shape=jax.ShapeDtypeStruct((n, D_MODEL), F32),
        compiler_params=_params(("parallel",)),
        name="conv_front",
    )(x2d, g, w_bf, b)


_CONV_ROWS = 16


def _conv_back_body(main_ref, tail_ref, wdw_ref, bdw_ref, lng_ref, lnb_ref, w2_ref, b2_ref, x_ref, g1_ref,
                    o_ref, win_ref, y_ref):
    tt = main_ref.shape[1]
    win_ref[0:tt, :] = main_ref[0]
    win_ref[tt:tt + CONV_HALO, :] = tail_ref[0]
    first = CONV_HALO - (CONV_W - 1)

    for r0 in range(0, tt, _CONV_ROWS):
        acc = jnp.zeros((_CONV_ROWS, D_MODEL), F32)
        for k in range(CONV_W):
            acc = acc + win_ref[r0 + first + k:r0 + first + k + _CONV_ROWS, :] * wdw_ref[k:k + 1, :]
        y_ref[r0:r0 + _CONV_ROWS, :] = acc + bdw_ref[...]
    y = y_ref[...]
    yc = y - jnp.mean(y, axis=-1, keepdims=True)
    var = jnp.mean(yc * yc, axis=-1, keepdims=True)
    ln = yc * lax.rsqrt(var + EPS) * lng_ref[...] + lnb_ref[...]
    act = (ln * _sigmoid(ln)).astype(BF16)
    m = _dot(act, w2_ref[...]) + b2_ref[...]
    o_ref[0] = x_ref[0] + _rms(m, g1_ref[...])


def _conv_back(ext, x3d, wdw, bdw, lng, lnb, w2_bf, b2, g1):
    b, t, _ = x3d.shape
    tt = _pick(t, (256, 128, 64, 32))
    halo_blocks = tt // CONV_HALO
    return pl.pallas_call(
        _conv_back_body,
        grid=(b, t // tt),
        in_specs=[pl.BlockSpec((1, tt, D_MODEL), lambda bi, i: (bi, i, 0)),
                  pl.BlockSpec((1, CONV_HALO, D_MODEL), lambda bi, i: (bi, (i + 1) * halo_blocks, 0)),
                  _const_spec((CONV_HALO, D_MODEL)),
                  _const_spec((1, D_MODEL)), _const_spec((1, D_MODEL)), _const_spec((1, D_MODEL)),
                  _const_spec((D_MODEL, D_MODEL)), _const_spec((1, D_MODEL)),
                  pl.BlockSpec((1, tt, D_MODEL), lambda bi, i: (bi, i, 0)),
                  _const_spec((1, D_MODEL))],
        out_specs=pl.BlockSpec((1, tt, D_MODEL), lambda bi, i: (bi, i, 0)),
        out_shape=jax.ShapeDtypeStruct((b, t, D_MODEL), F32),
        scratch_shapes=[pltpu.VMEM((tt + CONV_HALO, D_MODEL), F32), pltpu.VMEM((tt, D_MODEL), F32)],
        compiler_params=_params(("parallel", "parallel")),
        name="conv_back",
    )(ext, ext, wdw, bdw, lng, lnb, w2_bf, b2, x3d, g1)


_FF_CHUNK = 1024


def _ffn_body(x_ref, g2_ref, w1_ref, w2_ref, g3_ref, wg_ref, bg_ref, p_ref, wp_ref, o_ref):
    x = x_ref[...]
    h = _rms(x, g2_ref[...]).astype(BF16)
    f = jnp.zeros(x.shape, F32)
    for c in range(D_FF // _FF_CHUNK):
        a = jnp.maximum(_dot(h, w1_ref[:, c * _FF_CHUNK:(c + 1) * _FF_CHUNK]), 0.0)
        f = f + _dot((a * a).astype(BF16), w2_ref[c * _FF_CHUNK:(c + 1) * _FF_CHUNK, :])
    x2 = x + _rms(f, g3_ref[...])
    gate = _sigmoid(_dot(x2.astype(BF16), wg_ref[...]) + bg_ref[...])
    o_ref[...] = x2 + gate * _dot(p_ref[...].astype(BF16), wp_ref[...])


def _ffn(x2d, g2, w1_bf, w2_bf, g3, wg_bf, bg, p2d, wp_bf):
    n = x2d.shape[0]
    d_ple = p2d.shape[1]
    tm = _pick(n, (256, 128, 64, 32, 16, 8))
    return pl.pallas_call(
        _ffn_body,
        grid=(n // tm,),
        in_specs=[pl.BlockSpec((tm, D_MODEL), lambda i: (i, 0)),
                  _const_spec((1, D_MODEL)),
                  _const_spec((D_MODEL, D_FF)), _const_spec((D_FF, D_MODEL)),
                  _const_spec((1, D_MODEL)),
                  _const_spec((D_MODEL, D_MODEL)), _const_spec((1, D_MODEL)),
                  pl.BlockSpec((tm, d_ple), lambda i: (i, 0)),
                  _const_spec((d_ple, D_MODEL))],
        out_specs=pl.BlockSpec((tm, D_MODEL), lambda i: (i, 0)),
        out_shape=jax.ShapeDtypeStruct((n, D_MODEL), F32),
        compiler_params=_params(("parallel",)),
        name="ffn_ple",
    )(x2d, g2, w1_bf, w2_bf, g3, wg_bf, bg, p2d, wp_bf)


_PROJ_W = Q_DIM + 6 * KV_DIM + LANES


def _rope_block(blk, c, su, sd):
    return blk * c + pltpu.roll(blk, ROT_DIM // 2, 1) * su + pltpu.roll(blk, LANES - ROT_DIM // 2, 1) * sd


def _nsa_proj_body(x_ref, g_ref, w_ref, c_ref, su_ref, sd_ref,
                   q_ref, qr_ref, cmp_ref, sel_ref, win_ref, gate_ref, qt_ref, qrt_ref, vt_ref, gatet_ref):
    h = _rms(x_ref[...], g_ref[...]).astype(BF16)
    z = _dot(h, w_ref[...])
    c, su, sd = c_ref[...], su_ref[...], sd_ref[...]
    for i in range(Q_DIM // LANES):
        cols = slice(i * LANES, (i + 1) * LANES)
        qs = z[:, cols] * SCALE
        qrs = _rope_block(z[:, cols], c, su, sd) * SCALE
        q_ref[:, cols] = qs.astype(BF16)
        qr_ref[:, cols] = qrs.astype(BF16)
        qt_ref[0, cols, :] = qs.T.astype(BF16)
        qrt_ref[0, cols, :] = qrs.T.astype(BF16)
    o = Q_DIM
    cmp_ref[...] = z[:, o:o + 2 * KV_DIM]
    o += 2 * KV_DIM
    for n, dst in enumerate((sel_ref, win_ref)):
        for i in range(KV_DIM // LANES):
            dst[:, i * LANES:(i + 1) * LANES] = _rope_block(z[:, o + i * LANES:o + (i + 1) * LANES], c, su, sd)
            v = z[:, o + KV_DIM + i * LANES:o + KV_DIM + (i + 1) * LANES]
            vt_ref[0, n * KV_DIM + i * LANES:n * KV_DIM + (i + 1) * LANES, :] = v.T.astype(BF16)
        dst[:, KV_DIM:] = z[:, o + KV_DIM:o + 2 * KV_DIM]
        o += 2 * KV_DIM
    gates = _sigmoid(z[:, o:o + LANES])
    gate_ref[...] = gates
    gatet_ref[0] = gates.T


def _rope_tables(pos, rows):
    half = ROT_DIM // 2
    inv = jnp.float32(ROPE_THETA) ** (-jnp.arange(half, dtype=F32) * (2.0 / ROT_DIM))
    ang = pos.astype(F32)[:, None] * inv[None, :]
    lane = np.arange(LANES)
    within = lane % HEAD_DIM
    cos = jnp.cos(ang)[:, lane % half]
    sin = jnp.sin(ang)[:, lane % half]
    c = jnp.where(within[None, :] < ROT_DIM, cos, 1.0)
    su = jnp.where((within[None, :] >= half) & (within[None, :] < ROT_DIM), sin, 0.0)
    sd = jnp.where(within[None, :] < half, -sin, 0.0)
    reps = rows // pos.shape[0]
    return tuple(jnp.tile(a, (reps, 1)) for a in (c, su, sd))


def _nsa_proj(x2d, g, w_bf, pos):
    n = x2d.shape[0]
    t = pos.shape[0]
    tm = _pick(n, (256, 128, 64, 32, 16, 8))
    if t >= tm:
        assert t % tm == 0
        tabs = _rope_tables(pos, t)
        nt = t // tm
        tab_spec = pl.BlockSpec((tm, LANES), lambda i: (i % nt, 0))
    else:
        assert tm % t == 0
        tabs = _rope_tables(pos, tm)
        tab_spec = _const_spec((tm, LANES))
    row = lambda w: pl.BlockSpec((tm, w), lambda i: (i, 0))
    tile = lambda rows: pl.BlockSpec((1, rows, tm), lambda i: (i, 0, 0))
    tiles = lambda rows, dt: jax.ShapeDtypeStruct((n // tm, rows, tm), dt)
    return pl.pallas_call(
        _nsa_proj_body,
        grid=(n // tm,),
        in_specs=[row(D_MODEL), _const_spec((1, D_MODEL)), _const_spec((D_MODEL, _PROJ_W)),
                  tab_spec, tab_spec, tab_spec],
        out_specs=[row(Q_DIM), row(Q_DIM), row(2 * KV_DIM), row(2 * KV_DIM), row(2 * KV_DIM), row(LANES),
                   tile(Q_DIM), tile(Q_DIM), tile(2 * KV_DIM), tile(LANES)],
        out_shape=[jax.ShapeDtypeStruct((n, Q_DIM), BF16), jax.ShapeDtypeStruct((n, Q_DIM), BF16),
                   jax.ShapeDtypeStruct((n, 2 * KV_DIM), F32), jax.ShapeDtypeStruct((n, 2 * KV_DIM), F32),
                   jax.ShapeDtypeStruct((n, 2 * KV_DIM), F32), jax.ShapeDtypeStruct((n, LANES), F32),
                   tiles(Q_DIM, BF16), tiles(Q_DIM, BF16), tiles(2 * KV_DIM, BF16), tiles(LANES, F32)],
        compiler_params=_params(("parallel",)),
        name="nsa_proj",
    )(x2d, g, w_bf, *tabs)


_CHUNK_W = STRIDE * 2 * KV_DIM
_AB_W = 2 * N_KV * CMP_HID


def _chunk_rows_first_layer(src_ref, kv, w_ref):
    acc = jnp.zeros((src_ref.shape[1], _AB_W), F32)
    for s in range(STRIDE):
        lo = s * 2 * KV_DIM + kv * KV_DIM
        acc = acc + _dot(src_ref[0, :, lo:lo + KV_DIM].astype(BF16), w_ref[s])
    return acc


def _compress_finish(xl_ref, per_kv, n_rows):
    half = _AB_W // 2
    for kv, (ab_ref, w_ref, pe_ref, w1_ref, w2_ref, out_ref) in enumerate(per_kv):
        ab_ref[n_rows:n_rows + 8, :] = _chunk_rows_first_layer(xl_ref, kv, w_ref)
        c = jnp.sum(pe_ref[...] * w1_ref[...], axis=0, keepdims=True)
        c = jnp.concatenate([c] * N_KV, axis=1)
        pre = ab_ref[0:n_rows, 0:half] + ab_ref[1:n_rows + 1, half:] + c
        hid = (pre * _sigmoid(pre)).astype(BF16)
        out_ref[0] = _dot(hid, w2_ref[...])


def _compress_body(x_ref, xl_ref, wk_ref, wv_ref, pek_ref, pev_ref, w1k_ref, w1v_ref, w2k_ref, w2v_ref,
                   kcc_ref, vcc_ref, abk_ref, abv_ref):
    rt = x_ref.shape[1]
    i = pl.program_id(1)
    n_rt = pl.num_programs(1)
    r0 = pl.multiple_of(i * rt, rt)
    abk_ref[pl.ds(r0, rt), :] = _chunk_rows_first_layer(x_ref, 0, wk_ref)
    abv_ref[pl.ds(r0, rt), :] = _chunk_rows_first_layer(x_ref, 1, wv_ref)

    @pl.when(i == n_rt - 1)
    def _():
        _compress_finish(xl_ref, ((abk_ref, wk_ref, pek_ref, w1k_ref, w2k_ref, kcc_ref),
                                  (abv_ref, wv_ref, pev_ref, w1v_ref, w2v_ref, vcc_ref)), kcc_ref.shape[1])


def _compress(x_chunks, x_last, wk, wv, pek, pev, w1k, w1v, w2k, w2v):
    b, n, _ = x_chunks.shape
    rt = _pick(n, (128, 64, 32, 16, 8)) if n % 8 == 0 else n
    flat = L_CMP * HEAD_DIM
    out = jax.ShapeDtypeStruct((b, n, KV_DIM), F32)
    return pl.pallas_call(
        _compress_body,
        grid=(b, n // rt),
        in_specs=[pl.BlockSpec((1, rt, _CHUNK_W), lambda bi, i: (bi, i, 0)),
                  pl.BlockSpec((1, 8, _CHUNK_W), lambda bi, i: (bi, 0, 0)),
                  _const_spec((STRIDE, KV_DIM, _AB_W)), _const_spec((STRIDE, KV_DIM, _AB_W)),
                  _const_spec((flat, 1)), _const_spec((flat, 1)),
                  _const_spec((flat, CMP_HID)), _const_spec((flat, CMP_HID)),
                  _const_spec((N_KV * CMP_HID, KV_DIM)), _const_spec((N_KV * CMP_HID, KV_DIM))],
        out_specs=[pl.BlockSpec((1, n, KV_DIM), lambda bi, i: (bi, 0, 0))] * 2,
        out_shape=[out, out],
        scratch_shapes=[pltpu.VMEM((n + 8, _AB_W), F32), pltpu.VMEM((n + 8, _AB_W), F32)],
        compiler_params=_params(("parallel", "arbitrary")),
        name="compress",
    )(x_chunks, x_last, wk, wv, pek, pev, w1k, w1v, w2k, w2v)


def _compress_weights(cmp_pe, cmp_w1, cmp_w2):
    eye = jnp.eye(N_KV, dtype=F32)
    outs = []
    for kv in range(2):
        w1 = cmp_w1[kv].reshape(2, STRIDE, HEAD_DIM, CMP_HID)
        big = jnp.einsum('gk,asdh->sgdakh', eye, w1).reshape(STRIDE, KV_DIM, _AB_W)
        w2 = jnp.einsum('gk,hd->ghkd', eye, cmp_w2[kv]).reshape(N_KV * CMP_HID, KV_DIM)
        outs.append((big.astype(BF16), cmp_pe[kv].reshape(L_CMP * HEAD_DIM, 1),
                     cmp_w1[kv].reshape(L_CMP * HEAD_DIM, CMP_HID), w2.astype(BF16)))
    return outs


def _select_blocks(imp_t, blk, qpos, n_real, k):
    cur = jnp.right_shift(qpos, int(math.log2(L_SEL)))
    forced = (blk == 0) | (blk == cur) | (blk == cur - 1)
    future = blk * L_SEL > qpos
    score = jnp.where(future, -jnp.inf, jnp.where(forced, FORCE, imp_t))
    score = jnp.where(blk < n_real, score, -jnp.inf)
    sub = 8
    rows = [score[i:i + sub] for i in range(0, score.shape[0], sub)]
    local = lax.broadcasted_iota(jnp.int32, rows[0].shape, 0)
    rank = [jnp.zeros(r.shape, jnp.int32) for r in rows]
    for j in range(n_real):
        sj = score[j:j + 1, :]
        for n, r in enumerate(rows):
            if n * sub > j:
                beats = sj >= r
            elif n * sub + sub - 1 <= j:
                beats = sj > r
            else:
                beats = jnp.where(local > j - n * sub, jnp.where(sj >= r, 1, 0), jnp.where(sj > r, 1, 0)) > 0
            rank[n] = rank[n] + jnp.where(beats, 1, 0)
    return (jnp.concatenate(rank, axis=0) < k) & (blk < n_real)


def _overlap_t(n_sel_pad, n_cmp_pad, n_sel, n_cmp):
    ci = np.arange(n_cmp_pad)[None, :] * STRIDE
    sj = np.arange(n_sel_pad)[:, None] * L_SEL
    ov = (ci < sj + L_SEL) & (ci + L_CMP > sj)
    ov &= (np.arange(n_cmp_pad)[None, :] < n_cmp) & (np.arange(n_sel_pad)[:, None] < n_sel)
    return jnp.asarray(ov, dtype=BF16)


def _head_rows(h):
    return slice(h * HEAD_DIM, (h + 1) * HEAD_DIM)


def _cmp_topk_body(n_sel, qt_ref, kcc_ref, vcct_ref, gt_ref, ovt_ref, ocg_ref, bias_ref):
    tq = qt_ref.shape[2]
    nc = kcc_ref.shape[2]
    nsp = ovt_ref.shape[0]
    t0 = pl.program_id(1) * tq
    wide = N_REP * tq
    qpos = t0 + (lax.broadcasted_iota(jnp.int32, (1, wide), 1) & (tq - 1))
    cmp_end = lax.broadcasted_iota(jnp.int32, (nc, 1), 0) * STRIDE + (L_CMP - 1)
    cmask = cmp_end <= qpos
    blk = lax.broadcasted_iota(jnp.int32, (nsp, tq), 0)
    qpos_t = t0 + lax.broadcasted_iota(jnp.int32, (nsp, tq), 1)
    ovt = ovt_ref[...]
    for g in range(N_KV):
        heads = range(g * N_REP, (g + 1) * N_REP)
        q = jnp.concatenate([qt_ref[0, _head_rows(h), :] for h in heads], axis=1)
        lm = jnp.where(cmask, _dot(kcc_ref[0, g], q), NEG)
        e = jnp.exp(lm - jnp.max(lm, axis=0, keepdims=True))
        pc = jnp.where(cmask, e * (1.0 / jnp.sum(e, axis=0, keepdims=True)), 0.0)
        o = _dot(vcct_ref[0, _head_rows(g), :], pc.astype(BF16))
        pg = jnp.zeros((nc, tq), F32)
        for r, h in enumerate(heads):
            ocg_ref[0, _head_rows(h), :] = o[:, r * tq:(r + 1) * tq] * gt_ref[0, 3 * h:3 * h + 1, :]
            pg = pg + pc[:, r * tq:(r + 1) * tq]
        hi, lo = _split_bf16(pg)
        imp_t = _dot(ovt, hi) + _dot(ovt, lo)
        sel = _select_blocks(imp_t, blk, qpos_t, n_sel, min(N_TOP, n_sel))
        bias_ref[0, g] = jnp.where(sel, 0.0, NEG).astype(BF16)


def _cmp_topk(q_t, kcc_gm, vcc_t, gates_t, b, n_cmp, n_sel, nsp):
    tq = q_t.shape[2]
    n_t = q_t.shape[0] // b
    nc = kcc_gm.shape[2]
    ovt = _overlap_t(nsp, nc, n_sel, n_cmp)
    tile = lambda rows: pl.BlockSpec((1, rows, tq), lambda bi, i: (bi * n_t + i, 0, 0))
    return pl.pallas_call(
        functools.partial(_cmp_topk_body, n_sel),
        grid=(b, n_t),
        in_specs=[tile(Q_DIM),
                  pl.BlockSpec((1, N_KV, nc, HEAD_DIM), lambda bi, i: (bi, 0, 0, 0)),
                  pl.BlockSpec((1, KV_DIM, nc), lambda bi, i: (bi, 0, 0)),
                  tile(LANES), _const_spec((nsp, nc))],
        out_specs=[tile(Q_DIM), pl.BlockSpec((1, N_KV, nsp, tq), lambda bi, i: (bi, 0, 0, i))],
        out_shape=[jax.ShapeDtypeStruct((b * n_t, Q_DIM, tq), F32),
                   jax.ShapeDtypeStruct((b, N_KV, nsp, n_t * tq), BF16)],
        compiler_params=_params(("parallel", "parallel")),
        name="cmp_topk",
    )(q_t, kcc_gm, vcc_t, gates_t, ovt)


def _flash_init(m_ref, l_ref, acc_ref):
    m_ref[...] = jnp.full(m_ref.shape, NEG, F32)
    l_ref[...] = jnp.zeros(l_ref.shape, F32)
    acc_ref[...] = jnp.zeros(acc_ref.shape, F32)


def _flash_t(state, s, vt):
    m, l, acc = state
    m_new = jnp.maximum(m, jnp.max(s, axis=0, keepdims=True))
    alpha = jnp.exp(m - m_new)
    p = jnp.exp(s - m_new)
    return (m_new, alpha * l + jnp.sum(p, axis=0, keepdims=True), alpha * acc + _dot(vt, p.astype(BF16)))


def _sel_win_body(qrt_ref, bias_ref, kp_ref, kw_ref, vt_ref, ocg_ref, gt_ref, w_ref, x_ref, g1_ref,
                  o_ref, ot_ref):
    tq = qrt_ref.shape[2]
    kc = tq
    qt = pl.program_id(1)
    wide = N_REP * tq
    krow = lax.broadcasted_iota(jnp.int32, (kc, wide), 0)
    qcol = lax.broadcasted_iota(jnp.int32, (kc, wide), 1) & (tq - 1)
    n_back = WINDOW // kc
    fresh = (jnp.full((1, wide), NEG, F32), jnp.zeros((1, wide), F32), jnp.zeros((HEAD_DIM, wide), F32))

    for g in range(N_KV):
        heads = range(g * N_REP, (g + 1) * N_REP)
        qw = jnp.concatenate([qrt_ref[0, _head_rows(h), :] for h in heads], axis=1)
        qs = jnp.concatenate([qw, jnp.concatenate([bias_ref[0, g]] * N_REP, axis=1)], axis=0)

        def sel_chunk(j, state, diagonal):
            s = _dot(kp_ref[0, g, pl.ds(pl.multiple_of(j * kc, kc), kc), :], qs)
            return _flash_t(state, jnp.where(krow <= qcol, s, NEG) if diagonal else s,
                            vt_ref[j, _head_rows(g), :])

        state = lax.fori_loop(0, qt // 2, lambda i, st: sel_chunk(2 * i + 1, sel_chunk(2 * i, st, False), False),
                              fresh)
        state = lax.cond(qt % 2 == 1, lambda st: sel_chunk(qt - 1, st, False), lambda st: st, state)
        _, l_sel, acc_sel = sel_chunk(qt, state, True)

        def win_chunk(back, state):
            j = qt - back
            s = _dot(kw_ref[0, g, pl.ds(pl.multiple_of(j * kc, kc), kc), :], qw)
            if back == 0:
                s = jnp.where(krow <= qcol, s, NEG)
            elif back == n_back:
                s = jnp.where(krow > qcol, s, NEG)
            return _flash_t(state, s, vt_ref[j, KV_DIM + g * HEAD_DIM:KV_DIM + (g + 1) * HEAD_DIM, :])

        state = win_chunk(0, fresh)
        for back in range(1, n_back + 1):
            state = lax.cond(qt >= back, functools.partial(win_chunk, back), lambda st: st, state)
        _, l_win, acc_win = state

        for r, h in enumerate(heads):
            cols = slice(r * tq, (r + 1) * tq)
            ot_ref[_head_rows(h), :] = (
                ocg_ref[0, _head_rows(h), :]
                + acc_sel[:, cols] * (gt_ref[0, 3 * h + 1:3 * h + 2, :] * (1.0 / l_sel[:, cols]))
                + acc_win[:, cols] * (gt_ref[0, 3 * h + 2:3 * h + 3, :] * (1.0 / l_win[:, cols])))

    m = _dot(ot_ref[...].T.astype(BF16), w_ref[...])
    o_ref[...] = x_ref[...] + _rms(m, g1_ref[...])


def _sel_win_out(qr_t, bias_t, kp_gm, kw_gm, v_t, ocg_t, gates_t, w_out_bf, x2d, g1, b):
    tq = qr_t.shape[2]
    n_t = qr_t.shape[0] // b
    t = n_t * tq
    cw = kp_gm.shape[3]
    nsp = cw - HEAD_DIM
    assert WINDOW % tq == 0
    tile = lambda rows: pl.BlockSpec((1, rows, tq), lambda bi, i: (bi * n_t + i, 0, 0))
    keys = lambda w: pl.BlockSpec((1, N_KV, t, w), lambda bi, i: (bi, 0, 0, 0))
    xrow = pl.BlockSpec((tq, D_MODEL), lambda bi, i: (bi * n_t + i, 0))
    return pl.pallas_call(
        _sel_win_body,
        grid=(b, n_t),
        in_specs=[tile(Q_DIM),
                  pl.BlockSpec((1, N_KV, nsp, tq), lambda bi, i: (bi, 0, 0, i)),
                  keys(cw), keys(HEAD_DIM),
                  pl.BlockSpec((n_t, 2 * KV_DIM, tq), lambda bi, i: (bi, 0, 0)),
                  tile(Q_DIM), tile(LANES),
                  _const_spec((Q_DIM, D_MODEL)), xrow, _const_spec((1, D_MODEL))],
        out_specs=xrow,
        out_shape=jax.ShapeDtypeStruct((b * t, D_MODEL), F32),
        scratch_shapes=[pltpu.VMEM((Q_DIM, tq), F32)],
        compiler_params=_params(("parallel", "parallel")),
        name="sel_win_out",
    )(qr_t, bias_t, kp_gm, kw_gm, v_t, ocg_t, gates_t, w_out_bf, x2d, g1)


def _nsa_out_body(oc_ref, os_ref, ow_ref, gate_ref, e_ref, w_ref, x_ref, g1_ref, o_ref):
    hi, lo = _split_bf16(gate_ref[...])
    o = jnp.zeros(oc_ref.shape, F32)
    for c, src in enumerate((oc_ref, os_ref, ow_ref)):
        o = o + (_dot(hi, e_ref[c]) + _dot(lo, e_ref[c])) * src[...]
    m = _dot(o.astype(BF16), w_ref[...])
    o_ref[...] = x_ref[...] + _rms(m, g1_ref[...])


def _gate_expand():
    e = np.zeros((3, LANES, Q_DIM), np.float32)
    for h in range(N_HEADS):
        for c in range(3):
            e[c, h * 3 + c, h * HEAD_DIM:(h + 1) * HEAD_DIM] = 1.0
    return jnp.asarray(e, dtype=BF16)


def _nsa_out(o_cmp, o_sel, o_win, gates, w_bf, x2d, g1):
    n = x2d.shape[0]
    tm = _pick(n, (256, 128, 64, 32, 16, 8))
    row = lambda w: pl.BlockSpec((tm, w), lambda i: (i, 0))
    return pl.pallas_call(
        _nsa_out_body,
        grid=(n // tm,),
        in_specs=[row(Q_DIM), row(Q_DIM), row(Q_DIM), row(LANES), _const_spec((3, LANES, Q_DIM)),
                  _const_spec((Q_DIM, D_MODEL)), row(D_MODEL), _const_spec((1, D_MODEL))],
        out_specs=row(D_MODEL),
        out_shape=jax.ShapeDtypeStruct((n, D_MODEL), F32),
        compiler_params=_params(("parallel",)),
        name="nsa_out",
    )(o_cmp, o_sel, o_win, gates, _gate_expand(), w_bf, x2d, g1)


class _PageRing:
    def __init__(self, pt_ref, cache_hbm, buf_ref, sem_ref, n_b, n_c):
        self.pt, self.cache, self.buf, self.sem = pt_ref, cache_hbm, buf_ref, sem_ref
        self.n_b, self.n_c, self.pg = n_b, n_c, buf_ref.shape[1]

    def _copies(self, step):
        b, c, slot = step // self.n_c, step % self.n_c, step % 2
        return [pltpu.make_async_copy(self.cache.at[self.pt[b, c * self.pg + p]], self.buf.at[slot, p],
                                      self.sem.at[slot]) for p in range(self.pg)]

    def acquire(self, b, c):
        step = b * self.n_c + c

        @pl.when(step == 0)
        def _():
            for cp in self._copies(step):
                cp.start()

        @pl.when(step + 1 < self.n_b * self.n_c)
        def _():
            for cp in self._copies(step + 1):
                cp.start()

        for cp in self._copies(step):
            cp.wait()
        return step % 2


def _compress_paged_body(n_b, n_c, pt_ref, cache_hbm, xl_ref, wk_ref, wv_ref, pek_ref, pev_ref, w1k_ref, w1v_ref,
                         w2k_ref, w2v_ref, kcc_ref, vcc_ref, buf_ref, sem_ref, xt_ref, abk_ref, abv_ref):
    pg, page = buf_ref.shape[1], buf_ref.shape[4]
    rows = pg * page // STRIDE
    b, c = pl.program_id(0), pl.program_id(1)
    slot = _PageRing(pt_ref, cache_hbm, buf_ref, sem_ref, n_b, n_c).acquire(b, c)
    r0 = pl.multiple_of(c * rows, rows)
    for kv, (ab_ref, w_ref) in enumerate(((abk_ref, wk_ref), (abv_ref, wv_ref))):
        for p in range(pg):
            for j in range(KV_DIM // LANES):
                xt_ref[j, p * page:(p + 1) * page, :] = buf_ref[slot, p, kv, j * LANES:(j + 1) * LANES, :].T
        acc = jnp.zeros((rows, _AB_W), F32)
        for s in range(STRIDE):
            xs = jnp.concatenate([xt_ref[j, pl.ds(s, rows, stride=STRIDE), :] for j in range(KV_DIM // LANES)],
                                 axis=1)
            acc = acc + _dot(xs.astype(BF16), w_ref[s])
        ab_ref[pl.ds(r0, rows), :] = acc

    @pl.when(c == n_c - 1)
    def _():
        _compress_finish(xl_ref, ((abk_ref, wk_ref, pek_ref, w1k_ref, w2k_ref, kcc_ref),
                                  (abv_ref, wv_ref, pev_ref, w1v_ref, w2v_ref, vcc_ref)), kcc_ref.shape[1])


def _compress_paged(page_table, cache_fm, x_last, wk, wv, pek, pev, w1k, w1v, w2k, w2v):
    b, n_pages = page_table.shape
    page = cache_fm.shape[3]
    pg = _pick(n_pages, (16, 8, 4, 2, 1))
    n_c = n_pages // pg
    n = n_pages * page // STRIDE
    flat = L_CMP * HEAD_DIM
    out = jax.ShapeDtypeStruct((b, n, KV_DIM), F32)
    const = _const_spec
    return pl.pallas_call(
        functools.partial(_compress_paged_body, b, n_c),
        grid_spec=pltpu.PrefetchScalarGridSpec(
            num_scalar_prefetch=1,
            grid=(b, n_c),
            in_specs=[pl.BlockSpec(memory_space=pl.ANY),
                      pl.BlockSpec((1, 8, _CHUNK_W), lambda bi, c, pt: (bi, 0, 0)),
                      const((STRIDE, KV_DIM, _AB_W)), const((STRIDE, KV_DIM, _AB_W)),
                      const((flat, 1)), const((flat, 1)), const((flat, CMP_HID)), const((flat, CMP_HID)),
                      const((N_KV * CMP_HID, KV_DIM)), const((N_KV * CMP_HID, KV_DIM))],
            out_specs=[pl.BlockSpec((1, n, KV_DIM), lambda bi, c, pt: (bi, 0, 0))] * 2,
            scratch_shapes=[pltpu.VMEM((2, pg, 2, KV_DIM, page), F32), pltpu.SemaphoreType.DMA((2,)),
                            pltpu.VMEM((KV_DIM // LANES, pg * page, LANES), F32),
                            pltpu.VMEM((n + 8, _AB_W), F32), pltpu.VMEM((n + 8, _AB_W), F32)]),
        out_shape=[out, out],
        compiler_params=_params(("arbitrary", "arbitrary")),
        name="compress_paged",
    )(page_table, cache_fm, x_last, wk, wv, pek, pev, w1k, w1v, w2k, w2v)


def _softmax_rows(s):
    e = jnp.exp(s - jnp.max(s, axis=-1, keepdims=True))
    return e / jnp.sum(e, axis=-1, keepdims=True)


def _sample_cmp_body(n_sel, past_len, t_new, q_ref, kcc_ref, vcc_ref, ovt_ref, rep_ref, o_ref, sel_ref):
    rows = q_ref.shape[1]
    nc = kcc_ref.shape[1]
    nsp = ovt_ref.shape[0]
    gq = N_KV * t_new
    tq_col = lax.broadcasted_iota(jnp.int32, (rows, 1), 0) & (t_new - 1)
    cmp_end = lax.broadcasted_iota(jnp.int32, (1, nc), 1) * STRIDE + (L_CMP - 1)
    cmask = cmp_end <= past_len + tq_col
    lm = jnp.where(cmask, _dot_nt(q_ref[0], kcc_ref[0].astype(BF16)), NEG)
    pc = jnp.where(cmask, _softmax_rows(lm), 0.0)
    o_ref[0] = _dot(pc.astype(BF16), vcc_ref[0].astype(BF16))
    pg = pc.reshape(N_KV, N_REP, t_new, nc).sum(axis=1).reshape(gq, nc)
    hi, lo = _split_bf16(pg)
    imp_t = _dot_nt(ovt_ref[...], hi) + _dot_nt(ovt_ref[...], lo)
    blk = lax.broadcasted_iota(jnp.int32, (nsp, gq), 0)
    qpos = past_len + (lax.broadcasted_iota(jnp.int32, (nsp, gq), 1) & (t_new - 1))
    sel = _select_blocks(imp_t, blk, qpos, n_sel, min(N_TOP, n_sel))
    sel_ref[0] = _dot_nt(rep_ref[...], jnp.where(sel, 1.0, 0.0).astype(BF16)).astype(BF16)


def _sample_cmp(q_bd, kcc, vcc, n_cmp, n_sel, nsp, past_len, t_new):
    b, rows, _ = q_bd.shape
    nc = kcc.shape[1]
    gq = N_KV * t_new
    ovt = _overlap_t(nsp, nc, n_sel, n_cmp)
    rep = np.zeros((rows, gq), np.float32)
    for h in range(N_HEADS):
        for t in range(t_new):
            rep[h * t_new + t, (h // N_REP) * t_new + t] = 1.0
    return pl.pallas_call(
        functools.partial(_sample_cmp_body, n_sel, past_len, t_new),
        grid=(b,),
        in_specs=[pl.BlockSpec((1, rows, KV_DIM), lambda bi: (bi, 0, 0)),
                  pl.BlockSpec((1, nc, KV_DIM), lambda bi: (bi, 0, 0)),
                  pl.BlockSpec((1, nc, KV_DIM), lambda bi: (bi, 0, 0)),
                  _const_spec((nsp, nc)), _const_spec((rows, gq))],
        out_specs=[pl.BlockSpec((1, rows, KV_DIM), lambda bi: (bi, 0, 0)),
                   pl.BlockSpec((1, rows, nsp), lambda bi: (bi, 0, 0))],
        out_shape=[jax.ShapeDtypeStruct((b, rows, KV_DIM), F32), jax.ShapeDtypeStruct((b, rows, nsp), BF16)],
        compiler_params=_params(("parallel",)),
        name="sample_cmp",
    )(q_bd, kcc, vcc, ovt, jnp.asarray(rep, dtype=BF16))


def _flash_update(s, pv_fn, m_ref, l_ref, acc_ref):
    m_prev = m_ref[...]
    m_new = jnp.maximum(m_prev, jnp.max(s, axis=-1, keepdims=True))
    alpha = jnp.exp(m_prev - m_new)
    p = jnp.exp(s - m_new)
    l_ref[...] = alpha * l_ref[...] + jnp.sum(p, axis=-1, keepdims=True)
    acc_ref[...] = alpha * acc_ref[...] + pv_fn(p.astype(BF16))
    m_ref[...] = m_new


def _sample_sel_body(t_new, n_b, n_c, pt_ref, cache_hbm, q_ref, sel_ref, e_ref, et_ref, tail_ref, o_ref,
                     buf_ref, sem_ref, m_ref, l_ref, acc_ref):
    rows = q_ref.shape[1]
    pg, page = buf_ref.shape[1], buf_ref.shape[4]
    b, c = pl.program_id(0), pl.program_id(1)
    q = q_ref[0]

    @pl.when(c == 0)
    def _():
        _flash_init(m_ref, l_ref, acc_ref)

    @pl.when(c < n_c)
    def _():
        slot = _PageRing(pt_ref, cache_hbm, buf_ref, sem_ref, n_b, n_c).acquire(b, c)
        s = jnp.concatenate([_dot(q, buf_ref[slot, p, 0].astype(BF16)) for p in range(pg)], axis=1)
        s = jnp.where(_dot(sel_ref[0], e_ref[...]) > 0.5, s, NEG)

        def pv(p_bf):
            return sum(_dot_nt(p_bf[:, p * page:(p + 1) * page], buf_ref[slot, p, 1].astype(BF16))
                       for p in range(pg))

        _flash_update(s, pv, m_ref, l_ref, acc_ref)

    @pl.when(c == n_c)
    def _():
        kv = tail_ref[0]
        nk = kv.shape[0]
        s = _dot_nt(q, kv[:, :KV_DIM].astype(BF16))
        tq = lax.broadcasted_iota(jnp.int32, (rows, 1), 0) & (t_new - 1)
        ok = (_dot(sel_ref[0], et_ref[...]) > 0.5) & (lax.broadcasted_iota(jnp.int32, (1, nk), 1) <= tq)
        _flash_update(jnp.where(ok, s, NEG), lambda p_bf: _dot(p_bf, kv[:, KV_DIM:].astype(BF16)),
                      m_ref, l_ref, acc_ref)
        o_ref[0] = acc_ref[...] / l_ref[...]


def _sample_sel(page_table, cache_fm, q_bd, sel01, tail_sel, n_sel, t_new):
    b, rows, _ = q_bd.shape
    nsp = sel01.shape[2]
    n_pages = page_table.shape[1]
    page = cache_fm.shape[3]
    pg = _pick(n_pages, (8, 4, 2, 1))
    n_c = n_pages // pg
    kc = pg * page
    nk = tail_sel.shape[1]
    key_blk = np.arange(n_pages * page) // L_SEL
    e = (np.arange(nsp)[:, None] == key_blk[None, :]).astype(np.float32)
    et = np.zeros((nsp, nk), np.float32)
    et[n_sel - 1, :] = 1.0
    return pl.pallas_call(
        functools.partial(_sample_sel_body, t_new, b, n_c),
        grid_spec=pltpu.PrefetchScalarGridSpec(
            num_scalar_prefetch=1,
            grid=(b, n_c + 1),
            in_specs=[pl.BlockSpec(memory_space=pl.ANY),
                      pl.BlockSpec((1, rows, KV_DIM), lambda bi, c, pt: (bi, 0, 0)),
                      pl.BlockSpec((1, rows, nsp), lambda bi, c, pt: (bi, 0, 0)),
                      pl.BlockSpec((nsp, kc), lambda bi, c, pt: (0, jnp.minimum(c, n_c - 1))),
                      _const_spec((nsp, nk)),
                      pl.BlockSpec((1, nk, 2 * KV_DIM), lambda bi, c, pt: (bi, 0, 0))],
            out_specs=pl.BlockSpec((1, rows, KV_DIM), lambda bi, c, pt: (bi, 0, 0)),
            scratch_shapes=[pltpu.VMEM((2, pg, 2, KV_DIM, page), F32), pltpu.SemaphoreType.DMA((2,)),
                            pltpu.VMEM((rows, 1), F32), pltpu.VMEM((rows, 1), F32),
                            pltpu.VMEM((rows, KV_DIM), F32)]),
        out_shape=jax.ShapeDtypeStruct((b, rows, KV_DIM), F32),
        compiler_params=_params(("arbitrary", "arbitrary")),
        name="sample_sel",
    )(page_table, cache_fm, q_bd, sel01, jnp.asarray(e, dtype=BF16), jnp.asarray(et, dtype=BF16), tail_sel)


def _sample_win_body(t_new, q_ref, kv_ref, tail_ref, o_ref):
    rows = q_ref.shape[1]
    wb = kv_ref.shape[3]
    nk = tail_ref.shape[1]
    q = q_ref[0]
    tq = lax.broadcasted_iota(jnp.int32, (rows, 1), 0) & (t_new - 1)
    tail = tail_ref[0]
    s_old = jnp.where(lax.broadcasted_iota(jnp.int32, (1, wb), 1) > tq - WINDOW + wb,
                      _dot(q, kv_ref[0, 0].astype(BF16)), NEG)
    s_new = jnp.where(lax.broadcasted_iota(jnp.int32, (1, nk), 1) <= tq,
                      _dot_nt(q, tail[:, :KV_DIM].astype(BF16)), NEG)
    m = jnp.maximum(jnp.max(s_old, axis=-1, keepdims=True), jnp.max(s_new, axis=-1, keepdims=True))
    p_old = jnp.exp(s_old - m)
    p_new = jnp.exp(s_new - m)
    den = jnp.sum(p_old, axis=-1, keepdims=True) + jnp.sum(p_new, axis=-1, keepdims=True)
    o = _dot_nt(p_old.astype(BF16), kv_ref[0, 1].astype(BF16)) + _dot(p_new.astype(BF16), tail[:, KV_DIM:].astype(BF16))
    o_ref[0] = o / den


def _sample_win(q_bd, win_fm, tail_win, t_new):
    b, rows, _ = q_bd.shape
    wb = win_fm.shape[3]
    nk = tail_win.shape[1]
    return pl.pallas_call(
        functools.partial(_sample_win_body, t_new),
        grid=(b,),
        in_specs=[pl.BlockSpec((1, rows, KV_DIM), lambda bi: (bi, 0, 0)),
                  pl.BlockSpec((1, 2, KV_DIM, wb), lambda bi: (bi, 0, 0, 0)),
                  pl.BlockSpec((1, nk, 2 * KV_DIM), lambda bi: (bi, 0, 0))],
        out_specs=pl.BlockSpec((1, rows, KV_DIM), lambda bi: (bi, 0, 0)),
        out_shape=jax.ShapeDtypeStruct((b, rows, KV_DIM), F32),
        compiler_params=_params(("parallel",)),
        name="sample_win",
    )(q_bd, win_fm, tail_win)


def _heads_major(a2d, b, t, n):
    return a2d.reshape(b, t, n, HEAD_DIM).transpose(0, 2, 1, 3)


def _tokens_major(a_hm):
    b, n, t, d = a_hm.shape
    return a_hm.transpose(0, 2, 1, 3).reshape(b * t, n * d)


def _block_diag_q(q2d, b, t):
    q_hm = _heads_major(q2d, b, t, N_HEADS)
    onehot = jnp.asarray(np.eye(N_KV)[np.arange(N_HEADS) // N_REP], dtype=q2d.dtype)
    return jnp.einsum('bhtd,hg->bhtgd', q_hm, onehot).reshape(b, N_HEADS * t, KV_DIM)


def _own_group(o_bd, b, t):
    o = o_bd.reshape(b, N_KV, N_REP, t, N_KV, HEAD_DIM)
    o = jnp.stack([o[:, g, :, :, g, :] for g in range(N_KV)], axis=1)
    return o.transpose(0, 3, 1, 2, 4).reshape(b * t, Q_DIM)


def _nsa_prompt(x2d, b, t, g0, g1, w_in_bf, w_out_bf, cw):
    pos = jnp.arange(t)
    _, _, cmp_rows, sel_rows, win_rows, _, q_t, qr_t, v_t, gates_t = _nsa_proj(x2d, g0, w_in_bf, pos)
    assert t % STRIDE == 0 and t % L_SEL == 0
    n_ch = t // STRIDE
    n_cmp = n_ch - 1
    n_sel = t // L_SEL
    nsp = -(-n_sel // HEAD_DIM) * HEAD_DIM
    x_last = jnp.zeros((b, 8, _CHUNK_W), F32)
    (wk, pek, w1k, w2k), (wv, pev, w1v, w2v) = cw
    kcc, vcc = _compress(cmp_rows.reshape(b, n_ch, _CHUNK_W), x_last, wk, wv, pek, pev, w1k, w1v, w2k, w2v)
    gm = lambda a: a.reshape(b, -1, N_KV, HEAD_DIM).transpose(0, 2, 1, 3).astype(BF16)
    ocg_t, bias_t = _cmp_topk(q_t, gm(kcc), vcc.transpose(0, 2, 1).astype(BF16), gates_t, b, n_cmp, n_sel, nsp)
    sel3 = sel_rows.reshape(b, t, 2 * KV_DIM)
    win3 = win_rows.reshape(b, t, 2 * KV_DIM)
    onehot = jnp.asarray(np.arange(t)[:, None] // L_SEL == np.arange(nsp)[None, :], dtype=BF16)
    kp_gm = jnp.concatenate([gm(sel3[..., :KV_DIM]), jnp.broadcast_to(onehot, (b, N_KV, t, nsp))], axis=-1)
    x1 = _sel_win_out(qr_t, bias_t, kp_gm, gm(win3[..., :KV_DIM]), v_t, ocg_t, gates_t, w_out_bf, x2d, g1, b)
    rows5 = lambda a: a.reshape(1, b, t, 2, N_KV, HEAD_DIM)
    n_win = min(WINDOW, t)
    return x1, (rows5(cmp_rows), rows5(sel_rows), rows5(win_rows)[:, :, t - n_win:])


def _nsa_sample(x2d, b, t, g0, w_in_bf, cw, cache_cmp_l, cache_sel_l, cache_win_l, page_table):
    n_pages = page_table.shape[1]
    page = cache_cmp_l.shape[1]
    past_len = n_pages * page
    assert page % L_SEL == 0 and page % STRIDE == 0 and t <= STRIDE and t & (t - 1) == 0
    pos = past_len + jnp.arange(t)
    q, qr, cmp_rows, sel_rows, win_rows, gates = _nsa_proj(x2d, g0, w_in_bf, pos)[:6]
    row_w = 2 * KV_DIM
    fm = lambda a: a.transpose(0, 2, 3, 4, 1).reshape(a.shape[0], 2, KV_DIM, a.shape[1])
    n_past_ch = past_len // STRIDE
    n_cmp = n_past_ch
    n_sel = past_len // L_SEL + 1
    nsp = -(-n_sel // LANES) * LANES
    new3 = lambda a: a.reshape(b, t, row_w)
    x_last = jnp.pad(new3(cmp_rows), ((0, 0), (0, STRIDE - t), (0, 0))).reshape(b, 1, _CHUNK_W)
    x_last = jnp.pad(x_last, ((0, 0), (0, 7), (0, 0)))
    (wk, pek, w1k, w2k), (wv, pev, w1v, w2v) = cw
    kcc, vcc = _compress_paged(page_table, fm(cache_cmp_l), x_last, wk, wv, pek, pev, w1k, w1v, w2k, w2v)
    o_cmp_bd, sel01 = _sample_cmp(_block_diag_q(q, b, t), kcc, vcc, n_cmp, n_sel, nsp, past_len, t)
    qr_bd = _block_diag_q(qr, b, t)
    tail = lambda a: jnp.pad(new3(a), ((0, 0), (0, LANES - t), (0, 0)))
    o_sel_bd = _sample_sel(page_table, fm(cache_sel_l), qr_bd, sel01, tail(sel_rows), n_sel, t)
    win_fm = fm(cache_win_l)
    o_win_bd = _sample_win(qr_bd, win_fm, tail(win_rows), t)
    w_buf = win_fm.shape[3]
    new_fm = new3(win_rows).reshape(b, t, 2, KV_DIM).transpose(0, 2, 3, 1)
    new_win = jnp.concatenate([win_fm, new_fm], axis=3)[..., -w_buf:]
    new_win = new_win.reshape(b, 2, N_KV, HEAD_DIM, w_buf).transpose(0, 4, 1, 2, 3)
    rows5 = lambda a: a.reshape(1, b, -1, 2, N_KV, HEAD_DIM)
    caches = (rows5(cmp_rows), rows5(sel_rows), new_win[None])
    return _own_group(o_cmp_bd, b, t), _own_group(o_sel_bd, b, t), _own_group(o_win_bd, b, t), gates, caches


def kernel(x_prompt, x_sample, cache_cmp, cache_sel, cache_win, state_conv, page_table, p_prompt, p_sample,
           norm_g, w_ff1, w_ff2, w_ple, w_ple_gate, b_ple_gate,
           conv_w_pw1, conv_b_pw1, conv_w_dw, conv_b_dw, conv_ln_g, conv_ln_b, conv_w_pw2, conv_b_pw2,
           nsa_w_in, nsa_w_out, nsa_cmp_pe, nsa_cmp_w1, nsa_cmp_w2):
    depth = norm_g.shape[0]
    bf = lambda a: a.astype(BF16)
    row = lambda a: a.reshape(1, -1)
    w_ff1_bf, w_ff2_bf, w_ple_bf, w_gate_bf = bf(w_ff1), bf(w_ff2), bf(w_ple), bf(w_ple_gate)
    w_pw1_bf, w_pw2_bf, w_out_bf = bf(conv_w_pw1), bf(conv_w_pw2), bf(nsa_w_out)
    w_in_bf = bf(jnp.pad(nsa_w_in, ((0, 0), (0, 0), (0, _PROJ_W - nsa_w_in.shape[2]))))
    w_dw = jnp.pad(conv_w_dw, ((0, 0), (0, CONV_HALO - CONV_W), (0, 0)))

    def run(x, p, sample):
        b, t, _ = x.shape
        n = b * t
        x2d = x.reshape(n, D_MODEL)
        cmp_o, sel_o, win_o, conv_o = [], [], [], []
        for i in range(depth):
            g = lambda j: row(norm_g[i, j])
            if i % 2 == 0:
                c = i // 2
                u = _conv_front(x2d, g(0), w_pw1_bf[c], row(conv_b_pw1[c])).reshape(b, t, D_MODEL)
                if sample:
                    hist = jnp.pad(state_conv[c], ((0, 0), (CONV_HALO - (CONV_W - 1), 0), (0, 0)))
                else:
                    hist = jnp.zeros((b, CONV_HALO, D_MODEL), F32)
                tp = -(-t // CONV_HALO) * CONV_HALO
                ext = jnp.concatenate([hist, jnp.pad(u, ((0, 0), (0, tp - t), (0, 0)))], axis=1)
                x3 = jnp.pad(x2d.reshape(b, t, D_MODEL), ((0, 0), (0, tp - t), (0, 0)))
                x1 = _conv_back(ext, x3, w_dw[c], row(conv_b_dw[c]), row(conv_ln_g[c]), row(conv_ln_b[c]),
                                w_pw2_bf[c], row(conv_b_pw2[c]), g(1))[:, :t].reshape(n, D_MODEL)
                conv_o.append(ext[:, CONV_HALO + t - (CONV_W - 1):CONV_HALO + t][None])
            else:
                a = i // 2
                cw = _compress_weights(nsa_cmp_pe[a], nsa_cmp_w1[a], nsa_cmp_w2[a])
                if sample:
                    oc, osel, ow, gates, caches = _nsa_sample(x2d, b, t, g(0), w_in_bf[a], cw, cache_cmp[a],
                                                              cache_sel[a], cache_win[a], page_table)
                    x1 = _nsa_out(oc, osel, ow, gates, w_out_bf[a], x2d, g(1))
                else:
                    x1, caches = _nsa_prompt(x2d, b, t, g(0), g(1), w_in_bf[a], w_out_bf[a], cw)
                for dst, rows_ in zip((cmp_o, sel_o, win_o), caches):
                    dst.append(rows_)
            x2d = _ffn(x1, g(2), w_ff1_bf[i], w_ff2_bf[i], g(3), w_gate_bf[i], row(b_ple_gate[i]),
                       p[i].reshape(n, -1), w_ple_bf[i])
        cat = lambda parts: jnp.concatenate(parts, axis=0)
        return x2d.reshape(b, t, D_MODEL), cat(cmp_o), cat(sel_o), cat(win_o), cat(conv_o)

    y_p, cmp_p, sel_p, win_p, conv_p = run(x_prompt, p_prompt, False)
    y_s, cmp_s, sel_s, win_s, conv_s = run(x_sample, p_sample, True)
    return (y_p, y_s, cmp_p, cmp_s, sel_p, sel_s, win_p, win_s, conv_p, conv_s)
```

```python
import functools
import math

import numpy as np
import jax
import jax.numpy as jnp
from jax import lax
from jax.experimental import pallas as pl
from jax.experimental.pallas import tpu as pltpu

F32 = jnp.float32
BF16 = jnp.bfloat16

D_MODEL = 1024
N_HEADS = 16
N_KV = 4
N_REP = N_HEADS // N_KV
HEAD_DIM = 64
ROT_DIM = HEAD_DIM // 4
ROPE_THETA = 500000.0
L_CMP = 32
STRIDE = 16
CMP_HID = 2 * HEAD_DIM
L_SEL = 64
N_TOP = 16
WINDOW = 512
CONV_W = 31
Q_DIM = N_HEADS * HEAD_DIM
KV_DIM = N_KV * HEAD_DIM
GATE_DIM = 3 * N_HEADS
D_FF = 4 * D_MODEL
EPS = 1e-6
NEG = -1e30
FORCE = 1e6
SCALE = HEAD_DIM ** -0.5

LANES = 128
CONV_HALO = 32
VMEM_LIMIT = 56 * 1024 * 1024


def _pick(n, cands):
    for c in cands:
        if n % c == 0:
            return c
    raise ValueError(f"no tile in {cands} divides {n}")


def _const_spec(shape):
    nd = len(shape)
    return pl.BlockSpec(shape, lambda *_: (0,) * nd, pipeline_mode=pl.Buffered(1))


def _params(sem):
    return pltpu.CompilerParams(dimension_semantics=sem, vmem_limit_bytes=VMEM_LIMIT)


def _sigmoid(x):
    return 1.0 / (1.0 + jnp.exp(-x))


def _rms(x, g):
    return x * lax.rsqrt(jnp.mean(x * x, axis=-1, keepdims=True) + EPS) * g


def _dot(a, b):
    return jnp.dot(a, b, preferred_element_type=F32)


def _dot_nt(a, b):
    return lax.dot_general(a, b, (((1,), (1,)), ((), ())), preferred_element_type=F32)


def _split_bf16(x):
    hi = x.astype(BF16)
    lo = (x - hi.astype(F32)).astype(BF16)
    return hi, lo


def _conv_front_body(x_ref, g_ref, w_ref, b_ref, u_ref):
    h = _rms(x_ref[...], g_ref[...]).astype(BF16)
    z = _dot(h, w_ref[...]) + b_ref[...]
    u_ref[...] = z[:, :D_MODEL] * _sigmoid(z[:, D_MODEL:])


def _conv_front(x2d, g, w_bf, b):
    n = x2d.shape[0]
    tm = _pick(n, (512, 256, 128, 64, 32, 16, 8))
    return pl.pallas_call(
        _conv_front_body,
        grid=(n // tm,),
        in_specs=[pl.BlockSpec((tm, D_MODEL), lambda i: (i, 0)),
                  _const_spec((1, D_MODEL)),
                  _const_spec((D_MODEL, 2 * D_MODEL)),
                  _const_spec((1, 2 * D_MODEL))],
        out_specs=pl.BlockSpec((tm, D_MODEL), lambda i: (i, 0)),
        out_shape=jax.ShapeDtypeStruct((n, D_MODEL), F32),
        compiler_params=_params(("parallel",)),
        name="conv_front",
    )(x2d, g, w_bf, b)


_CONV_ROWS = 32
_CONV_COLS = 256


def _conv_back_body(hist_ref, prev_ref, main_ref, wdw_ref, bdw_ref, lng_ref, lnb_ref, w2_ref, b2_ref, x_ref, g1_ref,
                    o_ref, win_ref, y_ref):
    tt = main_ref.shape[1]
    @pl.when(pl.program_id(1) == 0)
    def _():
        win_ref[0:CONV_HALO, :] = hist_ref[0]

    @pl.when(pl.program_id(1) > 0)
    def _():
        win_ref[0:CONV_HALO, :] = prev_ref[0]

    win_ref[CONV_HALO:CONV_HALO + tt, :] = main_ref[0]
    first = CONV_HALO - (CONV_W - 1)

    sub = 8
    for r0 in range(0, tt, _CONV_ROWS):
        for c0 in range(0, D_MODEL, _CONV_COLS):
            cols = slice(c0, c0 + _CONV_COLS)
            acc = jnp.zeros((_CONV_ROWS, _CONV_COLS), F32)
            for s in range(sub):
                part = None
                for k in range(CONV_W):
                    if (first + k) % sub != s:
                        continue
                    base = r0 + first + k - s
                    term = win_ref[base:base + _CONV_ROWS + (sub if s else 0), cols] * wdw_ref[k:k + 1, cols]
                    part = term if part is None else part + term
                if part is not None:
                    acc = acc + part[s:s + _CONV_ROWS]
            y_ref[r0:r0 + _CONV_ROWS, cols] = acc + bdw_ref[:, cols]
    y = y_ref[...]
    yc = y - jnp.mean(y, axis=-1, keepdims=True)
    var = jnp.mean(yc * yc, axis=-1, keepdims=True)
    ln = yc * lax.rsqrt(var + EPS) * lng_ref[...] + lnb_ref[...]
    act = (ln * _sigmoid(ln)).astype(BF16)
    m = _dot(act, w2_ref[...]) + b2_ref[...]
    o_ref[0] = x_ref[0] + _rms(m, g1_ref[...])


def _conv_back(hist, u3d, x3d, wdw, bdw, lng, lnb, w2_bf, b2, g1):
    b, t, _ = x3d.shape
    tt = _pick(t, (256, 128, 64, 32))
    halo_blocks = tt // CONV_HALO
    return pl.pallas_call(
        _conv_back_body,
        grid=(b, t // tt),
        in_specs=[pl.BlockSpec((1, CONV_HALO, D_MODEL), lambda bi, i: (bi, 0, 0)),
                  pl.BlockSpec((1, CONV_HALO, D_MODEL), lambda bi, i: (bi, jnp.maximum(i * halo_blocks - 1, 0), 0)),
                  pl.BlockSpec((1, tt, D_MODEL), lambda bi, i: (bi, i, 0)),
                  _const_spec((CONV_HALO, D_MODEL)),
                  _const_spec((1, D_MODEL)), _const_spec((1, D_MODEL)), _const_spec((1, D_MODEL)),
                  _const_spec((D_MODEL, D_MODEL)), _const_spec((1, D_MODEL)),
                  pl.BlockSpec((1, tt, D_MODEL), lambda bi, i: (bi, i, 0)),
                  _const_spec((1, D_MODEL))],
        out_specs=pl.BlockSpec((1, tt, D_MODEL), lambda bi, i: (bi, i, 0)),
        out_shape=jax.ShapeDtypeStruct((b, t, D_MODEL), F32),
        scratch_shapes=[pltpu.VMEM((tt + CONV_HALO, D_MODEL), F32), pltpu.VMEM((tt, D_MODEL), F32)],
        compiler_params=_params(("parallel", "parallel")),
        name="conv_back",
    )(hist, u3d, u3d, wdw, bdw, lng, lnb, w2_bf, b2, x3d, g1)


_FF_CHUNK = 1024


def _ffn_body(x_ref, g2_ref, w1_ref, w2_ref, g3_ref, wg_ref, bg_ref, p_ref, wp_ref, o_ref):
    x = x_ref[...]
    h = _rms(x, g2_ref[...]).astype(BF16)
    f = jnp.zeros(x.shape, F32)
    for c in range(D_FF // _FF_CHUNK):
        a = jnp.maximum(_dot(h, w1_ref[:, c * _FF_CHUNK:(c + 1) * _FF_CHUNK]), 0.0)
        f = f + _dot((a * a).astype(BF16), w2_ref[c * _FF_CHUNK:(c + 1) * _FF_CHUNK, :])
    x2 = x + _rms(f, g3_ref[...])
    gate = _sigmoid(_dot(x2.astype(BF16), wg_ref[...]) + bg_ref[...])
    o_ref[...] = x2 + gate * _dot(p_ref[...].astype(BF16), wp_ref[...])


def _ffn(x2d, g2, w1_bf, w2_bf, g3, wg_bf, bg, p2d, wp_bf):
    n = x2d.shape[0]
    d_ple = p2d.shape[1]
    tm = _pick(n, (512, 256, 128, 64, 32, 16, 8))
    return pl.pallas_call(
        _ffn_body,
        grid=(n // tm,),
        in_specs=[pl.BlockSpec((tm, D_MODEL), lambda i: (i, 0)),
                  _const_spec((1, D_MODEL)),
                  _const_spec((D_MODEL, D_FF)), _const_spec((D_FF, D_MODEL)),
                  _const_spec((1, D_MODEL)),
                  _const_spec((D_MODEL, D_MODEL)), _const_spec((1, D_MODEL)),
                  pl.BlockSpec((tm, d_ple), lambda i: (i, 0)),
                  _const_spec((d_ple, D_MODEL))],
        out_specs=pl.BlockSpec((tm, D_MODEL), lambda i: (i, 0)),
        out_shape=jax.ShapeDtypeStruct((n, D_MODEL), F32),
        compiler_params=_params(("parallel",)),
        name="ffn_ple",
    )(x2d, g2, w1_bf, w2_bf, g3, wg_bf, bg, p2d, wp_bf)


_PROJ_W = Q_DIM + 6 * KV_DIM + LANES
LOG2E = 1.4426950408889634
V_ROWS = HEAD_DIM + 16
_VT_ROWS = 2 * N_KV * V_ROWS


def _rope_block(blk, c, su, sd):
    return blk * c + pltpu.roll(blk, ROT_DIM // 2, 1) * su + pltpu.roll(blk, LANES - ROT_DIM // 2, 1) * sd


def _nsa_proj_body(x_ref, g_ref, w_ref, c_ref, su_ref, sd_ref,
                   q_ref, qr_ref, cmp_ref, sel_ref, win_ref, gate_ref, qt_ref, qrt_ref, vt_ref, gatet_ref,
                   *cache_t_refs):
    h = _rms(x_ref[...], g_ref[...]).astype(BF16)
    z = _dot(h, w_ref[...])
    c, su, sd = c_ref[...], su_ref[...], sd_ref[...]
    for i in range(Q_DIM // LANES):
        cols = slice(i * LANES, (i + 1) * LANES)
        qs = z[:, cols] * SCALE
        qrs = _rope_block(z[:, cols], c, su, sd) * SCALE
        q_ref[:, cols] = qs.astype(BF16)
        qr_ref[:, cols] = qrs.astype(BF16)
        qt_ref[0, cols, :] = (qs * LOG2E).T.astype(BF16)
        qrt_ref[0, cols, :] = (qrs * LOG2E).T.astype(BF16)
    o = Q_DIM
    cmp_ref[...] = z[:, o:o + 2 * KV_DIM]
    if cache_t_refs:
        for i in range(2 * KV_DIM // LANES):
            cache_t_refs[0][0, i * LANES:(i + 1) * LANES, :] = z[:, o + i * LANES:o + (i + 1) * LANES].T
    o += 2 * KV_DIM
    ones = jnp.ones((V_ROWS - HEAD_DIM, x_ref.shape[0]), BF16)
    for n, dst in enumerate((sel_ref, win_ref)):
        for i in range(KV_DIM // LANES):
            k_rot = _rope_block(z[:, o + i * LANES:o + (i + 1) * LANES], c, su, sd)
            dst[:, i * LANES:(i + 1) * LANES] = k_rot
            vt = z[:, o + KV_DIM + i * LANES:o + KV_DIM + (i + 1) * LANES].T
            if cache_t_refs:
                cache_t_refs[1 + n][0, i * LANES:(i + 1) * LANES, :] = k_rot.T
                cache_t_refs[1 + n][0, KV_DIM + i * LANES:KV_DIM + (i + 1) * LANES, :] = vt
            for k in range(LANES // HEAD_DIM):
                base = (n * N_KV + i * (LANES // HEAD_DIM) + k) * V_ROWS
                vt_ref[0, base:base + HEAD_DIM, :] = vt[k * HEAD_DIM:(k + 1) * HEAD_DIM].astype(BF16)
                vt_ref[0, base + HEAD_DIM:base + V_ROWS, :] = ones
        dst[:, KV_DIM:] = z[:, o + KV_DIM:o + 2 * KV_DIM]
        o += 2 * KV_DIM
    gates = _sigmoid(z[:, o:o + LANES])
    gate_ref[...] = gates
    gatet_ref[0] = gates.T


def _rope_tables(pos, rows):
    half = ROT_DIM // 2
    inv = jnp.float32(ROPE_THETA) ** (-jnp.arange(half, dtype=F32) * (2.0 / ROT_DIM))
    ang = pos.astype(F32)[:, None] * inv[None, :]
    lane = np.arange(LANES)
    within = lane % HEAD_DIM
    cos = jnp.cos(ang)[:, lane % half]
    sin = jnp.sin(ang)[:, lane % half]
    c = jnp.where(within[None, :] < ROT_DIM, cos, 1.0)
    su = jnp.where((within[None, :] >= half) & (within[None, :] < ROT_DIM), sin, 0.0)
    sd = jnp.where(within[None, :] < half, -sin, 0.0)
    reps = rows // pos.shape[0]
    return tuple(jnp.tile(a, (reps, 1)) for a in (c, su, sd))


def _nsa_proj(x2d, g, w_bf, pos):
    n = x2d.shape[0]
    t = pos.shape[0]
    tm = _pick(n, (256, 128, 64, 32, 16, 8))
    if t >= tm:
        assert t % tm == 0
        tabs = _rope_tables(pos, t)
        nt = t // tm
        tab_spec = pl.BlockSpec((tm, LANES), lambda i: (i % nt, 0))
    else:
        assert tm % t == 0
        tabs = _rope_tables(pos, tm)
        tab_spec = _const_spec((tm, LANES))
    row = lambda w: pl.BlockSpec((tm, w), lambda i: (i, 0))
    tile = lambda rows: pl.BlockSpec((1, rows, tm), lambda i: (i, 0, 0))
    tiles = lambda rows, dt: jax.ShapeDtypeStruct((n // tm, rows, tm), dt)
    out_specs = [row(Q_DIM), row(Q_DIM), row(2 * KV_DIM), row(2 * KV_DIM), row(2 * KV_DIM), row(LANES),
                 tile(Q_DIM), tile(Q_DIM), tile(_VT_ROWS), tile(LANES)]
    out_shape = [jax.ShapeDtypeStruct((n, Q_DIM), BF16), jax.ShapeDtypeStruct((n, Q_DIM), BF16),
                 jax.ShapeDtypeStruct((n, 2 * KV_DIM), F32), jax.ShapeDtypeStruct((n, 2 * KV_DIM), F32),
                 jax.ShapeDtypeStruct((n, 2 * KV_DIM), F32), jax.ShapeDtypeStruct((n, LANES), F32),
                 tiles(Q_DIM, BF16), tiles(Q_DIM, BF16), tiles(_VT_ROWS, BF16), tiles(LANES, F32)]
    if t >= tm:
        out_specs += [pl.BlockSpec((1, 2 * KV_DIM, tm), lambda i: (i // nt, 0, i % nt))] * 3
        out_shape += [jax.ShapeDtypeStruct((n // t, 2 * KV_DIM, t), F32)] * 3
    return pl.pallas_call(
        _nsa_proj_body,
        grid=(n // tm,),
        in_specs=[row(D_MODEL), _const_spec((1, D_MODEL)), _const_spec((D_MODEL, _PROJ_W)),
                  tab_spec, tab_spec, tab_spec],
        out_specs=out_specs,
        out_shape=out_shape,
        compiler_params=_params(("parallel",)),
        name="nsa_proj",
    )(x2d, g, w_bf, *tabs)


_CHUNK_W = STRIDE * 2 * KV_DIM
_AB_W = 2 * N_KV * CMP_HID


_S_PER_DOT = 4


def _first_layer(get_tile, w_ref, rows):
    half = LANES // 2
    low = lax.broadcasted_iota(jnp.int32, (rows, LANES), 1) < half
    out = []
    for j in range(KV_DIM // LANES):
        acc = [jnp.zeros((rows, 2 * CMP_HID), F32) for _ in range(2)]
        for sq in range(STRIDE // _S_PER_DOT):
            even, odd = [], []
            for s in range(sq * _S_PER_DOT, (sq + 1) * _S_PER_DOT, 2):
                a, b = get_tile(s, j), get_tile(s + 1, j)
                even.append(jnp.where(low, a, pltpu.roll(b, half, 1)))
                odd.append(jnp.where(low, pltpu.roll(a, half, 1), b))
            acc[0] = acc[0] + _dot(jnp.concatenate(even, axis=1).astype(BF16), w_ref[sq])
            acc[1] = acc[1] + _dot(jnp.concatenate(odd, axis=1).astype(BF16), w_ref[sq])
        out += acc
    return out


def _store_ab(ab_ref, rows_slice, accs):
    for g, acc in enumerate(accs):
        ab_ref[rows_slice, g * CMP_HID:(g + 1) * CMP_HID] = acc[:, :CMP_HID]
        ab_ref[rows_slice, (N_KV + g) * CMP_HID:(N_KV + g + 1) * CMP_HID] = acc[:, CMP_HID:]


def _chunk_rows_tile(src_ref, kv):
    def get(s, j):
        lo = s * 2 * KV_DIM + kv * KV_DIM + j * LANES
        return src_ref[0, :, lo:lo + LANES]
    return get


def _compress_finish(xl_ref, per_kv, n_rows):
    half = _AB_W // 2
    for kv, (ab_ref, w_ref, pe_ref, w1_ref, w2_ref, out_ref) in enumerate(per_kv):
        _store_ab(ab_ref, slice(n_rows, n_rows + 8), _first_layer(_chunk_rows_tile(xl_ref, kv), w_ref, 8))
        c = jnp.sum(pe_ref[...] * w1_ref[...], axis=0, keepdims=True)
        c = jnp.concatenate([c] * N_KV, axis=1)
        pre = ab_ref[0:n_rows, 0:half] + ab_ref[1:n_rows + 1, half:] + c
        hid = (pre * _sigmoid(pre)).astype(BF16)
        out_ref[0] = _dot(hid, w2_ref[...])


def _compress_body(x_ref, xl_ref, wk_ref, wv_ref, pek_ref, pev_ref, w1k_ref, w1v_ref, w2k_ref, w2v_ref,
                   kcc_ref, vcc_ref, abk_ref, abv_ref):
    rt = x_ref.shape[1]
    i = pl.program_id(1)
    n_rt = pl.num_programs(1)
    rows = pl.ds(pl.multiple_of(i * rt, rt), rt)
    _store_ab(abk_ref, rows, _first_layer(_chunk_rows_tile(x_ref, 0), wk_ref, rt))
    _store_ab(abv_ref, rows, _first_layer(_chunk_rows_tile(x_ref, 1), wv_ref, rt))

    @pl.when(i == n_rt - 1)
    def _():
        _compress_finish(xl_ref, ((abk_ref, wk_ref, pek_ref, w1k_ref, w2k_ref, kcc_ref),
                                  (abv_ref, wv_ref, pev_ref, w1v_ref, w2v_ref, vcc_ref)), kcc_ref.shape[1])


def _compress(x_chunks, x_last, wk, wv, pek, pev, w1k, w1v, w2k, w2v):
    b, n, _ = x_chunks.shape
    rt = _pick(n, (128, 64, 32, 16, 8)) if n % 8 == 0 else n
    flat = L_CMP * HEAD_DIM
    out = jax.ShapeDtypeStruct((b, n, KV_DIM), F32)
    return pl.pallas_call(
        _compress_body,
        grid=(b, n // rt),
        in_specs=[pl.BlockSpec((1, rt, _CHUNK_W), lambda bi, i: (bi, i, 0)),
                  pl.BlockSpec((1, 8, _CHUNK_W), lambda bi, i: (bi, 0, 0)),
                  _const_spec(_W1_SHAPE), _const_spec(_W1_SHAPE),
                  _const_spec((flat, 1)), _const_spec((flat, 1)),
                  _const_spec((flat, CMP_HID)), _const_spec((flat, CMP_HID)),
                  _const_spec((N_KV * CMP_HID, KV_DIM)), _const_spec((N_KV * CMP_HID, KV_DIM))],
        out_specs=[pl.BlockSpec((1, n, KV_DIM), lambda bi, i: (bi, 0, 0))] * 2,
        out_shape=[out, out],
        scratch_shapes=[pltpu.VMEM((n + 8, _AB_W), F32), pltpu.VMEM((n + 8, _AB_W), F32)],
        compiler_params=_params(("parallel", "arbitrary")),
        name="compress",
    )(x_chunks, x_last, wk, wv, pek, pev, w1k, w1v, w2k, w2v)


_W1_SHAPE = (STRIDE // _S_PER_DOT, _S_PER_DOT * HEAD_DIM, 2 * CMP_HID)


def _compress_weights(cmp_pe, cmp_w1, cmp_w2):
    eye = jnp.eye(N_KV, dtype=F32)
    outs = []
    for kv in range(2):
        w1 = cmp_w1[kv].reshape(2, STRIDE // _S_PER_DOT, _S_PER_DOT, HEAD_DIM, CMP_HID)
        stacked = w1.transpose(1, 2, 3, 0, 4).reshape(_W1_SHAPE)
        w2 = jnp.einsum('gk,hd->ghkd', eye, cmp_w2[kv]).reshape(N_KV * CMP_HID, KV_DIM)
        outs.append((stacked.astype(BF16), cmp_pe[kv].reshape(L_CMP * HEAD_DIM, 1),
                     cmp_w1[kv].reshape(L_CMP * HEAD_DIM, CMP_HID), w2.astype(BF16)))
    return outs


def _select_blocks(imp_t, blk, qpos, n_real, k):
    cur = jnp.right_shift(qpos, int(math.log2(L_SEL)))
    forced = (blk == 0) | (blk == cur) | (blk == cur - 1)
    future = blk * L_SEL > qpos
    score = jnp.where(future, -jnp.inf, jnp.where(forced, FORCE, imp_t))
    score = jnp.where(blk < n_real, score, -jnp.inf)
    sub = 8
    rows = [score[i:i + sub] for i in range(0, score.shape[0], sub)]
    local = lax.broadcasted_iota(jnp.int32, rows[0].shape, 0)
    rank = [jnp.zeros(r.shape, jnp.int32) for r in rows]
    for j in range(n_real):
        sj = score[j:j + 1, :]
        for n, r in enumerate(rows):
            if n * sub > j:
                beats = sj >= r
            elif n * sub + sub - 1 <= j:
                beats = sj > r
            else:
                beats = jnp.where(local > j - n * sub, jnp.where(sj >= r, 1, 0), jnp.where(sj > r, 1, 0)) > 0
            rank[n] = rank[n] + jnp.where(beats, 1, 0)
    return (jnp.concatenate(rank, axis=0) < k) & (blk < n_real)


def _overlap_t(n_sel_pad, n_cmp_pad, n_sel, n_cmp):
    ci = np.arange(n_cmp_pad)[None, :] * STRIDE
    sj = np.arange(n_sel_pad)[:, None] * L_SEL
    ov = (ci < sj + L_SEL) & (ci + L_CMP > sj)
    ov &= (np.arange(n_cmp_pad)[None, :] < n_cmp) & (np.arange(n_sel_pad)[:, None] < n_sel)
    return jnp.asarray(ov, dtype=BF16)


def _head_rows(h):
    return slice(h * HEAD_DIM, (h + 1) * HEAD_DIM)


def _cmp_topk_body(n_sel, qt_ref, kcc_ref, vcct_ref, gt_ref, ovt_ref, ocg_ref, bias_ref):
    tq = qt_ref.shape[2]
    nc = kcc_ref.shape[2]
    nsp = ovt_ref.shape[0]
    t0 = pl.program_id(1) * tq
    wide = N_REP * tq
    qpos = t0 + (lax.broadcasted_iota(jnp.int32, (1, wide), 1) & (tq - 1))
    cmp_end = lax.broadcasted_iota(jnp.int32, (nc, 1), 0) * STRIDE + (L_CMP - 1)
    cmask = cmp_end <= qpos
    blk = lax.broadcasted_iota(jnp.int32, (nsp, tq), 0)
    qpos_t = t0 + lax.broadcasted_iota(jnp.int32, (nsp, tq), 1)
    ovt = ovt_ref[...]
    for g in range(N_KV):
        heads = range(g * N_REP, (g + 1) * N_REP)
        q = jnp.concatenate([qt_ref[0, _head_rows(h), :] for h in heads], axis=1)
        lm = jnp.where(cmask, _dot(kcc_ref[0, g], q), NEG)
        e = jnp.exp2(lm - jnp.max(lm, axis=0, keepdims=True))
        pc = jnp.where(cmask, e * (1.0 / jnp.sum(e, axis=0, keepdims=True)), 0.0)
        o = _dot(vcct_ref[0, _head_rows(g), :], pc.astype(BF16))
        pg = jnp.zeros((nc, tq), F32)
        for r, h in enumerate(heads):
            ocg_ref[0, _head_rows(h), :] = o[:, r * tq:(r + 1) * tq] * gt_ref[0, 3 * h:3 * h + 1, :]
            pg = pg + pc[:, r * tq:(r + 1) * tq]
        hi, lo = _split_bf16(pg)
        imp_t = _dot(ovt, hi) + _dot(ovt, lo)
        sel = _select_blocks(imp_t, blk, qpos_t, n_sel, min(N_TOP, n_sel))
        bias_ref[0, g] = jnp.where(sel, 0.0, NEG).astype(BF16)


def _cmp_topk(q_t, kcc_gm, vcc_t, gates_t, b, n_cmp, n_sel, nsp):
    tq = q_t.shape[2]
    n_t = q_t.shape[0] // b
    nc = kcc_gm.shape[2]
    ovt = _overlap_t(nsp, nc, n_sel, n_cmp)
    tile = lambda rows: pl.BlockSpec((1, rows, tq), lambda bi, i: (bi * n_t + i, 0, 0))
    return pl.pallas_call(
        functools.partial(_cmp_topk_body, n_sel),
        grid=(b, n_t),
        in_specs=[tile(Q_DIM),
                  pl.BlockSpec((1, N_KV, nc, HEAD_DIM), lambda bi, i: (bi, 0, 0, 0)),
                  pl.BlockSpec((1, KV_DIM, nc), lambda bi, i: (bi, 0, 0)),
                  tile(LANES), _const_spec((nsp, nc))],
        out_specs=[tile(Q_DIM), pl.BlockSpec((1, N_KV, nsp, tq), lambda bi, i: (bi, 0, 0, i))],
        out_shape=[jax.ShapeDtypeStruct((b * n_t, Q_DIM, tq), F32),
                   jax.ShapeDtypeStruct((b, N_KV, nsp, n_t * tq), BF16)],
        compiler_params=_params(("parallel", "parallel")),
        name="cmp_topk",
    )(q_t, kcc_gm, vcc_t, gates_t, ovt)


def _flash_init(m_ref, l_ref, acc_ref):
    m_ref[...] = jnp.full(m_ref.shape, NEG, F32)
    l_ref[...] = jnp.zeros(l_ref.shape, F32)
    acc_ref[...] = jnp.zeros(acc_ref.shape, F32)


def _flash_t(state, s, vt):
    m, acc = state
    m_new = jnp.maximum(m, jnp.max(s, axis=0, keepdims=True))
    return m_new, jnp.exp2(m - m_new) * acc + _dot(vt, jnp.exp2(s - m_new).astype(BF16))


def _v_rows(branch, g):
    base = (branch * N_KV + g) * V_ROWS
    return slice(base, base + V_ROWS)


def _sel_win_body(qrt_ref, bias_ref, kp_ref, kw_ref, vt_ref, ocg_ref, gt_ref, w_ref, x_ref, g1_ref,
                  o_ref, ot_ref):
    tq = qrt_ref.shape[2]
    kc = tq
    qt = pl.program_id(1)
    wide = N_REP * tq
    krow = lax.broadcasted_iota(jnp.int32, (kc, wide), 0)
    qcol = lax.broadcasted_iota(jnp.int32, (kc, wide), 1) & (tq - 1)
    n_back = WINDOW // kc
    fresh = (jnp.full((1, wide), NEG, F32), jnp.zeros((V_ROWS, wide), F32))

    for g in range(N_KV):
        heads = range(g * N_REP, (g + 1) * N_REP)
        qw = jnp.concatenate([qrt_ref[0, _head_rows(h), :] for h in heads], axis=1)
        qs = jnp.concatenate([qw, jnp.concatenate([bias_ref[0, g]] * N_REP, axis=1)], axis=0)

        def sel_chunk(j, state, diagonal):
            s = _dot(kp_ref[0, g, pl.ds(pl.multiple_of(j * kc, kc), kc), :], qs)
            return _flash_t(state, jnp.where(krow <= qcol, s, NEG) if diagonal else s, vt_ref[j, _v_rows(0, g), :])

        state = lax.fori_loop(0, qt // 2, lambda i, st: sel_chunk(2 * i + 1, sel_chunk(2 * i, st, False), False),
                              fresh)
        state = lax.cond(qt % 2 == 1, lambda st: sel_chunk(qt - 1, st, False), lambda st: st, state)
        _, acc_sel = sel_chunk(qt, state, True)

        def win_chunk(back, state):
            j = qt - back
            s = _dot(kw_ref[0, g, pl.ds(pl.multiple_of(j * kc, kc), kc), :], qw)
            if back == 0:
                s = jnp.where(krow <= qcol, s, NEG)
            elif back == n_back:
                s = jnp.where(krow > qcol, s, NEG)
            return _flash_t(state, s, vt_ref[j, _v_rows(1, g), :])

        state = win_chunk(0, fresh)
        for back in range(1, n_back + 1):
            state = lax.cond(qt >= back, functools.partial(win_chunk, back), lambda st: st, state)
        _, acc_win = state

        den = slice(HEAD_DIM, HEAD_DIM + 1)
        for r, h in enumerate(heads):
            cols = slice(r * tq, (r + 1) * tq)
            ot_ref[_head_rows(h), :] = (
                ocg_ref[0, _head_rows(h), :]
                + acc_sel[:HEAD_DIM, cols] * (gt_ref[0, 3 * h + 1:3 * h + 2, :] * (1.0 / acc_sel[den, cols]))
                + acc_win[:HEAD_DIM, cols] * (gt_ref[0, 3 * h + 2:3 * h + 3, :] * (1.0 / acc_win[den, cols])))

    m = _dot(ot_ref[...].T.astype(BF16), w_ref[...])
    o_ref[...] = x_ref[...] + _rms(m, g1_ref[...])


def _sel_win_out(qr_t, bias_t, kp_gm, kw_gm, v_t, ocg_t, gates_t, w_out_bf, x2d, g1, b):
    tq = qr_t.shape[2]
    n_t = qr_t.shape[0] // b
    t = n_t * tq
    cw = kp_gm.shape[3]
    nsp = cw - HEAD_DIM
    assert WINDOW % tq == 0
    tile = lambda rows: pl.BlockSpec((1, rows, tq), lambda bi, i: (bi * n_t + i, 0, 0))
    keys = lambda w: pl.BlockSpec((1, N_KV, t, w), lambda bi, i: (bi, 0, 0, 0))
    xrow = pl.BlockSpec((tq, D_MODEL), lambda bi, i: (bi * n_t + i, 0))
    return pl.pallas_call(
        _sel_win_body,
        grid=(b, n_t),
        in_specs=[tile(Q_DIM),
                  pl.BlockSpec((1, N_KV, nsp, tq), lambda bi, i: (bi, 0, 0, i)),
                  keys(cw), keys(HEAD_DIM),
                  pl.BlockSpec((n_t, _VT_ROWS, tq), lambda bi, i: (bi, 0, 0)),
                  tile(Q_DIM), tile(LANES),
                  _const_spec((Q_DIM, D_MODEL)), xrow, _const_spec((1, D_MODEL))],
        out_specs=xrow,
        out_shape=jax.ShapeDtypeStruct((b * t, D_MODEL), F32),
        scratch_shapes=[pltpu.VMEM((Q_DIM, tq), F32)],
        compiler_params=_params(("parallel", "parallel")),
        name="sel_win_out",
    )(qr_t, bias_t, kp_gm, kw_gm, v_t, ocg_t, gates_t, w_out_bf, x2d, g1)


def _nsa_out_body(oc_ref, os_ref, ow_ref, gate_ref, e_ref, w_ref, x_ref, g1_ref, o_ref):
    hi, lo = _split_bf16(gate_ref[...])
    o = jnp.zeros(oc_ref.shape, F32)
    for c, src in enumerate((oc_ref, os_ref, ow_ref)):
        o = o + (_dot(hi, e_ref[c]) + _dot(lo, e_ref[c])) * src[...]
    m = _dot(o.astype(BF16), w_ref[...])
    o_ref[...] = x_ref[...] + _rms(m, g1_ref[...])


def _gate_expand():
    e = np.zeros((3, LANES, Q_DIM), np.float32)
    for h in range(N_HEADS):
        for c in range(3):
            e[c, h * 3 + c, h * HEAD_DIM:(h + 1) * HEAD_DIM] = 1.0
    return jnp.asarray(e, dtype=BF16)


def _nsa_out(o_cmp, o_sel, o_win, gates, w_bf, x2d, g1):
    n = x2d.shape[0]
    tm = _pick(n, (256, 128, 64, 32, 16, 8))
    row = lambda w: pl.BlockSpec((tm, w), lambda i: (i, 0))
    return pl.pallas_call(
        _nsa_out_body,
        grid=(n // tm,),
        in_specs=[row(Q_DIM), row(Q_DIM), row(Q_DIM), row(LANES), _const_spec((3, LANES, Q_DIM)),
                  _const_spec((Q_DIM, D_MODEL)), row(D_MODEL), _const_spec((1, D_MODEL))],
        out_specs=row(D_MODEL),
        out_shape=jax.ShapeDtypeStruct((n, D_MODEL), F32),
        compiler_params=_params(("parallel",)),
        name="nsa_out",
    )(o_cmp, o_sel, o_win, gates, _gate_expand(), w_bf, x2d, g1)


class _PageRing:
    def __init__(self, pt_ref, cache_hbm, buf_ref, sem_ref, n_b, n_c):
        self.pt, self.cache, self.buf, self.sem = pt_ref, cache_hbm, buf_ref, sem_ref
        self.n_b, self.n_c, self.pg = n_b, n_c, buf_ref.shape[1]

    def _copies(self, step):
        b, c, slot = step // self.n_c, step % self.n_c, step % 2
        return [pltpu.make_async_copy(self.cache.at[self.pt[b, c * self.pg + p]], self.buf.at[slot, p],
                                      self.sem.at[slot]) for p in range(self.pg)]

    def acquire(self, b, c):
        step = b * self.n_c + c

        @pl.when(step == 0)
        def _():
            for cp in self._copies(step):
                cp.start()

        @pl.when(step + 1 < self.n_b * self.n_c)
        def _():
            for cp in self._copies(step + 1):
                cp.start()

        for cp in self._copies(step):
            cp.wait()
        return step % 2


def _compress_paged_body(n_b, n_c, pt_ref, cache_hbm, xl_ref, wk_ref, wv_ref, pek_ref, pev_ref, w1k_ref, w1v_ref,
                         w2k_ref, w2v_ref, kcc_ref, vcc_ref, buf_ref, sem_ref, xt_ref, abk_ref, abv_ref):
    pg, page = buf_ref.shape[1], buf_ref.shape[4]
    rows = pg * page // STRIDE
    b, c = pl.program_id(0), pl.program_id(1)
    slot = _PageRing(pt_ref, cache_hbm, buf_ref, sem_ref, n_b, n_c).acquire(b, c)
    r0 = pl.multiple_of(c * rows, rows)
    for kv, (ab_ref, w_ref) in enumerate(((abk_ref, wk_ref), (abv_ref, wv_ref))):
        for p in range(pg):
            for j in range(KV_DIM // LANES):
                xt_ref[j, p * page:(p + 1) * page, :] = buf_ref[slot, p, kv, j * LANES:(j + 1) * LANES, :].T
        tile = lambda s, j: xt_ref[j, pl.ds(s, rows, stride=STRIDE), :]
        _store_ab(ab_ref, pl.ds(r0, rows), _first_layer(tile, w_ref, rows))

    @pl.when(c == n_c - 1)
    def _():
        _compress_finish(xl_ref, ((abk_ref, wk_ref, pek_ref, w1k_ref, w2k_ref, kcc_ref),
                                  (abv_ref, wv_ref, pev_ref, w1v_ref, w2v_ref, vcc_ref)), kcc_ref.shape[1])


def _compress_paged(page_table, cache_fm, x_last, wk, wv, pek, pev, w1k, w1v, w2k, w2v):
    b, n_pages = page_table.shape
    page = cache_fm.shape[3]
    pg = _pick(n_pages, (16, 8, 4, 2, 1))
    n_c = n_pages // pg
    n = n_pages * page // STRIDE
    flat = L_CMP * HEAD_DIM
    out = jax.ShapeDtypeStruct((b, n, KV_DIM), F32)
    const = _const_spec
    return pl.pallas_call(
        functools.partial(_compress_paged_body, b, n_c),
        grid_spec=pltpu.PrefetchScalarGridSpec(
            num_scalar_prefetch=1,
            grid=(b, n_c),
            in_specs=[pl.BlockSpec(memory_space=pl.ANY),
                      pl.BlockSpec((1, 8, _CHUNK_W), lambda bi, c, pt: (bi, 0, 0)),
                      const(_W1_SHAPE), const(_W1_SHAPE),
                      const((flat, 1)), const((flat, 1)), const((flat, CMP_HID)), const((flat, CMP_HID)),
                      const((N_KV * CMP_HID, KV_DIM)), const((N_KV * CMP_HID, KV_DIM))],
            out_specs=[pl.BlockSpec((1, n, KV_DIM), lambda bi, c, pt: (bi, 0, 0))] * 2,
            scratch_shapes=[pltpu.VMEM((2, pg, 2, KV_DIM, page), F32), pltpu.SemaphoreType.DMA((2,)),
                            pltpu.VMEM((KV_DIM // LANES, pg * page, LANES), F32),
                            pltpu.VMEM((n + 8, _AB_W), F32), pltpu.VMEM((n + 8, _AB_W), F32)]),
        out_shape=[out, out],
        compiler_params=_params(("arbitrary", "arbitrary")),
        name="compress_paged",
    )(page_table, cache_fm, x_last, wk, wv, pek, pev, w1k, w1v, w2k, w2v)


def _softmax_rows(s):
    e = jnp.exp(s - jnp.max(s, axis=-1, keepdims=True))
    return e / jnp.sum(e, axis=-1, keepdims=True)


def _sample_cmp_body(n_sel, past_len, t_new, q_ref, kcc_ref, vcc_ref, ovt_ref, rep_ref, o_ref, sel_ref):
    rows = q_ref.shape[1]
    nc = kcc_ref.shape[1]
    nsp = ovt_ref.shape[0]
    gq = N_KV * t_new
    tq_col = lax.broadcasted_iota(jnp.int32, (rows, 1), 0) & (t_new - 1)
    cmp_end = lax.broadcasted_iota(jnp.int32, (1, nc), 1) * STRIDE + (L_CMP - 1)
    cmask = cmp_end <= past_len + tq_col
    lm = jnp.where(cmask, _dot_nt(q_ref[0], kcc_ref[0].astype(BF16)), NEG)
    pc = jnp.where(cmask, _softmax_rows(lm), 0.0)
    o_ref[0] = _dot(pc.astype(BF16), vcc_ref[0].astype(BF16))
    pg = pc.reshape(N_KV, N_REP, t_new, nc).sum(axis=1).reshape(gq, nc)
    hi, lo = _split_bf16(pg)
    imp_t = _dot_nt(ovt_ref[...], hi) + _dot_nt(ovt_ref[...], lo)
    blk = lax.broadcasted_iota(jnp.int32, (nsp, gq), 0)
    qpos = past_len + (lax.broadcasted_iota(jnp.int32, (nsp, gq), 1) & (t_new - 1))
    sel = _select_blocks(imp_t, blk, qpos, n_sel, min(N_TOP, n_sel))
    sel_ref[0] = _dot_nt(rep_ref[...], jnp.where(sel, 1.0, 0.0).astype(BF16)).astype(BF16)


def _sample_cmp(q_bd, kcc, vcc, n_cmp, n_sel, nsp, past_len, t_new):
    b, rows, _ = q_bd.shape
    nc = kcc.shape[1]
    gq = N_KV * t_new
    ovt = _overlap_t(nsp, nc, n_sel, n_cmp)
    rep = np.zeros((rows, gq), np.float32)
    for h in range(N_HEADS):
        for t in range(t_new):
            rep[h * t_new + t, (h // N_REP) * t_new + t] = 1.0
    return pl.pallas_call(
        functools.partial(_sample_cmp_body, n_sel, past_len, t_new),
        grid=(b,),
        in_specs=[pl.BlockSpec((1, rows, KV_DIM), lambda bi: (bi, 0, 0)),
                  pl.BlockSpec((1, nc, KV_DIM), lambda bi: (bi, 0, 0)),
                  pl.BlockSpec((1, nc, KV_DIM), lambda bi: (bi, 0, 0)),
                  _const_spec((nsp, nc)), _const_spec((rows, gq))],
        out_specs=[pl.BlockSpec((1, rows, KV_DIM), lambda bi: (bi, 0, 0)),
                   pl.BlockSpec((1, rows, nsp), lambda bi: (bi, 0, 0))],
        out_shape=[jax.ShapeDtypeStruct((b, rows, KV_DIM), F32), jax.ShapeDtypeStruct((b, rows, nsp), BF16)],
        compiler_params=_params(("parallel",)),
        name="sample_cmp",
    )(q_bd, kcc, vcc, ovt, jnp.asarray(rep, dtype=BF16))


def _flash_update(s, pv_fn, m_ref, l_ref, acc_ref):
    m_prev = m_ref[...]
    m_new = jnp.maximum(m_prev, jnp.max(s, axis=-1, keepdims=True))
    alpha = jnp.exp(m_prev - m_new)
    p = jnp.exp(s - m_new)
    l_ref[...] = alpha * l_ref[...] + jnp.sum(p, axis=-1, keepdims=True)
    acc_ref[...] = alpha * acc_ref[...] + pv_fn(p.astype(BF16))
    m_ref[...] = m_new


def _sample_sel_body(t_new, n_b, n_c, pt_ref, cache_hbm, q_ref, sel_ref, e_ref, et_ref, tail_ref, o_ref,
                     buf_ref, sem_ref, m_ref, l_ref, acc_ref):
    rows = q_ref.shape[1]
    pg, page = buf_ref.shape[1], buf_ref.shape[4]
    b, c = pl.program_id(0), pl.program_id(1)
    q = q_ref[0]

    @pl.when(c == 0)
    def _():
        _flash_init(m_ref, l_ref, acc_ref)

    @pl.when(c < n_c)
    def _():
        slot = _PageRing(pt_ref, cache_hbm, buf_ref, sem_ref, n_b, n_c).acquire(b, c)
        s = jnp.concatenate([_dot(q, buf_ref[slot, p, 0].astype(BF16)) for p in range(pg)], axis=1)
        s = jnp.where(_dot(sel_ref[0], e_ref[...]) > 0.5, s, NEG)

        def pv(p_bf):
            return sum(_dot_nt(p_bf[:, p * page:(p + 1) * page], buf_ref[slot, p, 1].astype(BF16))
                       for p in range(pg))

        _flash_update(s, pv, m_ref, l_ref, acc_ref)

    @pl.when(c == n_c)
    def _():
        kv = tail_ref[0]
        nk = kv.shape[0]
        s = _dot_nt(q, kv[:, :KV_DIM].astype(BF16))
        tq = lax.broadcasted_iota(jnp.int32, (rows, 1), 0) & (t_new - 1)
        ok = (_dot(sel_ref[0], et_ref[...]) > 0.5) & (lax.broadcasted_iota(jnp.int32, (1, nk), 1) <= tq)
        _flash_update(jnp.where(ok, s, NEG), lambda p_bf: _dot(p_bf, kv[:, KV_DIM:].astype(BF16)),
                      m_ref, l_ref, acc_ref)
        o_ref[0] = acc_ref[...] / l_ref[...]


def _sample_sel(page_table, cache_fm, q_bd, sel01, tail_sel, n_sel, t_new):
    b, rows, _ = q_bd.shape
    nsp = sel01.shape[2]
    n_pages = page_table.shape[1]
    page = cache_fm.shape[3]
    pg = _pick(n_pages, (8, 4, 2, 1))
    n_c = n_pages // pg
    kc = pg * page
    nk = tail_sel.shape[1]
    key_blk = np.arange(n_pages * page) // L_SEL
    e = (np.arange(nsp)[:, None] == key_blk[None, :]).astype(np.float32)
    et = np.zeros((nsp, nk), np.float32)
    et[n_sel - 1, :] = 1.0
    return pl.pallas_call(
        functools.partial(_sample_sel_body, t_new, b, n_c),
        grid_spec=pltpu.PrefetchScalarGridSpec(
            num_scalar_prefetch=1,
            grid=(b, n_c + 1),
            in_specs=[pl.BlockSpec(memory_space=pl.ANY),
                      pl.BlockSpec((1, rows, KV_DIM), lambda bi, c, pt: (bi, 0, 0)),
                      pl.BlockSpec((1, rows, nsp), lambda bi, c, pt: (bi, 0, 0)),
                      pl.BlockSpec((nsp, kc), lambda bi, c, pt: (0, jnp.minimum(c, n_c - 1))),
                      _const_spec((nsp, nk)),
                      pl.BlockSpec((1, nk, 2 * KV_DIM), lambda bi, c, pt: (bi, 0, 0))],
            out_specs=pl.BlockSpec((1, rows, KV_DIM), lambda bi, c, pt: (bi, 0, 0)),
            scratch_shapes=[pltpu.VMEM((2, pg, 2, KV_DIM, page), F32), pltpu.SemaphoreType.DMA((2,)),
                            pltpu.VMEM((rows, 1), F32), pltpu.VMEM((rows, 1), F32),
                            pltpu.VMEM((rows, KV_DIM), F32)]),
        out_shape=jax.ShapeDtypeStruct((b, rows, KV_DIM), F32),
        compiler_params=_params(("arbitrary", "arbitrary")),
        name="sample_sel",
    )(page_table, cache_fm, q_bd, sel01, jnp.asarray(e, dtype=BF16), jnp.asarray(et, dtype=BF16), tail_sel)


def _sample_win_body(t_new, q_ref, kv_ref, tail_ref, o_ref):
    rows = q_ref.shape[1]
    wb = kv_ref.shape[3]
    nk = tail_ref.shape[1]
    q = q_ref[0]
    tq = lax.broadcasted_iota(jnp.int32, (rows, 1), 0) & (t_new - 1)
    tail = tail_ref[0]
    s_old = jnp.where(lax.broadcasted_iota(jnp.int32, (1, wb), 1) > tq - WINDOW + wb,
                      _dot(q, kv_ref[0, 0].astype(BF16)), NEG)
    s_new = jnp.where(lax.broadcasted_iota(jnp.int32, (1, nk), 1) <= tq,
                      _dot_nt(q, tail[:, :KV_DIM].astype(BF16)), NEG)
    m = jnp.maximum(jnp.max(s_old, axis=-1, keepdims=True), jnp.max(s_new, axis=-1, keepdims=True))
    p_old = jnp.exp(s_old - m)
    p_new = jnp.exp(s_new - m)
    den = jnp.sum(p_old, axis=-1, keepdims=True) + jnp.sum(p_new, axis=-1, keepdims=True)
    o = _dot_nt(p_old.astype(BF16), kv_ref[0, 1].astype(BF16)) + _dot(p_new.astype(BF16), tail[:, KV_DIM:].astype(BF16))
    o_ref[0] = o / den


def _sample_win(q_bd, win_fm, tail_win, t_new):
    b, rows, _ = q_bd.shape
    wb = win_fm.shape[3]
    nk = tail_win.shape[1]
    return pl.pallas_call(
        functools.partial(_sample_win_body, t_new),
        grid=(b,),
        in_specs=[pl.BlockSpec((1, rows, KV_DIM), lambda bi: (bi, 0, 0)),
                  pl.BlockSpec((1, 2, KV_DIM, wb), lambda bi: (bi, 0, 0, 0)),
                  pl.BlockSpec((1, nk, 2 * KV_DIM), lambda bi: (bi, 0, 0))],
        out_specs=pl.BlockSpec((1, rows, KV_DIM), lambda bi: (bi, 0, 0)),
        out_shape=jax.ShapeDtypeStruct((b, rows, KV_DIM), F32),
        compiler_params=_params(("parallel",)),
        name="sample_win",
    )(q_bd, win_fm, tail_win)


def _heads_major(a2d, b, t, n):
    return a2d.reshape(b, t, n, HEAD_DIM).transpose(0, 2, 1, 3)


def _tokens_major(a_hm):
    b, n, t, d = a_hm.shape
    return a_hm.transpose(0, 2, 1, 3).reshape(b * t, n * d)


def _block_diag_q(q2d, b, t):
    q_hm = _heads_major(q2d, b, t, N_HEADS)
    onehot = jnp.asarray(np.eye(N_KV)[np.arange(N_HEADS) // N_REP], dtype=q2d.dtype)
    return jnp.einsum('bhtd,hg->bhtgd', q_hm, onehot).reshape(b, N_HEADS * t, KV_DIM)


def _own_group(o_bd, b, t):
    o = o_bd.reshape(b, N_KV, N_REP, t, N_KV, HEAD_DIM)
    o = jnp.stack([o[:, g, :, :, g, :] for g in range(N_KV)], axis=1)
    return o.transpose(0, 3, 1, 2, 4).reshape(b * t, Q_DIM)


def _nsa_prompt(x2d, b, t, g0, g1, w_in_bf, w_out_bf, cw):
    pos = jnp.arange(t)
    (_, _, cmp_rows, sel_rows, win_rows, _, q_t, qr_t, v_t, gates_t,
     cmp_fm, sel_fm, win_fm) = _nsa_proj(x2d, g0, w_in_bf, pos)
    assert t % STRIDE == 0 and t % L_SEL == 0
    n_ch = t // STRIDE
    n_cmp = n_ch - 1
    n_sel = t // L_SEL
    nsp = -(-n_sel // HEAD_DIM) * HEAD_DIM
    x_last = jnp.zeros((b, 8, _CHUNK_W), F32)
    (wk, pek, w1k, w2k), (wv, pev, w1v, w2v) = cw
    kcc, vcc = _compress(cmp_rows.reshape(b, n_ch, _CHUNK_W), x_last, wk, wv, pek, pev, w1k, w1v, w2k, w2v)
    gm = lambda a: a.reshape(b, -1, N_KV, HEAD_DIM).transpose(0, 2, 1, 3).astype(BF16)
    ocg_t, bias_t = _cmp_topk(q_t, gm(kcc), vcc.transpose(0, 2, 1).astype(BF16), gates_t, b, n_cmp, n_sel, nsp)
    sel3 = sel_rows.reshape(b, t, 2 * KV_DIM)
    win3 = win_rows.reshape(b, t, 2 * KV_DIM)
    onehot = jnp.asarray(np.arange(t)[:, None] // L_SEL == np.arange(nsp)[None, :], dtype=BF16)
    kp_gm = jnp.concatenate([gm(sel3[..., :KV_DIM]), jnp.broadcast_to(onehot, (b, N_KV, t, nsp))], axis=-1)
    x1 = _sel_win_out(qr_t, bias_t, kp_gm, gm(win3[..., :KV_DIM]), v_t, ocg_t, gates_t, w_out_bf, x2d, g1, b)
    rows5 = lambda a: a.reshape(b, 2, N_KV, HEAD_DIM, -1).transpose(0, 4, 1, 2, 3)[None]
    n_win = min(WINDOW, t)
    return x1, (rows5(cmp_fm), rows5(sel_fm), rows5(win_fm[:, :, t - n_win:]))


def _nsa_sample(x2d, b, t, g0, w_in_bf, cw, cache_cmp_l, cache_sel_l, cache_win_l, page_table):
    n_pages = page_table.shape[1]
    page = cache_cmp_l.shape[1]
    past_len = n_pages * page
    assert page % L_SEL == 0 and page % STRIDE == 0 and t <= STRIDE and t & (t - 1) == 0
    pos = past_len + jnp.arange(t)
    q, qr, cmp_rows, sel_rows, win_rows, gates = _nsa_proj(x2d, g0, w_in_bf, pos)[:6]
    row_w = 2 * KV_DIM
    fm = lambda a: a.transpose(0, 2, 3, 4, 1).reshape(a.shape[0], 2, KV_DIM, a.shape[1])
    n_past_ch = past_len // STRIDE
    n_cmp = n_past_ch
    n_sel = past_len // L_SEL + 1
    nsp = -(-n_sel // LANES) * LANES
    new3 = lambda a: a.reshape(b, t, row_w)
    x_last = jnp.pad(new3(cmp_rows), ((0, 0), (0, STRIDE - t), (0, 0))).reshape(b, 1, _CHUNK_W)
    x_last = jnp.pad(x_last, ((0, 0), (0, 7), (0, 0)))
    (wk, pek, w1k, w2k), (wv, pev, w1v, w2v) = cw
    kcc, vcc = _compress_paged(page_table, fm(cache_cmp_l), x_last, wk, wv, pek, pev, w1k, w1v, w2k, w2v)
    o_cmp_bd, sel01 = _sample_cmp(_block_diag_q(q, b, t), kcc, vcc, n_cmp, n_sel, nsp, past_len, t)
    qr_bd = _block_diag_q(qr, b, t)
    tail = lambda a: jnp.pad(new3(a), ((0, 0), (0, LANES - t), (0, 0)))
    o_sel_bd = _sample_sel(page_table, fm(cache_sel_l), qr_bd, sel01, tail(sel_rows), n_sel, t)
    win_fm = fm(cache_win_l)
    o_win_bd = _sample_win(qr_bd, win_fm, tail(win_rows), t)
    w_buf = win_fm.shape[3]
    new_fm = new3(win_rows).reshape(b, t, 2, KV_DIM).transpose(0, 2, 3, 1)
    new_win = jnp.concatenate([win_fm, new_fm], axis=3)[..., -w_buf:]
    new_win = new_win.reshape(b, 2, N_KV, HEAD_DIM, w_buf).transpose(0, 4, 1, 2, 3)
    rows5 = lambda a: a.reshape(1, b, -1, 2, N_KV, HEAD_DIM)
    caches = (rows5(cmp_rows), rows5(sel_rows), new_win[None])
    return _own_group(o_cmp_bd, b, t), _own_group(o_sel_bd, b, t), _own_group(o_win_bd, b, t), gates, caches


def kernel(x_prompt, x_sample, cache_cmp, cache_sel, cache_win, state_conv, page_table, p_prompt, p_sample,
           norm_g, w_ff1, w_ff2, w_ple, w_ple_gate, b_ple_gate,
           conv_w_pw1, conv_b_pw1, conv_w_dw, conv_b_dw, conv_ln_g, conv_ln_b, conv_w_pw2, conv_b_pw2,
           nsa_w_in, nsa_w_out, nsa_cmp_pe, nsa_cmp_w1, nsa_cmp_w2):
    depth = norm_g.shape[0]
    bf = lambda a: a.astype(BF16)
    row = lambda a: a.reshape(1, -1)
    w_ff1_bf, w_ff2_bf, w_ple_bf, w_gate_bf = bf(w_ff1), bf(w_ff2), bf(w_ple), bf(w_ple_gate)
    w_pw1_bf, w_pw2_bf, w_out_bf = bf(conv_w_pw1), bf(conv_w_pw2), bf(nsa_w_out)
    w_in_bf = bf(jnp.pad(nsa_w_in, ((0, 0), (0, 0), (0, _PROJ_W - nsa_w_in.shape[2]))))
    w_dw = jnp.pad(conv_w_dw, ((0, 0), (0, CONV_HALO - CONV_W), (0, 0)))

    def run(x, p, sample):
        b, t, _ = x.shape
        n = b * t
        x2d = x.reshape(n, D_MODEL)
        cmp_o, sel_o, win_o, conv_o = [], [], [], []
        for i in range(depth):
            g = lambda j: row(norm_g[i, j])
            if i % 2 == 0:
                c = i // 2
                u = _conv_front(x2d, g(0), w_pw1_bf[c], row(conv_b_pw1[c])).reshape(b, t, D_MODEL)
                if sample:
                    hist = jnp.pad(state_conv[c], ((0, 0), (CONV_HALO - (CONV_W - 1), 0), (0, 0)))
                else:
                    hist = jnp.zeros((b, CONV_HALO, D_MODEL), F32)
                tp = -(-t // CONV_HALO) * CONV_HALO
                pad_t = lambda a: jnp.pad(a, ((0, 0), (0, tp - t), (0, 0))) if tp > t else a
                x1 = _conv_back(hist, pad_t(u), pad_t(x2d.reshape(b, t, D_MODEL)), w_dw[c], row(conv_b_dw[c]),
                                row(conv_ln_g[c]), row(conv_ln_b[c]), w_pw2_bf[c], row(conv_b_pw2[c]),
                                g(1))[:, :t].reshape(n, D_MODEL)
                keep = CONV_W - 1
                conv_o.append(jnp.concatenate([hist, u], axis=1)[:, CONV_HALO + t - keep:][None])
            else:
                a = i // 2
                cw = _compress_weights(nsa_cmp_pe[a], nsa_cmp_w1[a], nsa_cmp_w2[a])
                if sample:
                    oc, osel, ow, gates, caches = _nsa_sample(x2d, b, t, g(0), w_in_bf[a], cw, cache_cmp[a],
                                                              cache_sel[a], cache_win[a], page_table)
                    x1 = _nsa_out(oc, osel, ow, gates, w_out_bf[a], x2d, g(1))
                else:
                    x1, caches = _nsa_prompt(x2d, b, t, g(0), g(1), w_in_bf[a], w_out_bf[a], cw)
                for dst, rows_ in zip((cmp_o, sel_o, win_o), caches):
                    dst.append(rows_)
            x2d = _ffn(x1, g(2), w_ff1_bf[i], w_ff2_bf[i], g(3), w_gate_bf[i], row(b_ple_gate[i]),
                       p[i].reshape(n, -1), w_ple_bf[i])
        cat = lambda parts: jnp.concatenate(parts, axis=0)
        return x2d.reshape(b, t, D_MODEL), cat(cmp_o), cat(sel_o), cat(win_o), cat(conv_o)

    y_p, cmp_p, sel_p, win_p, conv_p = run(x_prompt, p_prompt, False)
    y_s, cmp_s, sel_s, win_s, conv_s = run(x_sample, p_sample, True)
    return (y_p, y_s, cmp_p, cmp_s, sel_p, sel_s, win_p, win_s, conv_p, conv_s)
```

```python
import functools
import math

import numpy as np
import jax
import jax.numpy as jnp
from jax import lax
from jax.experimental import pallas as pl
from jax.experimental.pallas import tpu as pltpu

F32 = jnp.float32
BF16 = jnp.bfloat16

D_MODEL = 1024
N_HEADS = 16
N_KV = 4
N_REP = N_HEADS // N_KV
HEAD_DIM = 64
ROT_DIM = HEAD_DIM // 4
ROPE_THETA = 500000.0
L_CMP = 32
STRIDE = 16
CMP_HID = 2 * HEAD_DIM
L_SEL = 64
N_TOP = 16
WINDOW = 512
CONV_W = 31
Q_DIM = N_HEADS * HEAD_DIM
KV_DIM = N_KV * HEAD_DIM
GATE_DIM = 3 * N_HEADS
D_FF = 4 * D_MODEL
EPS = 1e-6
NEG = -1e30
FORCE = 1e6
SCALE = HEAD_DIM ** -0.5

LANES = 128
CONV_HALO = 32
VMEM_LIMIT = 56 * 1024 * 1024


def _pick(n, cands):
    for c in cands:
        if n % c == 0:
            return c
    raise ValueError(f"no tile in {cands} divides {n}")


def _const_spec(shape):
    nd = len(shape)
    return pl.BlockSpec(shape, lambda *_: (0,) * nd, pipeline_mode=pl.Buffered(1))


def _params(sem):
    return pltpu.CompilerParams(dimension_semantics=sem, vmem_limit_bytes=VMEM_LIMIT)


def _sigmoid(x):
    return 1.0 / (1.0 + jnp.exp(-x))


def _rms(x, g):
    return x * lax.rsqrt(jnp.mean(x * x, axis=-1, keepdims=True) + EPS) * g


def _dot(a, b):
    return jnp.dot(a, b, preferred_element_type=F32)


def _dot_nt(a, b):
    return lax.dot_general(a, b, (((1,), (1,)), ((), ())), preferred_element_type=F32)


def _split_bf16(x):
    hi = x.astype(BF16)
    lo = (x - hi.astype(F32)).astype(BF16)
    return hi, lo


def _conv_front_body(x_ref, g_ref, w_ref, b_ref, u_ref):
    h = _rms(x_ref[...], g_ref[...]).astype(BF16)
    z = _dot(h, w_ref[...]) + b_ref[...]
    u_ref[...] = z[:, :D_MODEL] * _sigmoid(z[:, D_MODEL:])


def _conv_front(x2d, g, w_bf, b):
    n = x2d.shape[0]
    tm = _pick(n, (512, 256, 128, 64, 32, 16, 8))
    return pl.pallas_call(
        _conv_front_body,
        grid=(n // tm,),
        in_specs=[pl.BlockSpec((tm, D_MODEL), lambda i: (i, 0)),
                  _const_spec((1, D_MODEL)),
                  _const_spec((D_MODEL, 2 * D_MODEL)),
                  _const_spec((1, 2 * D_MODEL))],
        out_specs=pl.BlockSpec((tm, D_MODEL), lambda i: (i, 0)),
        out_shape=jax.ShapeDtypeStruct((n, D_MODEL), F32),
        compiler_params=_params(("parallel",)),
        name="conv_front",
    )(x2d, g, w_bf, b)


_CONV_ROWS = 32
_CONV_COLS = 256


def _conv_back_body(hist_ref, prev_ref, main_ref, wdw_ref, bdw_ref, lng_ref, lnb_ref, w2_ref, b2_ref, x_ref, g1_ref,
                    o_ref, win_ref, y_ref):
    tt = main_ref.shape[1]
    @pl.when(pl.program_id(1) == 0)
    def _():
        win_ref[0:CONV_HALO, :] = hist_ref[0]

    @pl.when(pl.program_id(1) > 0)
    def _():
        win_ref[0:CONV_HALO, :] = prev_ref[0]

    win_ref[CONV_HALO:CONV_HALO + tt, :] = main_ref[0]
    first = CONV_HALO - (CONV_W - 1)

    sub = 8
    for r0 in range(0, tt, _CONV_ROWS):
        for c0 in range(0, D_MODEL, _CONV_COLS):
            cols = slice(c0, c0 + _CONV_COLS)
            acc = jnp.zeros((_CONV_ROWS, _CONV_COLS), F32)
            for s in range(sub):
                part = None
                for k in range(CONV_W):
                    if (first + k) % sub != s:
                        continue
                    base = r0 + first + k - s
                    term = win_ref[base:base + _CONV_ROWS + (sub if s else 0), cols] * wdw_ref[k:k + 1, cols]
                    part = term if part is None else part + term
                if part is not None:
                    acc = acc + part[s:s + _CONV_ROWS]
            y_ref[r0:r0 + _CONV_ROWS, cols] = acc + bdw_ref[:, cols]
    y = y_ref[...]
    yc = y - jnp.mean(y, axis=-1, keepdims=True)
    var = jnp.mean(yc * yc, axis=-1, keepdims=True)
    ln = yc * lax.rsqrt(var + EPS) * lng_ref[...] + lnb_ref[...]
    act = (ln * _sigmoid(ln)).astype(BF16)
    m = _dot(act, w2_ref[...]) + b2_ref[...]
    o_ref[0] = x_ref[0] + _rms(m, g1_ref[...])


def _conv_back(hist, u3d, x3d, wdw, bdw, lng, lnb, w2_bf, b2, g1):
    b, t, _ = x3d.shape
    tt = _pick(t, (256, 128, 64, 32))
    halo_blocks = tt // CONV_HALO
    return pl.pallas_call(
        _conv_back_body,
        grid=(b, t // tt),
        in_specs=[pl.BlockSpec((1, CONV_HALO, D_MODEL), lambda bi, i: (bi, 0, 0)),
                  pl.BlockSpec((1, CONV_HALO, D_MODEL), lambda bi, i: (bi, jnp.maximum(i * halo_blocks - 1, 0), 0)),
                  pl.BlockSpec((1, tt, D_MODEL), lambda bi, i: (bi, i, 0)),
                  _const_spec((CONV_HALO, D_MODEL)),
                  _const_spec((1, D_MODEL)), _const_spec((1, D_MODEL)), _const_spec((1, D_MODEL)),
                  _const_spec((D_MODEL, D_MODEL)), _const_spec((1, D_MODEL)),
                  pl.BlockSpec((1, tt, D_MODEL), lambda bi, i: (bi, i, 0)),
                  _const_spec((1, D_MODEL))],
        out_specs=pl.BlockSpec((1, tt, D_MODEL), lambda bi, i: (bi, i, 0)),
        out_shape=jax.ShapeDtypeStruct((b, t, D_MODEL), F32),
        scratch_shapes=[pltpu.VMEM((tt + CONV_HALO, D_MODEL), F32), pltpu.VMEM((tt, D_MODEL), F32)],
        compiler_params=_params(("parallel", "parallel")),
        name="conv_back",
    )(hist, u3d, u3d, wdw, bdw, lng, lnb, w2_bf, b2, x3d, g1)


_FF_CHUNK = 1024


def _ffn_body(x_ref, g2_ref, w1_ref, w2_ref, g3_ref, wg_ref, bg_ref, p_ref, wp_ref, o_ref):
    x = x_ref[...]
    h = _rms(x, g2_ref[...]).astype(BF16)
    f = jnp.zeros(x.shape, F32)
    for c in range(D_FF // _FF_CHUNK):
        a = jnp.maximum(_dot(h, w1_ref[:, c * _FF_CHUNK:(c + 1) * _FF_CHUNK]), 0.0)
        f = f + _dot((a * a).astype(BF16), w2_ref[c * _FF_CHUNK:(c + 1) * _FF_CHUNK, :])
    x2 = x + _rms(f, g3_ref[...])
    gate = _sigmoid(_dot(x2.astype(BF16), wg_ref[...]) + bg_ref[...])
    o_ref[...] = x2 + gate * _dot(p_ref[...].astype(BF16), wp_ref[...])


def _ffn(x2d, g2, w1_bf, w2_bf, g3, wg_bf, bg, p2d, wp_bf):
    n = x2d.shape[0]
    d_ple = p2d.shape[1]
    tm = _pick(n, (512, 256, 128, 64, 32, 16, 8))
    return pl.pallas_call(
        _ffn_body,
        grid=(n // tm,),
        in_specs=[pl.BlockSpec((tm, D_MODEL), lambda i: (i, 0)),
                  _const_spec((1, D_MODEL)),
                  _const_spec((D_MODEL, D_FF)), _const_spec((D_FF, D_MODEL)),
                  _const_spec((1, D_MODEL)),
                  _const_spec((D_MODEL, D_MODEL)), _const_spec((1, D_MODEL)),
                  pl.BlockSpec((tm, d_ple), lambda i: (i, 0)),
                  _const_spec((d_ple, D_MODEL))],
        out_specs=pl.BlockSpec((tm, D_MODEL), lambda i: (i, 0)),
        out_shape=jax.ShapeDtypeStruct((n, D_MODEL), F32),
        compiler_params=_params(("parallel",)),
        name="ffn_ple",
    )(x2d, g2, w1_bf, w2_bf, g3, wg_bf, bg, p2d, wp_bf)


_PROJ_W = Q_DIM + 6 * KV_DIM + LANES
LOG2E = 1.4426950408889634
V_ROWS = HEAD_DIM + 16
_VT_ROWS = 2 * N_KV * V_ROWS


def _rope_block(blk, c, su, sd):
    return blk * c + pltpu.roll(blk, ROT_DIM // 2, 1) * su + pltpu.roll(blk, LANES - ROT_DIM // 2, 1) * sd


def _nsa_proj_body(prompt, x_ref, g_ref, w_ref, c_ref, su_ref, sd_ref, *refs):
    if prompt:
        blk_ref, refs = refs[0], refs[1:]
        cache_t_refs, key_refs = refs[10:13], refs[13:15]
    else:
        cache_t_refs = key_refs = ()
    q_ref, qr_ref, cmp_ref, sel_ref, win_ref, gate_ref, qt_ref, qrt_ref, vt_ref, gatet_ref = refs[:10]
    low = lax.broadcasted_iota(jnp.int32, (x_ref.shape[0], LANES), 1) < HEAD_DIM
    h = _rms(x_ref[...], g_ref[...]).astype(BF16)
    z = _dot(h, w_ref[...])
    c, su, sd = c_ref[...], su_ref[...], sd_ref[...]
    for i in range(Q_DIM // LANES):
        cols = slice(i * LANES, (i + 1) * LANES)
        qs = z[:, cols] * SCALE
        qrs = _rope_block(z[:, cols], c, su, sd) * SCALE
        q_ref[:, cols] = qs.astype(BF16)
        qr_ref[:, cols] = qrs.astype(BF16)
        qt_ref[0, cols, :] = (qs * LOG2E).T.astype(BF16)
        qrt_ref[0, cols, :] = (qrs * LOG2E).T.astype(BF16)
    o = Q_DIM
    cmp_ref[...] = z[:, o:o + 2 * KV_DIM]
    if prompt:
        for i in range(2 * KV_DIM // LANES):
            cache_t_refs[0][0, i * LANES:(i + 1) * LANES, :] = z[:, o + i * LANES:o + (i + 1) * LANES].T
    o += 2 * KV_DIM
    ones = jnp.ones((V_ROWS - HEAD_DIM, x_ref.shape[0]), BF16)
    for n, dst in enumerate((sel_ref, win_ref)):
        for i in range(KV_DIM // LANES):
            k_rot = _rope_block(z[:, o + i * LANES:o + (i + 1) * LANES], c, su, sd)
            dst[:, i * LANES:(i + 1) * LANES] = k_rot
            vt = z[:, o + KV_DIM + i * LANES:o + KV_DIM + (i + 1) * LANES].T
            if prompt:
                cache_t_refs[1 + n][0, i * LANES:(i + 1) * LANES, :] = k_rot.T
                cache_t_refs[1 + n][0, KV_DIM + i * LANES:KV_DIM + (i + 1) * LANES, :] = vt
                fill = blk_ref[...] if n == 0 else 0.0
                per_tile = LANES // HEAD_DIM
                key_refs[n][per_tile * i] = jnp.where(low, k_rot, fill).astype(BF16)
                key_refs[n][per_tile * i + 1] = jnp.where(low, pltpu.roll(k_rot, HEAD_DIM, 1), fill).astype(BF16)
            for k in range(LANES // HEAD_DIM):
                base = (n * N_KV + i * (LANES // HEAD_DIM) + k) * V_ROWS
                vt_ref[0, base:base + HEAD_DIM, :] = vt[k * HEAD_DIM:(k + 1) * HEAD_DIM].astype(BF16)
                vt_ref[0, base + HEAD_DIM:base + V_ROWS, :] = ones
        dst[:, KV_DIM:] = z[:, o + KV_DIM:o + 2 * KV_DIM]
        o += 2 * KV_DIM
    gates = _sigmoid(z[:, o:o + LANES])
    gate_ref[...] = gates
    gatet_ref[0] = gates.T


def _rope_tables(pos, rows):
    half = ROT_DIM // 2
    inv = jnp.float32(ROPE_THETA) ** (-jnp.arange(half, dtype=F32) * (2.0 / ROT_DIM))
    ang = pos.astype(F32)[:, None] * inv[None, :]
    lane = np.arange(LANES)
    within = lane % HEAD_DIM
    cos = jnp.cos(ang)[:, lane % half]
    sin = jnp.sin(ang)[:, lane % half]
    c = jnp.where(within[None, :] < ROT_DIM, cos, 1.0)
    su = jnp.where((within[None, :] >= half) & (within[None, :] < ROT_DIM), sin, 0.0)
    sd = jnp.where(within[None, :] < half, -sin, 0.0)
    reps = rows // pos.shape[0]
    return tuple(jnp.tile(a, (reps, 1)) for a in (c, su, sd))


def _nsa_proj(x2d, g, w_bf, pos):
    n = x2d.shape[0]
    t = pos.shape[0]
    tm = _pick(n, (256, 128, 64, 32, 16, 8))
    if t >= tm:
        assert t % tm == 0
        tabs = _rope_tables(pos, t)
        nt = t // tm
        tab_spec = pl.BlockSpec((tm, LANES), lambda i: (i % nt, 0))
    else:
        assert tm % t == 0
        tabs = _rope_tables(pos, tm)
        tab_spec = _const_spec((tm, LANES))
    row = lambda w: pl.BlockSpec((tm, w), lambda i: (i, 0))
    tile = lambda rows: pl.BlockSpec((1, rows, tm), lambda i: (i, 0, 0))
    tiles = lambda rows, dt: jax.ShapeDtypeStruct((n // tm, rows, tm), dt)
    out_specs = [row(Q_DIM), row(Q_DIM), row(2 * KV_DIM), row(2 * KV_DIM), row(2 * KV_DIM), row(LANES),
                 tile(Q_DIM), tile(Q_DIM), tile(_VT_ROWS), tile(LANES)]
    out_shape = [jax.ShapeDtypeStruct((n, Q_DIM), BF16), jax.ShapeDtypeStruct((n, Q_DIM), BF16),
                 jax.ShapeDtypeStruct((n, 2 * KV_DIM), F32), jax.ShapeDtypeStruct((n, 2 * KV_DIM), F32),
                 jax.ShapeDtypeStruct((n, 2 * KV_DIM), F32), jax.ShapeDtypeStruct((n, LANES), F32),
                 tiles(Q_DIM, BF16), tiles(Q_DIM, BF16), tiles(_VT_ROWS, BF16), tiles(LANES, F32)]
    in_specs = [row(D_MODEL), _const_spec((1, D_MODEL)), _const_spec((D_MODEL, _PROJ_W)),
                tab_spec, tab_spec, tab_spec]
    args = [x2d, g, w_bf, *tabs]
    prompt = t >= tm
    if prompt:
        assert t // L_SEL <= HEAD_DIM
        blk = np.zeros((t, LANES), np.float32)
        blk[np.arange(t), HEAD_DIM + np.arange(t) // L_SEL] = 1.0
        in_specs.append(tab_spec)
        args.append(jnp.asarray(blk))
        out_specs += [pl.BlockSpec((1, 2 * KV_DIM, tm), lambda i: (i // nt, 0, i % nt))] * 3
        out_shape += [jax.ShapeDtypeStruct((n // t, 2 * KV_DIM, t), F32)] * 3
        out_specs += [pl.BlockSpec((N_KV, tm, LANES), lambda i: (0, i, 0))] * 2
        out_shape += [jax.ShapeDtypeStruct((N_KV, n, LANES), BF16)] * 2
    return pl.pallas_call(
        functools.partial(_nsa_proj_body, prompt),
        grid=(n // tm,),
        in_specs=in_specs,
        out_specs=out_specs,
        out_shape=out_shape,
        compiler_params=_params(("parallel",)),
        name="nsa_proj",
    )(*args)


_CHUNK_W = STRIDE * 2 * KV_DIM
_AB_W = 2 * N_KV * CMP_HID


_S_PER_DOT = 4


def _first_layer(get_tile, w_ref, rows):
    half = LANES // 2
    low = lax.broadcasted_iota(jnp.int32, (rows, LANES), 1) < half
    out = []
    for j in range(KV_DIM // LANES):
        acc = [jnp.zeros((rows, 2 * CMP_HID), F32) for _ in range(2)]
        for sq in range(STRIDE // _S_PER_DOT):
            even, odd = [], []
            for s in range(sq * _S_PER_DOT, (sq + 1) * _S_PER_DOT, 2):
                a, b = get_tile(s, j), get_tile(s + 1, j)
                even.append(jnp.where(low, a, pltpu.roll(b, half, 1)))
                odd.append(jnp.where(low, pltpu.roll(a, half, 1), b))
            acc[0] = acc[0] + _dot(jnp.concatenate(even, axis=1).astype(BF16), w_ref[sq])
            acc[1] = acc[1] + _dot(jnp.concatenate(odd, axis=1).astype(BF16), w_ref[sq])
        out += acc
    return out


def _store_ab(ab_ref, rows_slice, accs):
    for g, acc in enumerate(accs):
        ab_ref[rows_slice, g * CMP_HID:(g + 1) * CMP_HID] = acc[:, :CMP_HID]
        ab_ref[rows_slice, (N_KV + g) * CMP_HID:(N_KV + g + 1) * CMP_HID] = acc[:, CMP_HID:]


def _chunk_rows_tile(src_ref, kv):
    def get(s, j):
        lo = s * 2 * KV_DIM + kv * KV_DIM + j * LANES
        return src_ref[0, :, lo:lo + LANES]
    return get


def _compress_finish(xl_ref, per_kv, n_rows):
    half = _AB_W // 2
    for kv, (ab_ref, w_ref, pe_ref, w1_ref, w2_ref, out_ref) in enumerate(per_kv):
        _store_ab(ab_ref, slice(n_rows, n_rows + 8), _first_layer(_chunk_rows_tile(xl_ref, kv), w_ref, 8))
        c = jnp.sum(pe_ref[...] * w1_ref[...], axis=0, keepdims=True)
        c = jnp.concatenate([c] * N_KV, axis=1)
        pre = ab_ref[0:n_rows, 0:half] + ab_ref[1:n_rows + 1, half:] + c
        hid = (pre * _sigmoid(pre)).astype(BF16)
        out_ref[0] = _dot(hid, w2_ref[...])


def _compress_body(x_ref, xl_ref, wk_ref, wv_ref, pek_ref, pev_ref, w1k_ref, w1v_ref, w2k_ref, w2v_ref,
                   kcc_ref, vcc_ref, abk_ref, abv_ref):
    rt = x_ref.shape[1]
    i = pl.program_id(1)
    n_rt = pl.num_programs(1)
    rows = pl.ds(pl.multiple_of(i * rt, rt), rt)
    _store_ab(abk_ref, rows, _first_layer(_chunk_rows_tile(x_ref, 0), wk_ref, rt))
    _store_ab(abv_ref, rows, _first_layer(_chunk_rows_tile(x_ref, 1), wv_ref, rt))

    @pl.when(i == n_rt - 1)
    def _():
        _compress_finish(xl_ref, ((abk_ref, wk_ref, pek_ref, w1k_ref, w2k_ref, kcc_ref),
                                  (abv_ref, wv_ref, pev_ref, w1v_ref, w2v_ref, vcc_ref)), kcc_ref.shape[1])


def _compress(x_chunks, x_last, wk, wv, pek, pev, w1k, w1v, w2k, w2v):
    b, n, _ = x_chunks.shape
    rt = _pick(n, (128, 64, 32, 16, 8)) if n % 8 == 0 else n
    flat = L_CMP * HEAD_DIM
    out = jax.ShapeDtypeStruct((b, n, KV_DIM), F32)
    return pl.pallas_call(
        _compress_body,
        grid=(b, n // rt),
        in_specs=[pl.BlockSpec((1, rt, _CHUNK_W), lambda bi, i: (bi, i, 0)),
                  pl.BlockSpec((1, 8, _CHUNK_W), lambda bi, i: (bi, 0, 0)),
                  _const_spec(_W1_SHAPE), _const_spec(_W1_SHAPE),
                  _const_spec((flat, 1)), _const_spec((flat, 1)),
                  _const_spec((flat, CMP_HID)), _const_spec((flat, CMP_HID)),
                  _const_spec((N_KV * CMP_HID, KV_DIM)), _const_spec((N_KV * CMP_HID, KV_DIM))],
        out_specs=[pl.BlockSpec((1, n, KV_DIM), lambda bi, i: (bi, 0, 0))] * 2,
        out_shape=[out, out],
        scratch_shapes=[pltpu.VMEM((n + 8, _AB_W), F32), pltpu.VMEM((n + 8, _AB_W), F32)],
        compiler_params=_params(("parallel", "arbitrary")),
        name="compress",
    )(x_chunks, x_last, wk, wv, pek, pev, w1k, w1v, w2k, w2v)


_W1_SHAPE = (STRIDE // _S_PER_DOT, _S_PER_DOT * HEAD_DIM, 2 * CMP_HID)


def _compress_weights(cmp_pe, cmp_w1, cmp_w2):
    eye = jnp.eye(N_KV, dtype=F32)
    outs = []
    for kv in range(2):
        w1 = cmp_w1[kv].reshape(2, STRIDE // _S_PER_DOT, _S_PER_DOT, HEAD_DIM, CMP_HID)
        stacked = w1.transpose(1, 2, 3, 0, 4).reshape(_W1_SHAPE)
        w2 = jnp.einsum('gk,hd->ghkd', eye, cmp_w2[kv]).reshape(N_KV * CMP_HID, KV_DIM)
        outs.append((stacked.astype(BF16), cmp_pe[kv].reshape(L_CMP * HEAD_DIM, 1),
                     cmp_w1[kv].reshape(L_CMP * HEAD_DIM, CMP_HID), w2.astype(BF16)))
    return outs


def _select_blocks(imp_t, blk, qpos, n_real, k):
    cur = jnp.right_shift(qpos, int(math.log2(L_SEL)))
    forced = (blk == 0) | (blk == cur) | (blk == cur - 1)
    future = blk * L_SEL > qpos
    score = jnp.where(future, -jnp.inf, jnp.where(forced, FORCE, imp_t))
    score = jnp.where(blk < n_real, score, -jnp.inf)
    sub = 8
    rows = [score[i:i + sub] for i in range(0, score.shape[0], sub)]
    local = lax.broadcasted_iota(jnp.int32, rows[0].shape, 0)
    rank = [jnp.zeros(r.shape, jnp.int32) for r in rows]
    for j in range(n_real):
        sj = score[j:j + 1, :]
        for n, r in enumerate(rows):
            if n * sub > j:
                beats = sj >= r
            elif n * sub + sub - 1 <= j:
                beats = sj > r
            else:
                beats = jnp.where(local > j - n * sub, jnp.where(sj >= r, 1, 0), jnp.where(sj > r, 1, 0)) > 0
            rank[n] = rank[n] + jnp.where(beats, 1, 0)
    return (jnp.concatenate(rank, axis=0) < k) & (blk < n_real)


def _overlap_t(n_sel_pad, n_cmp_pad, n_sel, n_cmp):
    ci = np.arange(n_cmp_pad)[None, :] * STRIDE
    sj = np.arange(n_sel_pad)[:, None] * L_SEL
    ov = (ci < sj + L_SEL) & (ci + L_CMP > sj)
    ov &= (np.arange(n_cmp_pad)[None, :] < n_cmp) & (np.arange(n_sel_pad)[:, None] < n_sel)
    return jnp.asarray(ov, dtype=BF16)


def _head_rows(h):
    return slice(h * HEAD_DIM, (h + 1) * HEAD_DIM)


def _cmp_topk_body(n_sel, qt_ref, kcc_ref, vcct_ref, gt_ref, ovt_ref, ocg_ref, bias_ref):
    tq = qt_ref.shape[2]
    nc = kcc_ref.shape[2]
    nsp = ovt_ref.shape[0]
    t0 = pl.program_id(1) * tq
    wide = N_REP * tq
    qpos = t0 + (lax.broadcasted_iota(jnp.int32, (1, wide), 1) & (tq - 1))
    cmp_end = lax.broadcasted_iota(jnp.int32, (nc, 1), 0) * STRIDE + (L_CMP - 1)
    cmask = cmp_end <= qpos
    blk = lax.broadcasted_iota(jnp.int32, (nsp, tq), 0)
    qpos_t = t0 + lax.broadcasted_iota(jnp.int32, (nsp, tq), 1)
    ovt = ovt_ref[...]
    for g in range(N_KV):
        heads = range(g * N_REP, (g + 1) * N_REP)
        q = jnp.concatenate([qt_ref[0, _head_rows(h), :] for h in heads], axis=1)
        lm = jnp.where(cmask, _dot(kcc_ref[0, g], q), NEG)
        e = jnp.exp2(lm - jnp.max(lm, axis=0, keepdims=True))
        pc = jnp.where(cmask, e * (1.0 / jnp.sum(e, axis=0, keepdims=True)), 0.0)
        o = _dot(vcct_ref[0, _head_rows(g), :], pc.astype(BF16))
        pg = jnp.zeros((nc, tq), F32)
        for r, h in enumerate(heads):
            ocg_ref[0, _head_rows(h), :] = o[:, r * tq:(r + 1) * tq] * gt_ref[0, 3 * h:3 * h + 1, :]
            pg = pg + pc[:, r * tq:(r + 1) * tq]
        hi, lo = _split_bf16(pg)
        imp_t = _dot(ovt, hi) + _dot(ovt, lo)
        sel = _select_blocks(imp_t, blk, qpos_t, n_sel, min(N_TOP, n_sel))
        bias_ref[0, g] = jnp.where(sel, 0.0, NEG).astype(BF16)


def _cmp_topk(q_t, kcc_gm, vcc_t, gates_t, b, n_cmp, n_sel, nsp):
    tq = q_t.shape[2]
    n_t = q_t.shape[0] // b
    nc = kcc_gm.shape[2]
    ovt = _overlap_t(nsp, nc, n_sel, n_cmp)
    tile = lambda rows: pl.BlockSpec((1, rows, tq), lambda bi, i: (bi * n_t + i, 0, 0))
    return pl.pallas_call(
        functools.partial(_cmp_topk_body, n_sel),
        grid=(b, n_t),
        in_specs=[tile(Q_DIM),
                  pl.BlockSpec((1, N_KV, nc, HEAD_DIM), lambda bi, i: (bi, 0, 0, 0)),
                  pl.BlockSpec((1, KV_DIM, nc), lambda bi, i: (bi, 0, 0)),
                  tile(LANES), _const_spec((nsp, nc))],
        out_specs=[tile(Q_DIM), pl.BlockSpec((1, N_KV, nsp, tq), lambda bi, i: (bi, 0, 0, i))],
        out_shape=[jax.ShapeDtypeStruct((b * n_t, Q_DIM, tq), F32),
                   jax.ShapeDtypeStruct((b, N_KV, nsp, n_t * tq), BF16)],
        compiler_params=_params(("parallel", "parallel")),
        name="cmp_topk",
    )(q_t, kcc_gm, vcc_t, gates_t, ovt)


def _flash_init(m_ref, l_ref, acc_ref):
    m_ref[...] = jnp.full(m_ref.shape, NEG, F32)
    l_ref[...] = jnp.zeros(l_ref.shape, F32)
    acc_ref[...] = jnp.zeros(acc_ref.shape, F32)


def _flash_t(state, s, vt):
    m, acc = state
    m_new = jnp.maximum(m, jnp.max(s, axis=0, keepdims=True))
    return m_new, jnp.exp2(m - m_new) * acc + _dot(vt, jnp.exp2(s - m_new).astype(BF16))


def _v_rows(branch, g):
    base = (branch * N_KV + g) * V_ROWS
    return slice(base, base + V_ROWS)


def _sel_win_body(qrt_ref, bias_ref, kp_ref, kw_ref, vt_ref, ocg_ref, gt_ref, w_ref, x_ref, g1_ref,
                  o_ref, ot_ref):
    tq = qrt_ref.shape[2]
    kc = tq
    qt = pl.program_id(1)
    wide = N_REP * tq
    krow = lax.broadcasted_iota(jnp.int32, (kc, wide), 0)
    qcol = lax.broadcasted_iota(jnp.int32, (kc, wide), 1) & (tq - 1)
    n_back = WINDOW // kc
    fresh = (jnp.full((1, wide), NEG, F32), jnp.zeros((V_ROWS, wide), F32))

    for g in range(N_KV):
        heads = range(g * N_REP, (g + 1) * N_REP)
        qw = jnp.concatenate([qrt_ref[0, _head_rows(h), :] for h in heads], axis=1)
        qs = jnp.concatenate([qw, jnp.concatenate([bias_ref[0, g]] * N_REP, axis=1)], axis=0)
        qw = jnp.concatenate([qw, jnp.zeros_like(qw)], axis=0)

        def scores(j):
            return _dot(kp_ref[g, pl.ds(pl.multiple_of(j * kc, kc), kc), :], qs)

        def sel_chunk(j, state, diagonal):
            s = scores(j)
            return _flash_t(state, jnp.where(krow <= qcol, s, NEG) if diagonal else s, vt_ref[j, _v_rows(0, g), :])

        def sel_pair(i, state):
            s_a, s_b = scores(2 * i), scores(2 * i + 1)
            state = _flash_t(state, s_a, vt_ref[2 * i, _v_rows(0, g), :])
            return _flash_t(state, s_b, vt_ref[2 * i + 1, _v_rows(0, g), :])

        state = lax.fori_loop(0, qt // 2, sel_pair, fresh)
        state = lax.cond(qt % 2 == 1, lambda st: sel_chunk(qt - 1, st, False), lambda st: st, state)

        def win_scores(back):
            s = _dot(kw_ref[g, pl.ds(pl.multiple_of((qt - back) * kc, kc), kc), :], qw)
            if back == 0:
                return jnp.where(krow <= qcol, s, NEG)
            return jnp.where(krow > qcol, s, NEG) if back == n_back else s

        def win_seq(backs, state):
            for back, s in [(back, win_scores(back)) for back in backs]:
                state = _flash_t(state, s, vt_ref[qt - back, _v_rows(1, g), :])
            return state

        s_sel, s_win = scores(qt), win_scores(0)
        _, acc_sel = _flash_t(state, jnp.where(krow <= qcol, s_sel, NEG), vt_ref[qt, _v_rows(0, g), :])
        state = _flash_t(fresh, s_win, vt_ref[qt, _v_rows(1, g), :])
        older = lambda st: st
        for k in range(1, n_back + 1):
            older = functools.partial(lambda k, fewer, st: lax.cond(
                qt >= k, functools.partial(win_seq, range(1, k + 1)), fewer, st), k, older)
        _, acc_win = older(state)

        den = slice(HEAD_DIM, HEAD_DIM + 1)
        for r, h in enumerate(heads):
            cols = slice(r * tq, (r + 1) * tq)
            ot_ref[_head_rows(h), :] = (
                ocg_ref[0, _head_rows(h), :]
                + acc_sel[:HEAD_DIM, cols] * (gt_ref[0, 3 * h + 1:3 * h + 2, :] * (1.0 / acc_sel[den, cols]))
                + acc_win[:HEAD_DIM, cols] * (gt_ref[0, 3 * h + 2:3 * h + 3, :] * (1.0 / acc_win[den, cols])))

    m = _dot(ot_ref[...].T.astype(BF16), w_ref[...])
    o_ref[...] = x_ref[...] + _rms(m, g1_ref[...])


def _sel_win_out(qr_t, bias_t, k_sel, k_win, v_t, ocg_t, gates_t, w_out_bf, x2d, g1, b):
    tq = qr_t.shape[2]
    n_t = qr_t.shape[0] // b
    t = n_t * tq
    nsp = bias_t.shape[2]
    assert WINDOW % tq == 0 and HEAD_DIM + nsp == LANES
    tile = lambda rows: pl.BlockSpec((1, rows, tq), lambda bi, i: (bi * n_t + i, 0, 0))
    keys = pl.BlockSpec((N_KV, t, LANES), lambda bi, i: (0, bi, 0))
    xrow = pl.BlockSpec((tq, D_MODEL), lambda bi, i: (bi * n_t + i, 0))
    return pl.pallas_call(
        _sel_win_body,
        grid=(b, n_t),
        in_specs=[tile(Q_DIM),
                  pl.BlockSpec((1, N_KV, nsp, tq), lambda bi, i: (bi, 0, 0, i)),
                  keys, keys,
                  pl.BlockSpec((n_t, _VT_ROWS, tq), lambda bi, i: (bi, 0, 0)),
                  tile(Q_DIM), tile(LANES),
                  _const_spec((Q_DIM, D_MODEL)), xrow, _const_spec((1, D_MODEL))],
        out_specs=xrow,
        out_shape=jax.ShapeDtypeStruct((b * t, D_MODEL), F32),
        scratch_shapes=[pltpu.VMEM((Q_DIM, tq), F32)],
        compiler_params=_params(("parallel", "parallel")),
        name="sel_win_out",
    )(qr_t, bias_t, k_sel, k_win, v_t, ocg_t, gates_t, w_out_bf, x2d, g1)


def _nsa_out_body(oc_ref, os_ref, ow_ref, gate_ref, e_ref, w_ref, x_ref, g1_ref, o_ref):
    hi, lo = _split_bf16(gate_ref[...])
    o = jnp.zeros(oc_ref.shape, F32)
    for c, src in enumerate((oc_ref, os_ref, ow_ref)):
        o = o + (_dot(hi, e_ref[c]) + _dot(lo, e_ref[c])) * src[...]
    m = _dot(o.astype(BF16), w_ref[...])
    o_ref[...] = x_ref[...] + _rms(m, g1_ref[...])


def _gate_expand():
    e = np.zeros((3, LANES, Q_DIM), np.float32)
    for h in range(N_HEADS):
        for c in range(3):
            e[c, h * 3 + c, h * HEAD_DIM:(h + 1) * HEAD_DIM] = 1.0
    return jnp.asarray(e, dtype=BF16)


def _nsa_out(o_cmp, o_sel, o_win, gates, w_bf, x2d, g1):
    n = x2d.shape[0]
    tm = _pick(n, (256, 128, 64, 32, 16, 8))
    row = lambda w: pl.BlockSpec((tm, w), lambda i: (i, 0))
    return pl.pallas_call(
        _nsa_out_body,
        grid=(n // tm,),
        in_specs=[row(Q_DIM), row(Q_DIM), row(Q_DIM), row(LANES), _const_spec((3, LANES, Q_DIM)),
                  _const_spec((Q_DIM, D_MODEL)), row(D_MODEL), _const_spec((1, D_MODEL))],
        out_specs=row(D_MODEL),
        out_shape=jax.ShapeDtypeStruct((n, D_MODEL), F32),
        compiler_params=_params(("parallel",)),
        name="nsa_out",
    )(o_cmp, o_sel, o_win, gates, _gate_expand(), w_bf, x2d, g1)


class _PageRing:
    def __init__(self, pt_ref, cache_hbm, buf_ref, sem_ref, n_b, n_c):
        self.pt, self.cache, self.buf, self.sem = pt_ref, cache_hbm, buf_ref, sem_ref
        self.n_b, self.n_c, self.pg = n_b, n_c, buf_ref.shape[1]

    def _copies(self, step):
        b, c, slot = step // self.n_c, step % self.n_c, step % 2
        return [pltpu.make_async_copy(self.cache.at[self.pt[b, c * self.pg + p]], self.buf.at[slot, p],
                                      self.sem.at[slot]) for p in range(self.pg)]

    def acquire(self, b, c):
        step = b * self.n_c + c

        @pl.when(step == 0)
        def _():
            for cp in self._copies(step):
                cp.start()

        @pl.when(step + 1 < self.n_b * self.n_c)
        def _():
            for cp in self._copies(step + 1):
                cp.start()

        for cp in self._copies(step):
            cp.wait()
        return step % 2


def _compress_paged_body(n_b, n_c, pt_ref, cache_hbm, perm_ref, xl_ref, wk_ref, wv_ref, pek_ref, pev_ref,
                         w1k_ref, w1v_ref, w2k_ref, w2v_ref, kcc_ref, vcc_ref, buf_ref, sem_ref, xs_ref,
                         abk_ref, abv_ref):
    pg, page = buf_ref.shape[1], buf_ref.shape[4]
    per_page = page // STRIDE
    rows = pg * per_page
    b, c = pl.program_id(0), pl.program_id(1)
    slot = _PageRing(pt_ref, cache_hbm, buf_ref, sem_ref, n_b, n_c).acquire(b, c)
    r0 = pl.multiple_of(c * rows, rows)
    for kv, (ab_ref, w_ref) in enumerate(((abk_ref, wk_ref), (abv_ref, wv_ref))):
        for p in range(pg):
            t = _dot_nt(perm_ref[...], buf_ref[slot, p, kv].astype(BF16))
            for s in range(STRIDE):
                xs_ref[s, p * per_page:(p + 1) * per_page, :] = t[s * per_page:(s + 1) * per_page]
        tile = lambda s, j: xs_ref[s, :, j * LANES:(j + 1) * LANES]
        _store_ab(ab_ref, pl.ds(r0, rows), _first_layer(tile, w_ref, rows))

    @pl.when(c == n_c - 1)
    def _():
        _compress_finish(xl_ref, ((abk_ref, wk_ref, pek_ref, w1k_ref, w2k_ref, kcc_ref),
                                  (abv_ref, wv_ref, pev_ref, w1v_ref, w2v_ref, vcc_ref)), kcc_ref.shape[1])


def _compress_paged(page_table, cache_fm, x_last, wk, wv, pek, pev, w1k, w1v, w2k, w2v):
    b, n_pages = page_table.shape
    page = cache_fm.shape[3]
    pg = _pick(n_pages, (16, 8, 4, 2, 1))
    n_c = n_pages // pg
    n = n_pages * page // STRIDE
    flat = L_CMP * HEAD_DIM
    out = jax.ShapeDtypeStruct((b, n, KV_DIM), F32)
    const = _const_spec
    per_page = page // STRIDE
    assert per_page % 8 == 0
    tok = np.arange(page)
    perm = np.zeros((page, page), np.float32)
    perm[(tok % STRIDE) * per_page + tok // STRIDE, tok] = 1.0
    return pl.pallas_call(
        functools.partial(_compress_paged_body, b, n_c),
        grid_spec=pltpu.PrefetchScalarGridSpec(
            num_scalar_prefetch=1,
            grid=(b, n_c),
            in_specs=[pl.BlockSpec(memory_space=pl.ANY),
                      const((page, page)),
                      pl.BlockSpec((1, 8, _CHUNK_W), lambda bi, c, pt: (bi, 0, 0)),
                      const(_W1_SHAPE), const(_W1_SHAPE),
                      const((flat, 1)), const((flat, 1)), const((flat, CMP_HID)), const((flat, CMP_HID)),
                      const((N_KV * CMP_HID, KV_DIM)), const((N_KV * CMP_HID, KV_DIM))],
            out_specs=[pl.BlockSpec((1, n, KV_DIM), lambda bi, c, pt: (bi, 0, 0))] * 2,
            scratch_shapes=[pltpu.VMEM((2, pg, 2, KV_DIM, page), F32), pltpu.SemaphoreType.DMA((2,)),
                            pltpu.VMEM((STRIDE, pg * per_page, KV_DIM), F32),
                            pltpu.VMEM((n + 8, _AB_W), F32), pltpu.VMEM((n + 8, _AB_W), F32)]),
        out_shape=[out, out],
        compiler_params=_params(("arbitrary", "arbitrary")),
        name="compress_paged",
    )(page_table, cache_fm, jnp.asarray(perm, dtype=BF16), x_last, wk, wv, pek, pev, w1k, w1v, w2k, w2v)


def _softmax_rows(s):
    e = jnp.exp(s - jnp.max(s, axis=-1, keepdims=True))
    return e / jnp.sum(e, axis=-1, keepdims=True)


def _sample_cmp_body(n_sel, past_len, t_new, q_ref, kcc_ref, vcc_ref, ovt_ref, rep_ref, o_ref, sel_ref):
    rows = q_ref.shape[1]
    nc = kcc_ref.shape[1]
    nsp = ovt_ref.shape[0]
    gq = N_KV * t_new
    tq_col = lax.broadcasted_iota(jnp.int32, (rows, 1), 0) & (t_new - 1)
    cmp_end = lax.broadcasted_iota(jnp.int32, (1, nc), 1) * STRIDE + (L_CMP - 1)
    cmask = cmp_end <= past_len + tq_col
    lm = jnp.where(cmask, _dot_nt(q_ref[0], kcc_ref[0].astype(BF16)), NEG)
    pc = jnp.where(cmask, _softmax_rows(lm), 0.0)
    o_ref[0] = _dot(pc.astype(BF16), vcc_ref[0].astype(BF16))
    pg = pc.reshape(N_KV, N_REP, t_new, nc).sum(axis=1).reshape(gq, nc)
    hi, lo = _split_bf16(pg)
    imp_t = _dot_nt(ovt_ref[...], hi) + _dot_nt(ovt_ref[...], lo)
    blk = lax.broadcasted_iota(jnp.int32, (nsp, gq), 0)
    qpos = past_len + (lax.broadcasted_iota(jnp.int32, (nsp, gq), 1) & (t_new - 1))
    sel = _select_blocks(imp_t, blk, qpos, n_sel, min(N_TOP, n_sel))
    sel_ref[0] = _dot_nt(rep_ref[...], jnp.where(sel, 1.0, 0.0).astype(BF16)).astype(BF16)


def _sample_cmp(q_bd, kcc, vcc, n_cmp, n_sel, nsp, past_len, t_new):
    b, rows, _ = q_bd.shape
    nc = kcc.shape[1]
    gq = N_KV * t_new
    ovt = _overlap_t(nsp, nc, n_sel, n_cmp)
    rep = np.zeros((rows, gq), np.float32)
    for h in range(N_HEADS):
        for t in range(t_new):
            rep[h * t_new + t, (h // N_REP) * t_new + t] = 1.0
    return pl.pallas_call(
        functools.partial(_sample_cmp_body, n_sel, past_len, t_new),
        grid=(b,),
        in_specs=[pl.BlockSpec((1, rows, KV_DIM), lambda bi: (bi, 0, 0)),
                  pl.BlockSpec((1, nc, KV_DIM), lambda bi: (bi, 0, 0)),
                  pl.BlockSpec((1, nc, KV_DIM), lambda bi: (bi, 0, 0)),
                  _const_spec((nsp, nc)), _const_spec((rows, gq))],
        out_specs=[pl.BlockSpec((1, rows, KV_DIM), lambda bi: (bi, 0, 0)),
                   pl.BlockSpec((1, rows, nsp), lambda bi: (bi, 0, 0))],
        out_shape=[jax.ShapeDtypeStruct((b, rows, KV_DIM), F32), jax.ShapeDtypeStruct((b, rows, nsp), BF16)],
        compiler_params=_params(("parallel",)),
        name="sample_cmp",
    )(q_bd, kcc, vcc, ovt, jnp.asarray(rep, dtype=BF16))


def _flash_update(s, pv_fn, m_ref, l_ref, acc_ref):
    m_prev = m_ref[...]
    m_new = jnp.maximum(m_prev, jnp.max(s, axis=-1, keepdims=True))
    alpha = jnp.exp(m_prev - m_new)
    p = jnp.exp(s - m_new)
    l_ref[...] = alpha * l_ref[...] + jnp.sum(p, axis=-1, keepdims=True)
    acc_ref[...] = alpha * acc_ref[...] + pv_fn(p.astype(BF16))
    m_ref[...] = m_new


def _sample_sel_body(t_new, n_b, n_c, pt_ref, cache_hbm, q_ref, sel_ref, e_ref, et_ref, tail_ref, o_ref,
                     buf_ref, sem_ref, m_ref, l_ref, acc_ref):
    rows = q_ref.shape[1]
    pg, page = buf_ref.shape[1], buf_ref.shape[4]
    b, c = pl.program_id(0), pl.program_id(1)
    q = q_ref[0]

    @pl.when(c == 0)
    def _():
        _flash_init(m_ref, l_ref, acc_ref)

    @pl.when(c < n_c)
    def _():
        slot = _PageRing(pt_ref, cache_hbm, buf_ref, sem_ref, n_b, n_c).acquire(b, c)
        s = jnp.concatenate([_dot(q, buf_ref[slot, p, 0].astype(BF16)) for p in range(pg)], axis=1)
        s = jnp.where(_dot(sel_ref[0], e_ref[...]) > 0.5, s, NEG)

        def pv(p_bf):
            return sum(_dot_nt(p_bf[:, p * page:(p + 1) * page], buf_ref[slot, p, 1].astype(BF16))
                       for p in range(pg))

        _flash_update(s, pv, m_ref, l_ref, acc_ref)

    @pl.when(c == n_c)
    def _():
        kv = tail_ref[0]
        nk = kv.shape[0]
        s = _dot_nt(q, kv[:, :KV_DIM].astype(BF16))
        tq = lax.broadcasted_iota(jnp.int32, (rows, 1), 0) & (t_new - 1)
        ok = (_dot(sel_ref[0], et_ref[...]) > 0.5) & (lax.broadcasted_iota(jnp.int32, (1, nk), 1) <= tq)
        _flash_update(jnp.where(ok, s, NEG), lambda p_bf: _dot(p_bf, kv[:, KV_DIM:].astype(BF16)),
                      m_ref, l_ref, acc_ref)
        o_ref[0] = acc_ref[...] / l_ref[...]


def _sample_sel(page_table, cache_fm, q_bd, sel01, tail_sel, n_sel, t_new):
    b, rows, _ = q_bd.shape
    nsp = sel01.shape[2]
    n_pages = page_table.shape[1]
    page = cache_fm.shape[3]
    pg = _pick(n_pages, (8, 4, 2, 1))
    n_c = n_pages // pg
    kc = pg * page
    nk = tail_sel.shape[1]
    key_blk = np.arange(n_pages * page) // L_SEL
    e = (np.arange(nsp)[:, None] == key_blk[None, :]).astype(np.float32)
    et = np.zeros((nsp, nk), np.float32)
    et[n_sel - 1, :] = 1.0
    return pl.pallas_call(
        functools.partial(_sample_sel_body, t_new, b, n_c),
        grid_spec=pltpu.PrefetchScalarGridSpec(
            num_scalar_prefetch=1,
            grid=(b, n_c + 1),
            in_specs=[pl.BlockSpec(memory_space=pl.ANY),
                      pl.BlockSpec((1, rows, KV_DIM), lambda bi, c, pt: (bi, 0, 0)),
                      pl.BlockSpec((1, rows, nsp), lambda bi, c, pt: (bi, 0, 0)),
                      pl.BlockSpec((nsp, kc), lambda bi, c, pt: (0, jnp.minimum(c, n_c - 1))),
                      _const_spec((nsp, nk)),
                      pl.BlockSpec((1, nk, 2 * KV_DIM), lambda bi, c, pt: (bi, 0, 0))],
            out_specs=pl.BlockSpec((1, rows, KV_DIM), lambda bi, c, pt: (bi, 0, 0)),
            scratch_shapes=[pltpu.VMEM((2, pg, 2, KV_DIM, page), F32), pltpu.SemaphoreType.DMA((2,)),
                            pltpu.VMEM((rows, 1), F32), pltpu.VMEM((rows, 1), F32),
                            pltpu.VMEM((rows, KV_DIM), F32)]),
        out_shape=jax.ShapeDtypeStruct((b, rows, KV_DIM), F32),
        compiler_params=_params(("arbitrary", "arbitrary")),
        name="sample_sel",
    )(page_table, cache_fm, q_bd, sel01, jnp.asarray(e, dtype=BF16), jnp.asarray(et, dtype=BF16), tail_sel)


def _sample_win_body(t_new, q_ref, kv_ref, tail_ref, o_ref):
    rows = q_ref.shape[1]
    wb = kv_ref.shape[3]
    nk = tail_ref.shape[1]
    q = q_ref[0]
    tq = lax.broadcasted_iota(jnp.int32, (rows, 1), 0) & (t_new - 1)
    tail = tail_ref[0]
    s_old = jnp.where(lax.broadcasted_iota(jnp.int32, (1, wb), 1) > tq - WINDOW + wb,
                      _dot(q, kv_ref[0, 0].astype(BF16)), NEG)
    s_new = jnp.where(lax.broadcasted_iota(jnp.int32, (1, nk), 1) <= tq,
                      _dot_nt(q, tail[:, :KV_DIM].astype(BF16)), NEG)
    m = jnp.maximum(jnp.max(s_old, axis=-1, keepdims=True), jnp.max(s_new, axis=-1, keepdims=True))
    p_old = jnp.exp(s_old - m)
    p_new = jnp.exp(s_new - m)
    den = jnp.sum(p_old, axis=-1, keepdims=True) + jnp.sum(p_new, axis=-1, keepdims=True)
    o = _dot_nt(p_old.astype(BF16), kv_ref[0, 1].astype(BF16)) + _dot(p_new.astype(BF16), tail[:, KV_DIM:].astype(BF16))
    o_ref[0] = o / den


def _sample_win(q_bd, win_fm, tail_win, t_new):
    b, rows, _ = q_bd.shape
    wb = win_fm.shape[3]
    nk = tail_win.shape[1]
    return pl.pallas_call(
        functools.partial(_sample_win_body, t_new),
        grid=(b,),
        in_specs=[pl.BlockSpec((1, rows, KV_DIM), lambda bi: (bi, 0, 0)),
                  pl.BlockSpec((1, 2, KV_DIM, wb), lambda bi: (bi, 0, 0, 0)),
                  pl.BlockSpec((1, nk, 2 * KV_DIM), lambda bi: (bi, 0, 0))],
        out_specs=pl.BlockSpec((1, rows, KV_DIM), lambda bi: (bi, 0, 0)),
        out_shape=jax.ShapeDtypeStruct((b, rows, KV_DIM), F32),
        compiler_params=_params(("parallel",)),
        name="sample_win",
    )(q_bd, win_fm, tail_win)


def _heads_major(a2d, b, t, n):
    return a2d.reshape(b, t, n, HEAD_DIM).transpose(0, 2, 1, 3)


def _tokens_major(a_hm):
    b, n, t, d = a_hm.shape
    return a_hm.transpose(0, 2, 1, 3).reshape(b * t, n * d)


def _block_diag_q(q2d, b, t):
    q_hm = _heads_major(q2d, b, t, N_HEADS)
    onehot = jnp.asarray(np.eye(N_KV)[np.arange(N_HEADS) // N_REP], dtype=q2d.dtype)
    return jnp.einsum('bhtd,hg->bhtgd', q_hm, onehot).reshape(b, N_HEADS * t, KV_DIM)


def _own_group(o_bd, b, t):
    o = o_bd.reshape(b, N_KV, N_REP, t, N_KV, HEAD_DIM)
    o = jnp.stack([o[:, g, :, :, g, :] for g in range(N_KV)], axis=1)
    return o.transpose(0, 3, 1, 2, 4).reshape(b * t, Q_DIM)


def _nsa_prompt(x2d, b, t, g0, g1, w_in_bf, w_out_bf, cw):
    pos = jnp.arange(t)
    (_, _, cmp_rows, _, _, _, q_t, qr_t, v_t, gates_t,
     cmp_fm, sel_fm, win_fm, k_sel, k_win) = _nsa_proj(x2d, g0, w_in_bf, pos)
    assert t % STRIDE == 0 and t % L_SEL == 0
    n_ch = t // STRIDE
    n_cmp = n_ch - 1
    n_sel = t // L_SEL
    nsp = -(-n_sel // HEAD_DIM) * HEAD_DIM
    x_last = jnp.zeros((b, 8, _CHUNK_W), F32)
    (wk, pek, w1k, w2k), (wv, pev, w1v, w2v) = cw
    kcc, vcc = _compress(cmp_rows.reshape(b, n_ch, _CHUNK_W), x_last, wk, wv, pek, pev, w1k, w1v, w2k, w2v)
    gm = lambda a: a.reshape(b, -1, N_KV, HEAD_DIM).transpose(0, 2, 1, 3).astype(BF16)
    ocg_t, bias_t = _cmp_topk(q_t, gm(kcc), vcc.transpose(0, 2, 1).astype(BF16), gates_t, b, n_cmp, n_sel, nsp)
    x1 = _sel_win_out(qr_t, bias_t, k_sel, k_win, v_t, ocg_t, gates_t, w_out_bf, x2d, g1, b)
    rows5 = lambda a: a.reshape(b, 2, N_KV, HEAD_DIM, -1).transpose(0, 4, 1, 2, 3)[None]
    n_win = min(WINDOW, t)
    return x1, (rows5(cmp_fm), rows5(sel_fm), rows5(win_fm[:, :, t - n_win:]))


def _nsa_sample(x2d, b, t, g0, w_in_bf, cw, cache_cmp_l, cache_sel_l, cache_win_l, page_table):
    n_pages = page_table.shape[1]
    page = cache_cmp_l.shape[1]
    past_len = n_pages * page
    assert page % L_SEL == 0 and page % STRIDE == 0 and t <= STRIDE and t & (t - 1) == 0
    pos = past_len + jnp.arange(t)
    q, qr, cmp_rows, sel_rows, win_rows, gates = _nsa_proj(x2d, g0, w_in_bf, pos)[:6]
    row_w = 2 * KV_DIM
    fm = lambda a: a.transpose(0, 2, 3, 4, 1).reshape(a.shape[0], 2, KV_DIM, a.shape[1])
    n_past_ch = past_len // STRIDE
    n_cmp = n_past_ch
    n_sel = past_len // L_SEL + 1
    nsp = -(-n_sel // LANES) * LANES
    new3 = lambda a: a.reshape(b, t, row_w)
    x_last = jnp.pad(new3(cmp_rows), ((0, 0), (0, STRIDE - t), (0, 0))).reshape(b, 1, _CHUNK_W)
    x_last = jnp.pad(x_last, ((0, 0), (0, 7), (0, 0)))
    (wk, pek, w1k, w2k), (wv, pev, w1v, w2v) = cw
    kcc, vcc = _compress_paged(page_table, fm(cache_cmp_l), x_last, wk, wv, pek, pev, w1k, w1v, w2k, w2v)
    o_cmp_bd, sel01 = _sample_cmp(_block_diag_q(q, b, t), kcc, vcc, n_cmp, n_sel, nsp, past_len, t)
    qr_bd = _block_diag_q(qr, b, t)
    tail = lambda a: jnp.pad(new3(a), ((0, 0), (0, LANES - t), (0, 0)))
    o_sel_bd = _sample_sel(page_table, fm(cache_sel_l), qr_bd, sel01, tail(sel_rows), n_sel, t)
    win_fm = fm(cache_win_l)
    o_win_bd = _sample_win(qr_bd, win_fm, tail(win_rows), t)
    w_buf = win_fm.shape[3]
    new_fm = new3(win_rows).reshape(b, t, 2, KV_DIM).transpose(0, 2, 3, 1)
    new_win = jnp.concatenate([win_fm, new_fm], axis=3)[..., -w_buf:]
    new_win = new_win.reshape(b, 2, N_KV, HEAD_DIM, w_buf).transpose(0, 4, 1, 2, 3)
    rows5 = lambda a: a.reshape(1, b, -1, 2, N_KV, HEAD_DIM)
    caches = (rows5(cmp_rows), rows5(sel_rows), new_win[None])
    return _own_group(o_cmp_bd, b, t), _own_group(o_sel_bd, b, t), _own_group(o_win_bd, b, t), gates, caches


def kernel(x_prompt, x_sample, cache_cmp, cache_sel, cache_win, state_conv, page_table, p_prompt, p_sample,
           norm_g, w_ff1, w_ff2, w_ple, w_ple_gate, b_ple_gate,
           conv_w_pw1, conv_b_pw1, conv_w_dw, conv_b_dw, conv_ln_g, conv_ln_b, conv_w_pw2, conv_b_pw2,
           nsa_w_in, nsa_w_out, nsa_cmp_pe, nsa_cmp_w1, nsa_cmp_w2):
    depth = norm_g.shape[0]
    bf = lambda a: a.astype(BF16)
    row = lambda a: a.reshape(1, -1)
    w_ff1_bf, w_ff2_bf, w_ple_bf, w_gate_bf = bf(w_ff1), bf(w_ff2), bf(w_ple), bf(w_ple_gate)
    w_pw1_bf, w_pw2_bf, w_out_bf = bf(conv_w_pw1), bf(conv_w_pw2), bf(nsa_w_out)
    w_in_bf = bf(jnp.pad(nsa_w_in, ((0, 0), (0, 0), (0, _PROJ_W - nsa_w_in.shape[2]))))
    w_dw = jnp.pad(conv_w_dw, ((0, 0), (0, CONV_HALO - CONV_W), (0, 0)))

    def run(x, p, sample):
        b, t, _ = x.shape
        n = b * t
        x2d = x.reshape(n, D_MODEL)
        cmp_o, sel_o, win_o, conv_o = [], [], [], []
        for i in range(depth):
            g = lambda j: row(norm_g[i, j])
            if i % 2 == 0:
                c = i // 2
                u = _conv_front(x2d, g(0), w_pw1_bf[c], row(conv_b_pw1[c])).reshape(b, t, D_MODEL)
                if sample:
                    hist = jnp.pad(state_conv[c], ((0, 0), (CONV_HALO - (CONV_W - 1), 0), (0, 0)))
                else:
                    hist = jnp.zeros((b, CONV_HALO, D_MODEL), F32)
                tp = -(-t // CONV_HALO) * CONV_HALO
                pad_t = lambda a: jnp.pad(a, ((0, 0), (0, tp - t), (0, 0))) if tp > t else a
                x1 = _conv_back(hist, pad_t(u), pad_t(x2d.reshape(b, t, D_MODEL)), w_dw[c], row(conv_b_dw[c]),
                                row(conv_ln_g[c]), row(conv_ln_b[c]), w_pw2_bf[c], row(conv_b_pw2[c]),
                                g(1))[:, :t].reshape(n, D_MODEL)
                keep = CONV_W - 1
                conv_o.append(jnp.concatenate([hist, u], axis=1)[:, CONV_HALO + t - keep:][None])
            else:
                a = i // 2
                cw = _compress_weights(nsa_cmp_pe[a], nsa_cmp_w1[a], nsa_cmp_w2[a])
                if sample:
                    oc, osel, ow, gates, caches = _nsa_sample(x2d, b, t, g(0), w_in_bf[a], cw, cache_cmp[a],
                                                              cache_sel[a], cache_win[a], page_table)
                    x1 = _nsa_out(oc, osel, ow, gates, w_out_bf[a], x2d, g(1))
                else:
                    x1, caches = _nsa_prompt(x2d, b, t, g(0), g(1), w_in_bf[a], w_out_bf[a], cw)
                for dst, rows_ in zip((cmp_o, sel_o, win_o), caches):
                    dst.append(rows_)
            x2d = _ffn(x1, g(2), w_ff1_bf[i], w_ff2_bf[i], g(3), w_gate_bf[i], row(b_ple_gate[i]),
                       p[i].reshape(n, -1), w_ple_bf[i])
        cat = lambda parts: jnp.concatenate(parts, axis=0)
        return x2d.reshape(b, t, D_MODEL), cat(cmp_o), cat(sel_o), cat(win_o), cat(conv_o)

    y_p, cmp_p, sel_p, win_p, conv_p = run(x_prompt, p_prompt, False)
    y_s, cmp_s, sel_s, win_s, conv_s = run(x_sample, p_sample, True)
    return (y_p, y_s, cmp_p, cmp_s, sel_p, sel_s, win_p, win_s, conv_p, conv_s)
```

```python
import functools
import math

import numpy as np
import jax
import jax.numpy as jnp
from jax import lax
from jax.experimental import pallas as pl
from jax.experimental.pallas import tpu as pltpu

F32 = jnp.float32
BF16 = jnp.bfloat16

D_MODEL = 1024
N_HEADS = 16
N_KV = 4
N_REP = N_HEADS // N_KV
HEAD_DIM = 64
ROT_DIM = HEAD_DIM // 4
ROPE_THETA = 500000.0
L_CMP = 32
STRIDE = 16
CMP_HID = 2 * HEAD_DIM
L_SEL = 64
N_TOP = 16
WINDOW = 512
CONV_W = 31
Q_DIM = N_HEADS * HEAD_DIM
KV_DIM = N_KV * HEAD_DIM
GATE_DIM = 3 * N_HEADS
D_FF = 4 * D_MODEL
EPS = 1e-6
NEG = -1e30
FORCE = 1e6
SCALE = HEAD_DIM ** -0.5

LANES = 128
CONV_HALO = 32
VMEM_LIMIT = 56 * 1024 * 1024


def _pick(n, cands):
    for c in cands:
        if n % c == 0:
            return c
    raise ValueError(f"no tile in {cands} divides {n}")


def _const_spec(shape):
    nd = len(shape)
    return pl.BlockSpec(shape, lambda *_: (0,) * nd, pipeline_mode=pl.Buffered(1))


def _params(sem):
    return pltpu.CompilerParams(dimension_semantics=sem, vmem_limit_bytes=VMEM_LIMIT)


def _sigmoid(x):
    return 1.0 / (1.0 + jnp.exp(-x))


def _rms(x, g):
    return x * lax.rsqrt(jnp.mean(x * x, axis=-1, keepdims=True) + EPS) * g


def _dot(a, b):
    return jnp.dot(a, b, preferred_element_type=F32)


def _dot_nt(a, b):
    return lax.dot_general(a, b, (((1,), (1,)), ((), ())), preferred_element_type=F32)


def _split_bf16(x):
    hi = x.astype(BF16)
    lo = (x - hi.astype(F32)).astype(BF16)
    return hi, lo


def _conv_front_body(x_ref, g_ref, w_ref, b_ref, u_ref):
    h = _rms(x_ref[...], g_ref[...]).astype(BF16)
    z = _dot(h, w_ref[...]) + b_ref[...]
    u_ref[...] = z[:, :D_MODEL] * _sigmoid(z[:, D_MODEL:])


def _conv_front(x2d, g, w_bf, b):
    n = x2d.shape[0]
    tm = _pick(n, (512, 256, 128, 64, 32, 16, 8))
    return pl.pallas_call(
        _conv_front_body,
        grid=(n // tm,),
        in_specs=[pl.BlockSpec((tm, D_MODEL), lambda i: (i, 0)),
                  _const_spec((1, D_MODEL)),
                  _const_spec((D_MODEL, 2 * D_MODEL)),
                  _const_spec((1, 2 * D_MODEL))],
        out_specs=pl.BlockSpec((tm, D_MODEL), lambda i: (i, 0)),
        out_shape=jax.ShapeDtypeStruct((n, D_MODEL), F32),
        compiler_params=_params(("parallel",)),
        name="conv_front",
    )(x2d, g, w_bf, b)


_CONV_ROWS = 32
_CONV_COLS = 256


def _conv_back_body(hist_ref, prev_ref, main_ref, wdw_ref, bdw_ref, lng_ref, lnb_ref, w2_ref, b2_ref, x_ref, g1_ref,
                    o_ref, win_ref, y_ref):
    tt = main_ref.shape[1]
    @pl.when(pl.program_id(1) == 0)
    def _():
        win_ref[0:CONV_HALO, :] = hist_ref[0]

    @pl.when(pl.program_id(1) > 0)
    def _():
        win_ref[0:CONV_HALO, :] = prev_ref[0]

    win_ref[CONV_HALO:CONV_HALO + tt, :] = main_ref[0]
    first = CONV_HALO - (CONV_W - 1)

    sub = 8
    for r0 in range(0, tt, _CONV_ROWS):
        for c0 in range(0, D_MODEL, _CONV_COLS):
            cols = slice(c0, c0 + _CONV_COLS)
            acc = jnp.zeros((_CONV_ROWS, _CONV_COLS), F32)
            for s in range(sub):
                part = None
                for k in range(CONV_W):
                    if (first + k) % sub != s:
                        continue
                    base = r0 + first + k - s
                    term = win_ref[base:base + _CONV_ROWS + (sub if s else 0), cols] * wdw_ref[k:k + 1, cols]
                    part = term if part is None else part + term
                if part is not None:
                    acc = acc + part[s:s + _CONV_ROWS]
            y_ref[r0:r0 + _CONV_ROWS, cols] = acc + bdw_ref[:, cols]
    y = y_ref[...]
    yc = y - jnp.mean(y, axis=-1, keepdims=True)
    var = jnp.mean(yc * yc, axis=-1, keepdims=True)
    ln = yc * lax.rsqrt(var + EPS) * lng_ref[...] + lnb_ref[...]
    act = (ln * _sigmoid(ln)).astype(BF16)
    m = _dot(act, w2_ref[...]) + b2_ref[...]
    o_ref[0] = x_ref[0] + _rms(m, g1_ref[...])


def _conv_back(hist, u3d, x3d, wdw, bdw, lng, lnb, w2_bf, b2, g1):
    b, t, _ = x3d.shape
    tt = _pick(t, (256, 128, 64, 32))
    halo_blocks = tt // CONV_HALO
    return pl.pallas_call(
        _conv_back_body,
        grid=(b, t // tt),
        in_specs=[pl.BlockSpec((1, CONV_HALO, D_MODEL), lambda bi, i: (bi, 0, 0)),
                  pl.BlockSpec((1, CONV_HALO, D_MODEL), lambda bi, i: (bi, jnp.maximum(i * halo_blocks - 1, 0), 0)),
                  pl.BlockSpec((1, tt, D_MODEL), lambda bi, i: (bi, i, 0)),
                  _const_spec((CONV_HALO, D_MODEL)),
                  _const_spec((1, D_MODEL)), _const_spec((1, D_MODEL)), _const_spec((1, D_MODEL)),
                  _const_spec((D_MODEL, D_MODEL)), _const_spec((1, D_MODEL)),
                  pl.BlockSpec((1, tt, D_MODEL), lambda bi, i: (bi, i, 0)),
                  _const_spec((1, D_MODEL))],
        out_specs=pl.BlockSpec((1, tt, D_MODEL), lambda bi, i: (bi, i, 0)),
        out_shape=jax.ShapeDtypeStruct((b, t, D_MODEL), F32),
        scratch_shapes=[pltpu.VMEM((tt + CONV_HALO, D_MODEL), F32), pltpu.VMEM((tt, D_MODEL), F32)],
        compiler_params=_params(("parallel", "parallel")),
        name="conv_back",
    )(hist, u3d, u3d, wdw, bdw, lng, lnb, w2_bf, b2, x3d, g1)


_FF_CHUNK = 1024


def _ffn_body(x_ref, g2_ref, w1_ref, w2_ref, g3_ref, wg_ref, bg_ref, p_ref, wp_ref, o_ref):
    x = x_ref[...]
    h = _rms(x, g2_ref[...]).astype(BF16)
    f = jnp.zeros(x.shape, F32)
    for c in range(D_FF // _FF_CHUNK):
        a = jnp.maximum(_dot(h, w1_ref[:, c * _FF_CHUNK:(c + 1) * _FF_CHUNK]), 0.0)
        f = f + _dot((a * a).astype(BF16), w2_ref[c * _FF_CHUNK:(c + 1) * _FF_CHUNK, :])
    x2 = x + _rms(f, g3_ref[...])
    gate = _sigmoid(_dot(x2.astype(BF16), wg_ref[...]) + bg_ref[...])
    o_ref[...] = x2 + gate * _dot(p_ref[...].astype(BF16), wp_ref[...])


def _ffn(x2d, g2, w1_bf, w2_bf, g3, wg_bf, bg, p2d, wp_bf):
    n = x2d.shape[0]
    d_ple = p2d.shape[1]
    tm = _pick(n, (512, 256, 128, 64, 32, 16, 8))
    return pl.pallas_call(
        _ffn_body,
        grid=(n // tm,),
        in_specs=[pl.BlockSpec((tm, D_MODEL), lambda i: (i, 0)),
                  _const_spec((1, D_MODEL)),
                  _const_spec((D_MODEL, D_FF)), _const_spec((D_FF, D_MODEL)),
                  _const_spec((1, D_MODEL)),
                  _const_spec((D_MODEL, D_MODEL)), _const_spec((1, D_MODEL)),
                  pl.BlockSpec((tm, d_ple), lambda i: (i, 0)),
                  _const_spec((d_ple, D_MODEL))],
        out_specs=pl.BlockSpec((tm, D_MODEL), lambda i: (i, 0)),
        out_shape=jax.ShapeDtypeStruct((n, D_MODEL), F32),
        compiler_params=_params(("parallel",)),
        name="ffn_ple",
    )(x2d, g2, w1_bf, w2_bf, g3, wg_bf, bg, p2d, wp_bf)


_PROJ_W = Q_DIM + 6 * KV_DIM + LANES
LOG2E = 1.4426950408889634
V_ROWS = HEAD_DIM + 16
_VT_ROWS = 2 * N_KV * V_ROWS


def _rope_block(blk, c, su, sd):
    return blk * c + pltpu.roll(blk, ROT_DIM // 2, 1) * su + pltpu.roll(blk, LANES - ROT_DIM // 2, 1) * sd


def _nsa_proj_body(prompt, x_ref, g_ref, w_ref, c_ref, su_ref, sd_ref, *refs):
    if prompt:
        blk_ref, cmp_ref, qt_ref, qrt_ref, vt_ref, gatet_ref = refs[:6]
        cache_t_refs, key_refs = refs[6:9], refs[9:11]
        low = lax.broadcasted_iota(jnp.int32, (x_ref.shape[0], LANES), 1) < HEAD_DIM
        ones = jnp.ones((V_ROWS - HEAD_DIM, x_ref.shape[0]), BF16)
    else:
        q_ref, qr_ref, cmp_ref, sel_ref, win_ref, gate_ref = refs
    h = _rms(x_ref[...], g_ref[...]).astype(BF16)
    z = _dot(h, w_ref[...])
    c, su, sd = c_ref[...], su_ref[...], sd_ref[...]
    for i in range(Q_DIM // LANES):
        cols = slice(i * LANES, (i + 1) * LANES)
        qs = z[:, cols] * SCALE
        qrs = _rope_block(z[:, cols], c, su, sd) * SCALE
        if prompt:
            qt_ref[0, cols, :] = (qs * LOG2E).T.astype(BF16)
            qrt_ref[0, cols, :] = (qrs * LOG2E).T.astype(BF16)
        else:
            q_ref[:, cols] = qs.astype(BF16)
            qr_ref[:, cols] = qrs.astype(BF16)
    o = Q_DIM
    cmp_ref[...] = z[:, o:o + 2 * KV_DIM]
    if prompt:
        for i in range(2 * KV_DIM // LANES):
            cache_t_refs[0][0, i * LANES:(i + 1) * LANES, :] = z[:, o + i * LANES:o + (i + 1) * LANES].T
    o += 2 * KV_DIM
    for n in range(2):
        for i in range(KV_DIM // LANES):
            k_rot = _rope_block(z[:, o + i * LANES:o + (i + 1) * LANES], c, su, sd)
            if not prompt:
                (sel_ref, win_ref)[n][:, i * LANES:(i + 1) * LANES] = k_rot
                continue
            vt = z[:, o + KV_DIM + i * LANES:o + KV_DIM + (i + 1) * LANES].T
            cache_t_refs[1 + n][0, i * LANES:(i + 1) * LANES, :] = k_rot.T
            cache_t_refs[1 + n][0, KV_DIM + i * LANES:KV_DIM + (i + 1) * LANES, :] = vt
            fill = blk_ref[...] if n == 0 else 0.0
            per_tile = LANES // HEAD_DIM
            key_refs[n][per_tile * i] = jnp.where(low, k_rot, fill).astype(BF16)
            key_refs[n][per_tile * i + 1] = jnp.where(low, pltpu.roll(k_rot, HEAD_DIM, 1), fill).astype(BF16)
            for k in range(per_tile):
                base = (n * N_KV + i * per_tile + k) * V_ROWS
                vt_ref[0, base:base + HEAD_DIM, :] = vt[k * HEAD_DIM:(k + 1) * HEAD_DIM].astype(BF16)
                vt_ref[0, base + HEAD_DIM:base + V_ROWS, :] = ones
        if not prompt:
            (sel_ref, win_ref)[n][:, KV_DIM:] = z[:, o + KV_DIM:o + 2 * KV_DIM]
        o += 2 * KV_DIM
    gates = _sigmoid(z[:, o:o + LANES])
    if prompt:
        gatet_ref[0] = gates.T
    else:
        gate_ref[...] = gates


def _rope_tables(pos, rows):
    half = ROT_DIM // 2
    inv = jnp.float32(ROPE_THETA) ** (-jnp.arange(half, dtype=F32) * (2.0 / ROT_DIM))
    ang = pos.astype(F32)[:, None] * inv[None, :]
    lane = np.arange(LANES)
    within = lane % HEAD_DIM
    cos = jnp.cos(ang)[:, lane % half]
    sin = jnp.sin(ang)[:, lane % half]
    c = jnp.where(within[None, :] < ROT_DIM, cos, 1.0)
    su = jnp.where((within[None, :] >= half) & (within[None, :] < ROT_DIM), sin, 0.0)
    sd = jnp.where(within[None, :] < half, -sin, 0.0)
    reps = rows // pos.shape[0]
    return tuple(jnp.tile(a, (reps, 1)) for a in (c, su, sd))


def _nsa_proj(x2d, g, w_bf, pos):
    n = x2d.shape[0]
    t = pos.shape[0]
    tm = _pick(n, (256, 128, 64, 32, 16, 8))
    if t >= tm:
        assert t % tm == 0
        tabs = _rope_tables(pos, t)
        nt = t // tm
        tab_spec = pl.BlockSpec((tm, LANES), lambda i: (i % nt, 0))
    else:
        assert tm % t == 0
        tabs = _rope_tables(pos, tm)
        tab_spec = _const_spec((tm, LANES))
    row = lambda w: pl.BlockSpec((tm, w), lambda i: (i, 0))
    rows_of = lambda w, dt: jax.ShapeDtypeStruct((n, w), dt)
    in_specs = [row(D_MODEL), _const_spec((1, D_MODEL)), _const_spec((D_MODEL, _PROJ_W)),
                tab_spec, tab_spec, tab_spec]
    args = [x2d, g, w_bf, *tabs]
    prompt = t >= tm
    if prompt:
        assert t // L_SEL <= HEAD_DIM
        blk = np.zeros((t, LANES), np.float32)
        blk[np.arange(t), HEAD_DIM + np.arange(t) // L_SEL] = 1.0
        in_specs.append(tab_spec)
        args.append(jnp.asarray(blk))
        tile = lambda rows: pl.BlockSpec((1, rows, tm), lambda i: (i, 0, 0))
        tiles = lambda rows, dt: jax.ShapeDtypeStruct((n // tm, rows, tm), dt)
        out_specs = [row(2 * KV_DIM), tile(Q_DIM), tile(Q_DIM), tile(_VT_ROWS), tile(LANES)]
        out_shape = [rows_of(2 * KV_DIM, F32), tiles(Q_DIM, BF16), tiles(Q_DIM, BF16), tiles(_VT_ROWS, BF16),
                     tiles(LANES, F32)]
        out_specs += [pl.BlockSpec((1, 2 * KV_DIM, tm), lambda i: (i // nt, 0, i % nt))] * 3
        out_shape += [jax.ShapeDtypeStruct((n // t, 2 * KV_DIM, t), F32)] * 3
        out_specs += [pl.BlockSpec((N_KV, tm, LANES), lambda i: (0, i, 0))] * 2
        out_shape += [jax.ShapeDtypeStruct((N_KV, n, LANES), BF16)] * 2
    else:
        out_specs = [row(Q_DIM), row(Q_DIM), row(2 * KV_DIM), row(2 * KV_DIM), row(2 * KV_DIM), row(LANES)]
        out_shape = [rows_of(Q_DIM, BF16), rows_of(Q_DIM, BF16), rows_of(2 * KV_DIM, F32),
                     rows_of(2 * KV_DIM, F32), rows_of(2 * KV_DIM, F32), rows_of(LANES, F32)]
    return pl.pallas_call(
        functools.partial(_nsa_proj_body, prompt),
        grid=(n // tm,),
        in_specs=in_specs,
        out_specs=out_specs,
        out_shape=out_shape,
        compiler_params=_params(("parallel",)),
        name="nsa_proj",
    )(*args)


_CHUNK_W = STRIDE * 2 * KV_DIM
_AB_W = 2 * N_KV * CMP_HID


_S_PER_DOT = 4


def _first_layer(get_tile, w_ref, rows):
    half = LANES // 2
    low = lax.broadcasted_iota(jnp.int32, (rows, LANES), 1) < half
    out = []
    for j in range(KV_DIM // LANES):
        acc = [jnp.zeros((rows, 2 * CMP_HID), F32) for _ in range(2)]
        for sq in range(STRIDE // _S_PER_DOT):
            even, odd = [], []
            for s in range(sq * _S_PER_DOT, (sq + 1) * _S_PER_DOT, 2):
                a, b = get_tile(s, j), get_tile(s + 1, j)
                even.append(jnp.where(low, a, pltpu.roll(b, half, 1)))
                odd.append(jnp.where(low, pltpu.roll(a, half, 1), b))
            acc[0] = acc[0] + _dot(jnp.concatenate(even, axis=1).astype(BF16), w_ref[sq])
            acc[1] = acc[1] + _dot(jnp.concatenate(odd, axis=1).astype(BF16), w_ref[sq])
        out += acc
    return out


def _store_ab(ab_ref, rows_slice, accs):
    for g, acc in enumerate(accs):
        ab_ref[rows_slice, g * CMP_HID:(g + 1) * CMP_HID] = acc[:, :CMP_HID]
        ab_ref[rows_slice, (N_KV + g) * CMP_HID:(N_KV + g + 1) * CMP_HID] = acc[:, CMP_HID:]


def _chunk_rows_tile(src_ref, kv):
    def get(s, j):
        lo = s * 2 * KV_DIM + kv * KV_DIM + j * LANES
        return src_ref[0, :, lo:lo + LANES]
    return get


def _compress_finish(xl_ref, per_kv, n_rows):
    half = _AB_W // 2
    for kv, (ab_ref, w_ref, pe_ref, w1_ref, w2_ref, out_ref) in enumerate(per_kv):
        _store_ab(ab_ref, slice(n_rows, n_rows + 8), _first_layer(_chunk_rows_tile(xl_ref, kv), w_ref, 8))
        c = jnp.sum(pe_ref[...] * w1_ref[...], axis=0, keepdims=True)
        c = jnp.concatenate([c] * N_KV, axis=1)
        pre = ab_ref[0:n_rows, 0:half] + ab_ref[1:n_rows + 1, half:] + c
        hid = (pre * _sigmoid(pre)).astype(BF16)
        out_ref[0] = _dot(hid, w2_ref[...])


def _compress_body(x_ref, xl_ref, wk_ref, wv_ref, pek_ref, pev_ref, w1k_ref, w1v_ref, w2k_ref, w2v_ref,
                   kcc_ref, vcc_ref, abk_ref, abv_ref):
    rt = x_ref.shape[1]
    i = pl.program_id(1)
    n_rt = pl.num_programs(1)
    rows = pl.ds(pl.multiple_of(i * rt, rt), rt)
    _store_ab(abk_ref, rows, _first_layer(_chunk_rows_tile(x_ref, 0), wk_ref, rt))
    _store_ab(abv_ref, rows, _first_layer(_chunk_rows_tile(x_ref, 1), wv_ref, rt))

    @pl.when(i == n_rt - 1)
    def _():
        _compress_finish(xl_ref, ((abk_ref, wk_ref, pek_ref, w1k_ref, w2k_ref, kcc_ref),
                                  (abv_ref, wv_ref, pev_ref, w1v_ref, w2v_ref, vcc_ref)), kcc_ref.shape[1])


def _compress(x_chunks, x_last, wk, wv, pek, pev, w1k, w1v, w2k, w2v):
    b, n, _ = x_chunks.shape
    rt = _pick(n, (128, 64, 32, 16, 8)) if n % 8 == 0 else n
    flat = L_CMP * HEAD_DIM
    out = jax.ShapeDtypeStruct((b, n, KV_DIM), F32)
    return pl.pallas_call(
        _compress_body,
        grid=(b, n // rt),
        in_specs=[pl.BlockSpec((1, rt, _CHUNK_W), lambda bi, i: (bi, i, 0)),
                  pl.BlockSpec((1, 8, _CHUNK_W), lambda bi, i: (bi, 0, 0)),
                  _const_spec(_W1_SHAPE), _const_spec(_W1_SHAPE),
                  _const_spec((flat, 1)), _const_spec((flat, 1)),
                  _const_spec((flat, CMP_HID)), _const_spec((flat, CMP_HID)),
                  _const_spec((N_KV * CMP_HID, KV_DIM)), _const_spec((N_KV * CMP_HID, KV_DIM))],
        out_specs=[pl.BlockSpec((1, n, KV_DIM), lambda bi, i: (bi, 0, 0))] * 2,
        out_shape=[out, out],
        scratch_shapes=[pltpu.VMEM((n + 8, _AB_W), F32), pltpu.VMEM((n + 8, _AB_W), F32)],
        compiler_params=_params(("parallel", "arbitrary")),
        name="compress",
    )(x_chunks, x_last, wk, wv, pek, pev, w1k, w1v, w2k, w2v)


_W1_SHAPE = (STRIDE // _S_PER_DOT, _S_PER_DOT * HEAD_DIM, 2 * CMP_HID)


def _compress_weights(cmp_pe, cmp_w1, cmp_w2):
    eye = jnp.eye(N_KV, dtype=F32)
    outs = []
    for kv in range(2):
        w1 = cmp_w1[kv].reshape(2, STRIDE // _S_PER_DOT, _S_PER_DOT, HEAD_DIM, CMP_HID)
        stacked = w1.transpose(1, 2, 3, 0, 4).reshape(_W1_SHAPE)
        w2 = jnp.einsum('gk,hd->ghkd', eye, cmp_w2[kv]).reshape(N_KV * CMP_HID, KV_DIM)
        outs.append((stacked.astype(BF16), cmp_pe[kv].reshape(L_CMP * HEAD_DIM, 1),
                     cmp_w1[kv].reshape(L_CMP * HEAD_DIM, CMP_HID), w2.astype(BF16)))
    return outs


def _select_blocks(imp_t, blk, qpos, n_real, k):
    cur = jnp.right_shift(qpos, int(math.log2(L_SEL)))
    forced = (blk == 0) | (blk == cur) | (blk == cur - 1)
    future = blk * L_SEL > qpos
    score = jnp.where(future, -jnp.inf, jnp.where(forced, FORCE, imp_t))
    score = jnp.where(blk < n_real, score, -jnp.inf)
    sub = 8
    n_groups = -(-n_real // sub)
    rows = [score[i * sub:(i + 1) * sub] for i in range(n_groups)]
    local = lax.broadcasted_iota(jnp.int32, rows[0].shape, 0)
    rank = [jnp.zeros(r.shape, jnp.int32) for r in rows]
    for j in range(n_real):
        sj = score[j:j + 1, :]
        for n, r in enumerate(rows):
            if n * sub > j:
                beats = sj >= r
            elif n * sub + sub - 1 <= j:
                beats = sj > r
            else:
                beats = jnp.where(local > j - n * sub, jnp.where(sj >= r, 1, 0), jnp.where(sj > r, 1, 0)) > 0
            rank[n] = rank[n] + jnp.where(beats, 1, 0)
    pad = score.shape[0] - n_groups * sub
    if pad:
        rank.append(jnp.full((pad, score.shape[1]), k, jnp.int32))
    return (jnp.concatenate(rank, axis=0) < k) & (blk < n_real)


def _overlap_t(n_sel_pad, n_cmp_pad, n_sel, n_cmp):
    ci = np.arange(n_cmp_pad)[None, :] * STRIDE
    sj = np.arange(n_sel_pad)[:, None] * L_SEL
    ov = (ci < sj + L_SEL) & (ci + L_CMP > sj)
    ov &= (np.arange(n_cmp_pad)[None, :] < n_cmp) & (np.arange(n_sel_pad)[:, None] < n_sel)
    return jnp.asarray(ov, dtype=BF16)


def _head_rows(h):
    return slice(h * HEAD_DIM, (h + 1) * HEAD_DIM)


def _cmp_topk_body(n_sel, qt_ref, kcc_ref, vcct_ref, gt_ref, ovt_ref, ocg_ref, bias_ref):
    tq = qt_ref.shape[2]
    nc = kcc_ref.shape[2]
    nsp = ovt_ref.shape[0]
    t0 = pl.program_id(1) * tq
    wide = N_REP * tq
    qpos = t0 + (lax.broadcasted_iota(jnp.int32, (1, wide), 1) & (tq - 1))
    cmp_end = lax.broadcasted_iota(jnp.int32, (nc, 1), 0) * STRIDE + (L_CMP - 1)
    cmask = cmp_end <= qpos
    blk = lax.broadcasted_iota(jnp.int32, (nsp, tq), 0)
    qpos_t = t0 + lax.broadcasted_iota(jnp.int32, (nsp, tq), 1)
    ovt = ovt_ref[...]
    for g in range(N_KV):
        heads = range(g * N_REP, (g + 1) * N_REP)
        q = jnp.concatenate([qt_ref[0, _head_rows(h), :] for h in heads], axis=1)
        lm = jnp.where(cmask, _dot(kcc_ref[0, g], q), NEG)
        e = jnp.exp2(lm - jnp.max(lm, axis=0, keepdims=True))
        pc = jnp.where(cmask, e * (1.0 / jnp.sum(e, axis=0, keepdims=True)), 0.0)
        o = _dot(vcct_ref[0, _head_rows(g), :], pc.astype(BF16))
        pg = jnp.zeros((nc, tq), F32)
        for r, h in enumerate(heads):
            ocg_ref[0, _head_rows(h), :] = o[:, r * tq:(r + 1) * tq] * gt_ref[0, 3 * h:3 * h + 1, :]
            pg = pg + pc[:, r * tq:(r + 1) * tq]
        hi, lo = _split_bf16(pg)
        imp_t = _dot(ovt, hi) + _dot(ovt, lo)
        sel = _select_blocks(imp_t, blk, qpos_t, n_sel, min(N_TOP, n_sel))
        bias_ref[0, g] = jnp.where(sel, 0.0, NEG).astype(BF16)


def _cmp_topk(q_t, kcc_gm, vcc_t, gates_t, b, n_cmp, n_sel, nsp):
    tq = q_t.shape[2]
    n_t = q_t.shape[0] // b
    nc = kcc_gm.shape[2]
    ovt = _overlap_t(nsp, nc, n_sel, n_cmp)
    tile = lambda rows: pl.BlockSpec((1, rows, tq), lambda bi, i: (bi * n_t + i, 0, 0))
    return pl.pallas_call(
        functools.partial(_cmp_topk_body, n_sel),
        grid=(b, n_t),
        in_specs=[tile(Q_DIM),
                  pl.BlockSpec((1, N_KV, nc, HEAD_DIM), lambda bi, i: (bi, 0, 0, 0)),
                  pl.BlockSpec((1, KV_DIM, nc), lambda bi, i: (bi, 0, 0)),
                  tile(LANES), _const_spec((nsp, nc))],
        out_specs=[tile(Q_DIM), pl.BlockSpec((1, N_KV, nsp, tq), lambda bi, i: (bi, 0, 0, i))],
        out_shape=[jax.ShapeDtypeStruct((b * n_t, Q_DIM, tq), F32),
                   jax.ShapeDtypeStruct((b, N_KV, nsp, n_t * tq), BF16)],
        compiler_params=_params(("parallel", "parallel")),
        name="cmp_topk",
    )(q_t, kcc_gm, vcc_t, gates_t, ovt)


def _flash_init(m_ref, l_ref, acc_ref):
    m_ref[...] = jnp.full(m_ref.shape, NEG, F32)
    l_ref[...] = jnp.zeros(l_ref.shape, F32)
    acc_ref[...] = jnp.zeros(acc_ref.shape, F32)


def _flash_t(state, s, vt):
    m, acc = state
    m_new = jnp.maximum(m, jnp.max(s, axis=0, keepdims=True))
    return m_new, jnp.exp2(m - m_new) * acc + _dot(vt, jnp.exp2(s - m_new).astype(BF16))


def _v_rows(branch, g):
    base = (branch * N_KV + g) * V_ROWS
    return slice(base, base + V_ROWS)


def _sel_win_body(qrt_ref, bias_ref, kp_ref, kw_ref, vt_ref, ocg_ref, gt_ref, w_ref, x_ref, g1_ref,
                  o_ref, ot_ref):
    tq = qrt_ref.shape[2]
    kc = tq
    qt = pl.program_id(1)
    wide = N_REP * tq
    krow = lax.broadcasted_iota(jnp.int32, (kc, wide), 0)
    qcol = lax.broadcasted_iota(jnp.int32, (kc, wide), 1) & (tq - 1)
    n_back = WINDOW // kc
    fresh = (jnp.full((1, wide), NEG, F32), jnp.zeros((V_ROWS, wide), F32))

    for g in range(N_KV):
        heads = range(g * N_REP, (g + 1) * N_REP)
        qw = jnp.concatenate([qrt_ref[0, _head_rows(h), :] for h in heads], axis=1)
        qs = jnp.concatenate([qw, jnp.concatenate([bias_ref[0, g]] * N_REP, axis=1)], axis=0)
        qw = jnp.concatenate([qw, jnp.zeros_like(qw)], axis=0)

        def scores(j):
            return _dot(kp_ref[g, pl.ds(pl.multiple_of(j * kc, kc), kc), :], qs)

        def sel_chunk(j, state, diagonal):
            s = scores(j)
            return _flash_t(state, jnp.where(krow <= qcol, s, NEG) if diagonal else s, vt_ref[j, _v_rows(0, g), :])

        def sel_run(j0, n, state):
            ahead = [scores(j0), scores(j0 + 1)] if n > 1 else [scores(j0)]
            for k in range(n):
                s = ahead.pop(0)
                state = _flash_t(state, s, vt_ref[j0 + k, _v_rows(0, g), :])
                if k + 2 < n:
                    ahead.append(scores(j0 + k + 2))
            return state

        unroll = 4
        state = lax.fori_loop(0, qt // unroll, lambda i, st: sel_run(unroll * i, unroll, st), fresh)
        rest = qt % unroll
        done = qt - rest
        state = lax.cond(rest >= 2, lambda st: sel_run(done, 2, st), lambda st: st, state)
        state = lax.cond(rest % 2 == 1, lambda st: sel_chunk(qt - 1, st, False), lambda st: st, state)

        def win_scores(back):
            s = _dot(kw_ref[g, pl.ds(pl.multiple_of((qt - back) * kc, kc), kc), :], qw)
            if back == 0:
                return jnp.where(krow <= qcol, s, NEG)
            return jnp.where(krow > qcol, s, NEG) if back == n_back else s

        def win_seq(backs, state):
            for back, s in [(back, win_scores(back)) for back in backs]:
                state = _flash_t(state, s, vt_ref[qt - back, _v_rows(1, g), :])
            return state

        s_sel, s_win = scores(qt), win_scores(0)
        _, acc_sel = _flash_t(state, jnp.where(krow <= qcol, s_sel, NEG), vt_ref[qt, _v_rows(0, g), :])
        state = _flash_t(fresh, s_win, vt_ref[qt, _v_rows(1, g), :])
        older = lambda st: st
        for k in range(1, n_back + 1):
            older = functools.partial(lambda k, fewer, st: lax.cond(
                qt >= k, functools.partial(win_seq, range(1, k + 1)), fewer, st), k, older)
        _, acc_win = older(state)

        den = slice(HEAD_DIM, HEAD_DIM + 1)
        for r, h in enumerate(heads):
            cols = slice(r * tq, (r + 1) * tq)
            ot_ref[_head_rows(h), :] = (
                ocg_ref[0, _head_rows(h), :]
                + acc_sel[:HEAD_DIM, cols] * (gt_ref[0, 3 * h + 1:3 * h + 2, :] * (1.0 / acc_sel[den, cols]))
                + acc_win[:HEAD_DIM, cols] * (gt_ref[0, 3 * h + 2:3 * h + 3, :] * (1.0 / acc_win[den, cols])))

    m = _dot(ot_ref[...].T.astype(BF16), w_ref[...])
    o_ref[...] = x_ref[...] + _rms(m, g1_ref[...])


def _sel_win_out(qr_t, bias_t, k_sel, k_win, v_t, ocg_t, gates_t, w_out_bf, x2d, g1, b):
    tq = qr_t.shape[2]
    n_t = qr_t.shape[0] // b
    t = n_t * tq
    nsp = bias_t.shape[2]
    assert WINDOW % tq == 0 and HEAD_DIM + nsp == LANES
    tile = lambda rows: pl.BlockSpec((1, rows, tq), lambda bi, i: (bi * n_t + i, 0, 0))
    keys = pl.BlockSpec((N_KV, t, LANES), lambda bi, i: (0, bi, 0))
    xrow = pl.BlockSpec((tq, D_MODEL), lambda bi, i: (bi * n_t + i, 0))
    return pl.pallas_call(
        _sel_win_body,
        grid=(b, n_t),
        in_specs=[tile(Q_DIM),
                  pl.BlockSpec((1, N_KV, nsp, tq), lambda bi, i: (bi, 0, 0, i)),
                  keys, keys,
                  pl.BlockSpec((n_t, _VT_ROWS, tq), lambda bi, i: (bi, 0, 0)),
                  tile(Q_DIM), tile(LANES),
                  _const_spec((Q_DIM, D_MODEL)), xrow, _const_spec((1, D_MODEL))],
        out_specs=xrow,
        out_shape=jax.ShapeDtypeStruct((b * t, D_MODEL), F32),
        scratch_shapes=[pltpu.VMEM((Q_DIM, tq), F32)],
        compiler_params=_params(("parallel", "parallel")),
        name="sel_win_out",
    )(qr_t, bias_t, k_sel, k_win, v_t, ocg_t, gates_t, w_out_bf, x2d, g1)


def _nsa_out_body(oc_ref, os_ref, ow_ref, gate_ref, e_ref, w_ref, x_ref, g1_ref, o_ref):
    hi, lo = _split_bf16(gate_ref[...])
    o = jnp.zeros(oc_ref.shape, F32)
    for c, src in enumerate((oc_ref, os_ref, ow_ref)):
        o = o + (_dot(hi, e_ref[c]) + _dot(lo, e_ref[c])) * src[...]
    m = _dot(o.astype(BF16), w_ref[...])
    o_ref[...] = x_ref[...] + _rms(m, g1_ref[...])


def _gate_expand():
    e = np.zeros((3, LANES, Q_DIM), np.float32)
    for h in range(N_HEADS):
        for c in range(3):
            e[c, h * 3 + c, h * HEAD_DIM:(h + 1) * HEAD_DIM] = 1.0
    return jnp.asarray(e, dtype=BF16)


def _nsa_out(o_cmp, o_sel, o_win, gates, w_bf, x2d, g1):
    n = x2d.shape[0]
    tm = _pick(n, (256, 128, 64, 32, 16, 8))
    row = lambda w: pl.BlockSpec((tm, w), lambda i: (i, 0))
    return pl.pallas_call(
        _nsa_out_body,
        grid=(n // tm,),
        in_specs=[row(Q_DIM), row(Q_DIM), row(Q_DIM), row(LANES), _const_spec((3, LANES, Q_DIM)),
                  _const_spec((Q_DIM, D_MODEL)), row(D_MODEL), _const_spec((1, D_MODEL))],
        out_specs=row(D_MODEL),
        out_shape=jax.ShapeDtypeStruct((n, D_MODEL), F32),
        compiler_params=_params(("parallel",)),
        name="nsa_out",
    )(o_cmp, o_sel, o_win, gates, _gate_expand(), w_bf, x2d, g1)


class _PageRing:
    def __init__(self, pt_ref, cache_hbm, buf_ref, sem_ref, n_b, n_c):
        self.pt, self.cache, self.buf, self.sem = pt_ref, cache_hbm, buf_ref, sem_ref
        self.n_b, self.n_c, self.pg = n_b, n_c, buf_ref.shape[1]

    def _copies(self, step):
        b, c, slot = step // self.n_c, step % self.n_c, step % 2
        return [pltpu.make_async_copy(self.cache.at[self.pt[b, c * self.pg + p]], self.buf.at[slot, p],
                                      self.sem.at[slot]) for p in range(self.pg)]

    def acquire(self, b, c):
        step = b * self.n_c + c

        @pl.when(step == 0)
        def _():
            for cp in self._copies(step):
                cp.start()

        @pl.when(step + 1 < self.n_b * self.n_c)
        def _():
            for cp in self._copies(step + 1):
                cp.start()

        for cp in self._copies(step):
            cp.wait()
        return step % 2


def _compress_paged_body(n_b, n_c, pt_ref, cache_hbm, perm_ref, xl_ref, wk_ref, wv_ref, pek_ref, pev_ref,
                         w1k_ref, w1v_ref, w2k_ref, w2v_ref, kcc_ref, vcc_ref, buf_ref, sem_ref, xs_ref,
                         abk_ref, abv_ref):
    pg, page = buf_ref.shape[1], buf_ref.shape[4]
    per_page = page // STRIDE
    rows = pg * per_page
    b, c = pl.program_id(0), pl.program_id(1)
    slot = _PageRing(pt_ref, cache_hbm, buf_ref, sem_ref, n_b, n_c).acquire(b, c)
    r0 = pl.multiple_of(c * rows, rows)
    for kv, (ab_ref, w_ref) in enumerate(((abk_ref, wk_ref), (abv_ref, wv_ref))):
        for p in range(pg):
            t = _dot_nt(perm_ref[...], buf_ref[slot, p, kv].astype(BF16))
            for s in range(STRIDE):
                xs_ref[s, p * per_page:(p + 1) * per_page, :] = t[s * per_page:(s + 1) * per_page]
        tile = lambda s, j: xs_ref[s, :, j * LANES:(j + 1) * LANES]
        _store_ab(ab_ref, pl.ds(r0, rows), _first_layer(tile, w_ref, rows))

    @pl.when(c == n_c - 1)
    def _():
        _compress_finish(xl_ref, ((abk_ref, wk_ref, pek_ref, w1k_ref, w2k_ref, kcc_ref),
                                  (abv_ref, wv_ref, pev_ref, w1v_ref, w2v_ref, vcc_ref)), kcc_ref.shape[1])


def _compress_paged(page_table, cache_fm, x_last, wk, wv, pek, pev, w1k, w1v, w2k, w2v):
    b, n_pages = page_table.shape
    page = cache_fm.shape[3]
    pg = _pick(n_pages, (16, 8, 4, 2, 1))
    n_c = n_pages // pg
    n = n_pages * page // STRIDE
    flat = L_CMP * HEAD_DIM
    out = jax.ShapeDtypeStruct((b, n, KV_DIM), F32)
    const = _const_spec
    per_page = page // STRIDE
    assert per_page % 8 == 0
    tok = np.arange(page)
    perm = np.zeros((page, page), np.float32)
    perm[(tok % STRIDE) * per_page + tok // STRIDE, tok] = 1.0
    return pl.pallas_call(
        functools.partial(_compress_paged_body, b, n_c),
        grid_spec=pltpu.PrefetchScalarGridSpec(
            num_scalar_prefetch=1,
            grid=(b, n_c),
            in_specs=[pl.BlockSpec(memory_space=pl.ANY),
                      const((page, page)),
                      pl.BlockSpec((1, 8, _CHUNK_W), lambda bi, c, pt: (bi, 0, 0)),
                      const(_W1_SHAPE), const(_W1_SHAPE),
                      const((flat, 1)), const((flat, 1)), const((flat, CMP_HID)), const((flat, CMP_HID)),
                      const((N_KV * CMP_HID, KV_DIM)), const((N_KV * CMP_HID, KV_DIM))],
            out_specs=[pl.BlockSpec((1, n, KV_DIM), lambda bi, c, pt: (bi, 0, 0))] * 2,
            scratch_shapes=[pltpu.VMEM((2, pg, 2, KV_DIM, page), F32), pltpu.SemaphoreType.DMA((2,)),
                            pltpu.VMEM((STRIDE, pg * per_page, KV_DIM), F32),
                            pltpu.VMEM((n + 8, _AB_W), F32), pltpu.VMEM((n + 8, _AB_W), F32)]),
        out_shape=[out, out],
        compiler_params=_params(("arbitrary", "arbitrary")),
        name="compress_paged",
    )(page_table, cache_fm, jnp.asarray(perm, dtype=BF16), x_last, wk, wv, pek, pev, w1k, w1v, w2k, w2v)


def _softmax_rows(s):
    e = jnp.exp(s - jnp.max(s, axis=-1, keepdims=True))
    return e / jnp.sum(e, axis=-1, keepdims=True)


def _sample_cmp_body(n_sel, past_len, t_new, q_ref, kcc_ref, vcc_ref, ovt_ref, rep_ref, o_ref, sel_ref):
    rows = q_ref.shape[1]
    nc = kcc_ref.shape[1]
    nsp = ovt_ref.shape[0]
    gq = N_KV * t_new
    tq_col = lax.broadcasted_iota(jnp.int32, (rows, 1), 0) & (t_new - 1)
    cmp_end = lax.broadcasted_iota(jnp.int32, (1, nc), 1) * STRIDE + (L_CMP - 1)
    cmask = cmp_end <= past_len + tq_col
    lm = jnp.where(cmask, _dot_nt(q_ref[0], kcc_ref[0].astype(BF16)), NEG)
    pc = jnp.where(cmask, _softmax_rows(lm), 0.0)
    o_ref[0] = _dot(pc.astype(BF16), vcc_ref[0].astype(BF16))
    pg = pc.reshape(N_KV, N_REP, t_new, nc).sum(axis=1).reshape(gq, nc)
    hi, lo = _split_bf16(pg)
    imp_t = _dot_nt(ovt_ref[...], hi) + _dot_nt(ovt_ref[...], lo)
    blk = lax.broadcasted_iota(jnp.int32, (nsp, gq), 0)
    qpos = past_len + (lax.broadcasted_iota(jnp.int32, (nsp, gq), 1) & (t_new - 1))
    sel = _select_blocks(imp_t, blk, qpos, n_sel, min(N_TOP, n_sel))
    sel_ref[0] = _dot_nt(rep_ref[...], jnp.where(sel, 1.0, 0.0).astype(BF16)).astype(BF16)


def _sample_cmp(q_bd, kcc, vcc, n_cmp, n_sel, nsp, past_len, t_new):
    b, rows, _ = q_bd.shape
    nc = kcc.shape[1]
    gq = N_KV * t_new
    ovt = _overlap_t(nsp, nc, n_sel, n_cmp)
    rep = np.zeros((rows, gq), np.float32)
    for h in range(N_HEADS):
        for t in range(t_new):
            rep[h * t_new + t, (h // N_REP) * t_new + t] = 1.0
    return pl.pallas_call(
        functools.partial(_sample_cmp_body, n_sel, past_len, t_new),
        grid=(b,),
        in_specs=[pl.BlockSpec((1, rows, KV_DIM), lambda bi: (bi, 0, 0)),
                  pl.BlockSpec((1, nc, KV_DIM), lambda bi: (bi, 0, 0)),
                  pl.BlockSpec((1, nc, KV_DIM), lambda bi: (bi, 0, 0)),
                  _const_spec((nsp, nc)), _const_spec((rows, gq))],
        out_specs=[pl.BlockSpec((1, rows, KV_DIM), lambda bi: (bi, 0, 0)),
                   pl.BlockSpec((1, rows, nsp), lambda bi: (bi, 0, 0))],
        out_shape=[jax.ShapeDtypeStruct((b, rows, KV_DIM), F32), jax.ShapeDtypeStruct((b, rows, nsp), BF16)],
        compiler_params=_params(("parallel",)),
        name="sample_cmp",
    )(q_bd, kcc, vcc, ovt, jnp.asarray(rep, dtype=BF16))


def _flash_update(s, pv_fn, m_ref, l_ref, acc_ref):
    m_prev = m_ref[...]
    m_new = jnp.maximum(m_prev, jnp.max(s, axis=-1, keepdims=True))
    alpha = jnp.exp(m_prev - m_new)
    p = jnp.exp(s - m_new)
    l_ref[...] = alpha * l_ref[...] + jnp.sum(p, axis=-1, keepdims=True)
    acc_ref[...] = alpha * acc_ref[...] + pv_fn(p.astype(BF16))
    m_ref[...] = m_new


def _sample_sel_body(t_new, n_b, n_c, pt_ref, cache_hbm, q_ref, sel_ref, e_ref, et_ref, tail_ref, o_ref,
                     buf_ref, sem_ref, m_ref, l_ref, acc_ref):
    rows = q_ref.shape[1]
    pg, page = buf_ref.shape[1], buf_ref.shape[4]
    b, c = pl.program_id(0), pl.program_id(1)
    q = q_ref[0]

    @pl.when(c == 0)
    def _():
        _flash_init(m_ref, l_ref, acc_ref)

    @pl.when(c < n_c)
    def _():
        slot = _PageRing(pt_ref, cache_hbm, buf_ref, sem_ref, n_b, n_c).acquire(b, c)
        halves = [range(0, pg // 2), range(pg // 2, pg)] if pg > 1 else [range(pg)]

        def scores(pages):
            s = jnp.concatenate([_dot(q, buf_ref[slot, p, 0].astype(BF16)) for p in pages], axis=1)
            ok = _dot(sel_ref[0], e_ref[:, pages.start * page:pages.stop * page]) > 0.5
            return jnp.where(ok, s, NEG)

        for pages, s in [(pages, scores(pages)) for pages in halves]:
            def pv(p_bf, pages=pages):
                return sum(_dot_nt(p_bf[:, i * page:(i + 1) * page], buf_ref[slot, p, 1].astype(BF16))
                           for i, p in enumerate(pages))

            _flash_update(s, pv, m_ref, l_ref, acc_ref)

    @pl.when(c == n_c)
    def _():
        kv = tail_ref[0]
        nk = kv.shape[0]
        s = _dot_nt(q, kv[:, :KV_DIM].astype(BF16))
        tq = lax.broadcasted_iota(jnp.int32, (rows, 1), 0) & (t_new - 1)
        ok = (_dot(sel_ref[0], et_ref[...]) > 0.5) & (lax.broadcasted_iota(jnp.int32, (1, nk), 1) <= tq)
        _flash_update(jnp.where(ok, s, NEG), lambda p_bf: _dot(p_bf, kv[:, KV_DIM:].astype(BF16)),
                      m_ref, l_ref, acc_ref)
        o_ref[0] = acc_ref[...] / l_ref[...]


def _sample_sel(page_table, cache_fm, q_bd, sel01, tail_sel, n_sel, t_new):
    b, rows, _ = q_bd.shape
    nsp = sel01.shape[2]
    n_pages = page_table.shape[1]
    page = cache_fm.shape[3]
    pg = _pick(n_pages, (16, 8, 4, 2, 1))
    n_c = n_pages // pg
    kc = pg * page
    nk = tail_sel.shape[1]
    key_blk = np.arange(n_pages * page) // L_SEL
    e = (np.arange(nsp)[:, None] == key_blk[None, :]).astype(np.float32)
    et = np.zeros((nsp, nk), np.float32)
    et[n_sel - 1, :] = 1.0
    return pl.pallas_call(
        functools.partial(_sample_sel_body, t_new, b, n_c),
        grid_spec=pltpu.PrefetchScalarGridSpec(
            num_scalar_prefetch=1,
            grid=(b, n_c + 1),
            in_specs=[pl.BlockSpec(memory_space=pl.ANY),
                      pl.BlockSpec((1, rows, KV_DIM), lambda bi, c, pt: (bi, 0, 0)),
                      pl.BlockSpec((1, rows, nsp), lambda bi, c, pt: (bi, 0, 0)),
                      pl.BlockSpec((nsp, kc), lambda bi, c, pt: (0, jnp.minimum(c, n_c - 1))),
                      _const_spec((nsp, nk)),
                      pl.BlockSpec((1, nk, 2 * KV_DIM), lambda bi, c, pt: (bi, 0, 0))],
            out_specs=pl.BlockSpec((1, rows, KV_DIM), lambda bi, c, pt: (bi, 0, 0)),
            scratch_shapes=[pltpu.VMEM((2, pg, 2, KV_DIM, page), F32), pltpu.SemaphoreType.DMA((2,)),
                            pltpu.VMEM((rows, 1), F32), pltpu.VMEM((rows, 1), F32),
                            pltpu.VMEM((rows, KV_DIM), F32)]),
        out_shape=jax.ShapeDtypeStruct((b, rows, KV_DIM), F32),
        compiler_params=_params(("arbitrary", "arbitrary")),
        name="sample_sel",
    )(page_table, cache_fm, q_bd, sel01, jnp.asarray(e, dtype=BF16), jnp.asarray(et, dtype=BF16), tail_sel)


def _sample_win_body(t_new, q_ref, kv_ref, tail_ref, o_ref):
    rows = q_ref.shape[1]
    wb = kv_ref.shape[3]
    nk = tail_ref.shape[1]
    q = q_ref[0]
    tq = lax.broadcasted_iota(jnp.int32, (rows, 1), 0) & (t_new - 1)
    tail = tail_ref[0]
    s_old = jnp.where(lax.broadcasted_iota(jnp.int32, (1, wb), 1) > tq - WINDOW + wb,
                      _dot(q, kv_ref[0, 0].astype(BF16)), NEG)
    s_new = jnp.where(lax.broadcasted_iota(jnp.int32, (1, nk), 1) <= tq,
                      _dot_nt(q, tail[:, :KV_DIM].astype(BF16)), NEG)
    m = jnp.maximum(jnp.max(s_old, axis=-1, keepdims=True), jnp.max(s_new, axis=-1, keepdims=True))
    p_old = jnp.exp(s_old - m)
    p_new = jnp.exp(s_new - m)
    den = jnp.sum(p_old, axis=-1, keepdims=True) + jnp.sum(p_new, axis=-1, keepdims=True)
    o = _dot_nt(p_old.astype(BF16), kv_ref[0, 1].astype(BF16)) + _dot(p_new.astype(BF16), tail[:, KV_DIM:].astype(BF16))
    o_ref[0] = o / den


def _sample_win(q_bd, win_fm, tail_win, t_new):
    b, rows, _ = q_bd.shape
    wb = win_fm.shape[3]
    nk = tail_win.shape[1]
    return pl.pallas_call(
        functools.partial(_sample_win_body, t_new),
        grid=(b,),
        in_specs=[pl.BlockSpec((1, rows, KV_DIM), lambda bi: (bi, 0, 0)),
                  pl.BlockSpec((1, 2, KV_DIM, wb), lambda bi: (bi, 0, 0, 0)),
                  pl.BlockSpec((1, nk, 2 * KV_DIM), lambda bi: (bi, 0, 0))],
        out_specs=pl.BlockSpec((1, rows, KV_DIM), lambda bi: (bi, 0, 0)),
        out_shape=jax.ShapeDtypeStruct((b, rows, KV_DIM), F32),
        compiler_params=_params(("parallel",)),
        name="sample_win",
    )(q_bd, win_fm, tail_win)


def _heads_major(a2d, b, t, n):
    return a2d.reshape(b, t, n, HEAD_DIM).transpose(0, 2, 1, 3)


def _tokens_major(a_hm):
    b, n, t, d = a_hm.shape
    return a_hm.transpose(0, 2, 1, 3).reshape(b * t, n * d)


def _block_diag_q(q2d, b, t):
    q_hm = _heads_major(q2d, b, t, N_HEADS)
    onehot = jnp.asarray(np.eye(N_KV)[np.arange(N_HEADS) // N_REP], dtype=q2d.dtype)
    return jnp.einsum('bhtd,hg->bhtgd', q_hm, onehot).reshape(b, N_HEADS * t, KV_DIM)


def _own_group(o_bd, b, t):
    o = o_bd.reshape(b, N_KV, N_REP, t, N_KV, HEAD_DIM)
    o = jnp.stack([o[:, g, :, :, g, :] for g in range(N_KV)], axis=1)
    return o.transpose(0, 3, 1, 2, 4).reshape(b * t, Q_DIM)


def _nsa_prompt(x2d, b, t, g0, g1, w_in_bf, w_out_bf, cw):
    pos = jnp.arange(t)
    cmp_rows, q_t, qr_t, v_t, gates_t, cmp_fm, sel_fm, win_fm, k_sel, k_win = _nsa_proj(x2d, g0, w_in_bf, pos)
    assert t % STRIDE == 0 and t % L_SEL == 0
    n_ch = t // STRIDE
    n_cmp = n_ch - 1
    n_sel = t // L_SEL
    nsp = -(-n_sel // HEAD_DIM) * HEAD_DIM
    x_last = jnp.zeros((b, 8, _CHUNK_W), F32)
    (wk, pek, w1k, w2k), (wv, pev, w1v, w2v) = cw
    kcc, vcc = _compress(cmp_rows.reshape(b, n_ch, _CHUNK_W), x_last, wk, wv, pek, pev, w1k, w1v, w2k, w2v)
    gm = lambda a: a.reshape(b, -1, N_KV, HEAD_DIM).transpose(0, 2, 1, 3).astype(BF16)
    ocg_t, bias_t = _cmp_topk(q_t, gm(kcc), vcc.transpose(0, 2, 1).astype(BF16), gates_t, b, n_cmp, n_sel, nsp)
    x1 = _sel_win_out(qr_t, bias_t, k_sel, k_win, v_t, ocg_t, gates_t, w_out_bf, x2d, g1, b)
    rows5 = lambda a: a.reshape(b, 2, N_KV, HEAD_DIM, -1).transpose(0, 4, 1, 2, 3)[None]
    n_win = min(WINDOW, t)
    return x1, (rows5(cmp_fm), rows5(sel_fm), rows5(win_fm[:, :, t - n_win:]))


def _nsa_sample(x2d, b, t, g0, w_in_bf, cw, cache_cmp_l, cache_sel_l, cache_win_l, page_table):
    n_pages = page_table.shape[1]
    page = cache_cmp_l.shape[1]
    past_len = n_pages * page
    assert page % L_SEL == 0 and page % STRIDE == 0 and t <= STRIDE and t & (t - 1) == 0
    pos = past_len + jnp.arange(t)
    q, qr, cmp_rows, sel_rows, win_rows, gates = _nsa_proj(x2d, g0, w_in_bf, pos)
    row_w = 2 * KV_DIM
    fm = lambda a: a.transpose(0, 2, 3, 4, 1).reshape(a.shape[0], 2, KV_DIM, a.shape[1])
    n_past_ch = past_len // STRIDE
    n_cmp = n_past_ch
    n_sel = past_len // L_SEL + 1
    nsp = -(-n_sel // LANES) * LANES
    new3 = lambda a: a.reshape(b, t, row_w)
    x_last = jnp.pad(new3(cmp_rows), ((0, 0), (0, STRIDE - t), (0, 0))).reshape(b, 1, _CHUNK_W)
    x_last = jnp.pad(x_last, ((0, 0), (0, 7), (0, 0)))
    (wk, pek, w1k, w2k), (wv, pev, w1v, w2v) = cw
    kcc, vcc = _compress_paged(page_table, fm(cache_cmp_l), x_last, wk, wv, pek, pev, w1k, w1v, w2k, w2v)
    o_cmp_bd, sel01 = _sample_cmp(_block_diag_q(q, b, t), kcc, vcc, n_cmp, n_sel, nsp, past_len, t)
    qr_bd = _block_diag_q(qr, b, t)
    tail = lambda a: jnp.pad(new3(a), ((0, 0), (0, LANES - t), (0, 0)))
    o_sel_bd = _sample_sel(page_table, fm(cache_sel_l), qr_bd, sel01, tail(sel_rows), n_sel, t)
    win_fm = fm(cache_win_l)
    o_win_bd = _sample_win(qr_bd, win_fm, tail(win_rows), t)
    w_buf = win_fm.shape[3]
    new_fm = new3(win_rows).reshape(b, t, 2, KV_DIM).transpose(0, 2, 3, 1)
    new_win = jnp.concatenate([win_fm, new_fm], axis=3)[..., -w_buf:]
    new_win = new_win.reshape(b, 2, N_KV, HEAD_DIM, w_buf).transpose(0, 4, 1, 2, 3)
    rows5 = lambda a: a.reshape(1, b, -1, 2, N_KV, HEAD_DIM)
    caches = (rows5(cmp_rows), rows5(sel_rows), new_win[None])
    return _own_group(o_cmp_bd, b, t), _own_group(o_sel_bd, b, t), _own_group(o_win_bd, b, t), gates, caches


def kernel(x_prompt, x_sample, cache_cmp, cache_sel, cache_win, state_conv, page_table, p_prompt, p_sample,
           norm_g, w_ff1, w_ff2, w_ple, w_ple_gate, b_ple_gate,
           conv_w_pw1, conv_b_pw1, conv_w_dw, conv_b_dw, conv_ln_g, conv_ln_b, conv_w_pw2, conv_b_pw2,
           nsa_w_in, nsa_w_out, nsa_cmp_pe, nsa_cmp_w1, nsa_cmp_w2):
    depth = norm_g.shape[0]
    bf = lambda a: a.astype(BF16)
    row = lambda a: a.reshape(1, -1)
    w_ff1_bf, w_ff2_bf, w_ple_bf, w_gate_bf = bf(w_ff1), bf(w_ff2), bf(w_ple), bf(w_ple_gate)
    w_pw1_bf, w_pw2_bf, w_out_bf = bf(conv_w_pw1), bf(conv_w_pw2), bf(nsa_w_out)
    w_in_bf = bf(jnp.pad(nsa_w_in, ((0, 0), (0, 0), (0, _PROJ_W - nsa_w_in.shape[2]))))
    w_dw = jnp.pad(conv_w_dw, ((0, 0), (0, CONV_HALO - CONV_W), (0, 0)))

    def run(x, p, sample):
        b, t, _ = x.shape
        n = b * t
        x2d = x.reshape(n, D_MODEL)
        cmp_o, sel_o, win_o, conv_o = [], [], [], []
        for i in range(depth):
            g = lambda j: row(norm_g[i, j])
            if i % 2 == 0:
                c = i // 2
                u = _conv_front(x2d, g(0), w_pw1_bf[c], row(conv_b_pw1[c])).reshape(b, t, D_MODEL)
                if sample:
                    hist = jnp.pad(state_conv[c], ((0, 0), (CONV_HALO - (CONV_W - 1), 0), (0, 0)))
                else:
                    hist = jnp.zeros((b, CONV_HALO, D_MODEL), F32)
                tp = -(-t // CONV_HALO) * CONV_HALO
                pad_t = lambda a: jnp.pad(a, ((0, 0), (0, tp - t), (0, 0))) if tp > t else a
                x1 = _conv_back(hist, pad_t(u), pad_t(x2d.reshape(b, t, D_MODEL)), w_dw[c], row(conv_b_dw[c]),
                                row(conv_ln_g[c]), row(conv_ln_b[c]), w_pw2_bf[c], row(conv_b_pw2[c]),
                                g(1))[:, :t].reshape(n, D_MODEL)
                keep = CONV_W - 1
                conv_o.append(jnp.concatenate([hist, u], axis=1)[:, CONV_HALO + t - keep:][None])
            else:
                a = i // 2
                cw = _compress_weights(nsa_cmp_pe[a], nsa_cmp_w1[a], nsa_cmp_w2[a])
                if sample:
                    oc, osel, ow, gates, caches = _nsa_sample(x2d, b, t, g(0), w_in_bf[a], cw, cache_cmp[a],
                                                              cache_sel[a], cache_win[a], page_table)
                    x1 = _nsa_out(oc, osel, ow, gates, w_out_bf[a], x2d, g(1))
                else:
                    x1, caches = _nsa_prompt(x2d, b, t, g(0), g(1), w_in_bf[a], w_out_bf[a], cw)
                for dst, rows_ in zip((cmp_o, sel_o, win_o), caches):
                    dst.append(rows_)
            x2d = _ffn(x1, g(2), w_ff1_bf[i], w_ff2_bf[i], g(3), w_gate_bf[i], row(b_ple_gate[i]),
                       p[i].reshape(n, -1), w_ple_bf[i])
        cat = lambda parts: jnp.concatenate(parts, axis=0)
        return x2d.reshape(b, t, D_MODEL), cat(cmp_o), cat(sel_o), cat(win_o), cat(conv_o)

    y_p, cmp_p, sel_p, win_p, conv_p = run(x_prompt, p_prompt, False)
    y_s, cmp_s, sel_s, win_s, conv_s = run(x_sample, p_sample, True)
    return (y_p, y_s, cmp_p, cmp_s, sel_p, sel_s, win_p, win_s, conv_p, conv_s)
```

```python
import functools
import math

import numpy as np
import jax
import jax.numpy as jnp
from jax import lax
from jax.experimental import pallas as pl
from jax.experimental.pallas import tpu as pltpu

F32 = jnp.float32
BF16 = jnp.bfloat16

D_MODEL = 1024
N_HEADS = 16
N_KV = 4
N_REP = N_HEADS // N_KV
HEAD_DIM = 64
ROT_DIM = HEAD_DIM // 4
ROPE_THETA = 500000.0
L_CMP = 32
STRIDE = 16
CMP_HID = 2 * HEAD_DIM
L_SEL = 64
N_TOP = 16
WINDOW = 512
CONV_W = 31
Q_DIM = N_HEADS * HEAD_DIM
KV_DIM = N_KV * HEAD_DIM
GATE_DIM = 3 * N_HEADS
D_FF = 4 * D_MODEL
EPS = 1e-6
NEG = -1e30
FORCE = 1e6
SCALE = HEAD_DIM ** -0.5

LANES = 128
CONV_HALO = 32
VMEM_LIMIT = 56 * 1024 * 1024


def _pick(n, cands):
    for c in cands:
        if n % c == 0:
            return c
    raise ValueError(f"no tile in {cands} divides {n}")


def _const_spec(shape):
    nd = len(shape)
    return pl.BlockSpec(shape, lambda *_: (0,) * nd, pipeline_mode=pl.Buffered(1))


def _params(sem):
    return pltpu.CompilerParams(dimension_semantics=sem, vmem_limit_bytes=VMEM_LIMIT)


def _sigmoid(x):
    return 1.0 / (1.0 + jnp.exp(-x))


def _rms(x, g):
    return x * lax.rsqrt(jnp.mean(x * x, axis=-1, keepdims=True) + EPS) * g


def _dot(a, b):
    return jnp.dot(a, b, preferred_element_type=F32)


def _dot_nt(a, b):
    return lax.dot_general(a, b, (((1,), (1,)), ((), ())), preferred_element_type=F32)


def _split_bf16(x):
    hi = x.astype(BF16)
    lo = (x - hi.astype(F32)).astype(BF16)
    return hi, lo


def _conv_front_body(x_ref, g_ref, w_ref, b_ref, u_ref):
    h = _rms(x_ref[...], g_ref[...]).astype(BF16)
    z = _dot(h, w_ref[...]) + b_ref[...]
    u_ref[...] = z[:, :D_MODEL] * _sigmoid(z[:, D_MODEL:])


def _conv_front(x2d, g, w_bf, b):
    n = x2d.shape[0]
    tm = _pick(n, (512, 256, 128, 64, 32, 16, 8))
    return pl.pallas_call(
        _conv_front_body,
        grid=(n // tm,),
        in_specs=[pl.BlockSpec((tm, D_MODEL), lambda i: (i, 0)),
                  _const_spec((1, D_MODEL)),
                  _const_spec((D_MODEL, 2 * D_MODEL)),
                  _const_spec((1, 2 * D_MODEL))],
        out_specs=pl.BlockSpec((tm, D_MODEL), lambda i: (i, 0)),
        out_shape=jax.ShapeDtypeStruct((n, D_MODEL), F32),
        compiler_params=_params(("parallel",)),
        name="conv_front",
    )(x2d, g, w_bf, b)


_CONV_ROWS = 32
_CONV_COLS = 256


def _conv_back_body(hist_ref, prev_ref, main_ref, wdw_ref, bdw_ref, lng_ref, lnb_ref, w2_ref, b2_ref, x_ref, g1_ref,
                    o_ref, win_ref, y_ref):
    tt = main_ref.shape[1]
    @pl.when(pl.program_id(1) == 0)
    def _():
        win_ref[0:CONV_HALO, :] = hist_ref[0]

    @pl.when(pl.program_id(1) > 0)
    def _():
        win_ref[0:CONV_HALO, :] = prev_ref[0]

    win_ref[CONV_HALO:CONV_HALO + tt, :] = main_ref[0]
    first = CONV_HALO - (CONV_W - 1)

    sub = 8
    for r0 in range(0, tt, _CONV_ROWS):
        for c0 in range(0, D_MODEL, _CONV_COLS):
            cols = slice(c0, c0 + _CONV_COLS)
            acc = jnp.zeros((_CONV_ROWS, _CONV_COLS), F32)
            for s in range(sub):
                part = None
                for k in range(CONV_W):
                    if (first + k) % sub != s:
                        continue
                    base = r0 + first + k - s
                    term = win_ref[base:base + _CONV_ROWS + (sub if s else 0), cols] * wdw_ref[k:k + 1, cols]
                    part = term if part is None else part + term
                if part is not None:
                    acc = acc + part[s:s + _CONV_ROWS]
            y_ref[r0:r0 + _CONV_ROWS, cols] = acc + bdw_ref[:, cols]
    y = y_ref[...]
    yc = y - jnp.mean(y, axis=-1, keepdims=True)
    var = jnp.mean(yc * yc, axis=-1, keepdims=True)
    ln = yc * lax.rsqrt(var + EPS) * lng_ref[...] + lnb_ref[...]
    act = (ln * _sigmoid(ln)).astype(BF16)
    m = _dot(act, w2_ref[...]) + b2_ref[...]
    o_ref[0] = x_ref[0] + _rms(m, g1_ref[...])


def _conv_back(hist, u3d, x3d, wdw, bdw, lng, lnb, w2_bf, b2, g1):
    b, t, _ = x3d.shape
    tt = _pick(t, (256, 128, 64, 32))
    halo_blocks = tt // CONV_HALO
    return pl.pallas_call(
        _conv_back_body,
        grid=(b, t // tt),
        in_specs=[pl.BlockSpec((1, CONV_HALO, D_MODEL), lambda bi, i: (bi, 0, 0)),
                  pl.BlockSpec((1, CONV_HALO, D_MODEL), lambda bi, i: (bi, jnp.maximum(i * halo_blocks - 1, 0), 0)),
                  pl.BlockSpec((1, tt, D_MODEL), lambda bi, i: (bi, i, 0)),
                  _const_spec((CONV_HALO, D_MODEL)),
                  _const_spec((1, D_MODEL)), _const_spec((1, D_MODEL)), _const_spec((1, D_MODEL)),
                  _const_spec((D_MODEL, D_MODEL)), _const_spec((1, D_MODEL)),
                  pl.BlockSpec((1, tt, D_MODEL), lambda bi, i: (bi, i, 0)),
                  _const_spec((1, D_MODEL))],
        out_specs=pl.BlockSpec((1, tt, D_MODEL), lambda bi, i: (bi, i, 0)),
        out_shape=jax.ShapeDtypeStruct((b, t, D_MODEL), F32),
        scratch_shapes=[pltpu.VMEM((tt + CONV_HALO, D_MODEL), F32), pltpu.VMEM((tt, D_MODEL), F32)],
        compiler_params=_params(("parallel", "parallel")),
        name="conv_back",
    )(hist, u3d, u3d, wdw, bdw, lng, lnb, w2_bf, b2, x3d, g1)


_FF_CHUNK = 1024


def _ffn_body(x_ref, g2_ref, w1_ref, w2_ref, g3_ref, wg_ref, bg_ref, p_ref, wp_ref, o_ref):
    x = x_ref[...]
    h = _rms(x, g2_ref[...]).astype(BF16)
    f = jnp.zeros(x.shape, F32)
    for c in range(D_FF // _FF_CHUNK):
        a = jnp.maximum(_dot(h, w1_ref[:, c * _FF_CHUNK:(c + 1) * _FF_CHUNK]), 0.0)
        f = f + _dot((a * a).astype(BF16), w2_ref[c * _FF_CHUNK:(c + 1) * _FF_CHUNK, :])
    x2 = x + _rms(f, g3_ref[...])
    gate = _sigmoid(_dot(x2.astype(BF16), wg_ref[...]) + bg_ref[...])
    o_ref[...] = x2 + gate * _dot(p_ref[...].astype(BF16), wp_ref[...])


def _ffn(x2d, g2, w1_bf, w2_bf, g3, wg_bf, bg, p2d, wp_bf):
    n = x2d.shape[0]
    d_ple = p2d.shape[1]
    tm = _pick(n, (512, 256, 128, 64, 32, 16, 8))
    return pl.pallas_call(
        _ffn_body,
        grid=(n // tm,),
        in_specs=[pl.BlockSpec((tm, D_MODEL), lambda i: (i, 0)),
                  _const_spec((1, D_MODEL)),
                  _const_spec((D_MODEL, D_FF)), _const_spec((D_FF, D_MODEL)),
                  _const_spec((1, D_MODEL)),
                  _const_spec((D_MODEL, D_MODEL)), _const_spec((1, D_MODEL)),
                  pl.BlockSpec((tm, d_ple), lambda i: (i, 0)),
                  _const_spec((d_ple, D_MODEL))],
        out_specs=pl.BlockSpec((tm, D_MODEL), lambda i: (i, 0)),
        out_shape=jax.ShapeDtypeStruct((n, D_MODEL), F32),
        compiler_params=_params(("parallel",)),
        name="ffn_ple",
    )(x2d, g2, w1_bf, w2_bf, g3, wg_bf, bg, p2d, wp_bf)


_PROJ_W = Q_DIM + 6 * KV_DIM + LANES
LOG2E = 1.4426950408889634
V_ROWS = HEAD_DIM + 16
_VT_ROWS = 2 * N_KV * V_ROWS


def _rope_block(blk, c, su, sd):
    return blk * c + pltpu.roll(blk, ROT_DIM // 2, 1) * su + pltpu.roll(blk, LANES - ROT_DIM // 2, 1) * sd


def _nsa_proj_body(prompt, x_ref, g_ref, w_ref, c_ref, su_ref, sd_ref, *refs):
    if prompt:
        blk_ref, qt_ref, qrt_ref, vt_ref, gatet_ref = refs[:5]
        cache_t_refs, key_refs = refs[5:8], refs[8:10]
        low = lax.broadcasted_iota(jnp.int32, (x_ref.shape[0], LANES), 1) < HEAD_DIM
        ones = jnp.ones((V_ROWS - HEAD_DIM, x_ref.shape[0]), BF16)
    else:
        q_ref, qr_ref, cmp_ref, sel_ref, win_ref, gate_ref = refs
    h = _rms(x_ref[...], g_ref[...]).astype(BF16)
    z = _dot(h, w_ref[...])
    c, su, sd = c_ref[...], su_ref[...], sd_ref[...]
    for i in range(Q_DIM // LANES):
        cols = slice(i * LANES, (i + 1) * LANES)
        qs = z[:, cols] * SCALE
        qrs = _rope_block(z[:, cols], c, su, sd) * SCALE
        if prompt:
            qt_ref[0, cols, :] = (qs * LOG2E).T.astype(BF16)
            qrt_ref[0, cols, :] = (qrs * LOG2E).T.astype(BF16)
        else:
            q_ref[:, cols] = qs.astype(BF16)
            qr_ref[:, cols] = qrs.astype(BF16)
    o = Q_DIM
    if prompt:
        for i in range(2 * KV_DIM // LANES):
            cache_t_refs[0][0, i * LANES:(i + 1) * LANES, :] = z[:, o + i * LANES:o + (i + 1) * LANES].T
    else:
        cmp_ref[...] = z[:, o:o + 2 * KV_DIM]
    o += 2 * KV_DIM
    for n in range(2):
        for i in range(KV_DIM // LANES):
            k_rot = _rope_block(z[:, o + i * LANES:o + (i + 1) * LANES], c, su, sd)
            if not prompt:
                (sel_ref, win_ref)[n][:, i * LANES:(i + 1) * LANES] = k_rot
                continue
            vt = z[:, o + KV_DIM + i * LANES:o + KV_DIM + (i + 1) * LANES].T
            cache_t_refs[1 + n][0, i * LANES:(i + 1) * LANES, :] = k_rot.T
            cache_t_refs[1 + n][0, KV_DIM + i * LANES:KV_DIM + (i + 1) * LANES, :] = vt
            fill = blk_ref[...] if n == 0 else 0.0
            per_tile = LANES // HEAD_DIM
            key_refs[n][per_tile * i] = jnp.where(low, k_rot, fill).astype(BF16)
            key_refs[n][per_tile * i + 1] = jnp.where(low, pltpu.roll(k_rot, HEAD_DIM, 1), fill).astype(BF16)
            for k in range(per_tile):
                base = (n * N_KV + i * per_tile + k) * V_ROWS
                vt_ref[0, base:base + HEAD_DIM, :] = vt[k * HEAD_DIM:(k + 1) * HEAD_DIM].astype(BF16)
                vt_ref[0, base + HEAD_DIM:base + V_ROWS, :] = ones
        if not prompt:
            (sel_ref, win_ref)[n][:, KV_DIM:] = z[:, o + KV_DIM:o + 2 * KV_DIM]
        o += 2 * KV_DIM
    gates = _sigmoid(z[:, o:o + LANES])
    if prompt:
        gatet_ref[0] = gates.T
    else:
        gate_ref[...] = gates


def _rope_tables(pos, rows):
    half = ROT_DIM // 2
    inv = jnp.float32(ROPE_THETA) ** (-jnp.arange(half, dtype=F32) * (2.0 / ROT_DIM))
    ang = pos.astype(F32)[:, None] * inv[None, :]
    lane = np.arange(LANES)
    within = lane % HEAD_DIM
    cos = jnp.cos(ang)[:, lane % half]
    sin = jnp.sin(ang)[:, lane % half]
    c = jnp.where(within[None, :] < ROT_DIM, cos, 1.0)
    su = jnp.where((within[None, :] >= half) & (within[None, :] < ROT_DIM), sin, 0.0)
    sd = jnp.where(within[None, :] < half, -sin, 0.0)
    reps = rows // pos.shape[0]
    return tuple(jnp.tile(a, (reps, 1)) for a in (c, su, sd))


def _nsa_proj(x2d, g, w_bf, pos):
    n = x2d.shape[0]
    t = pos.shape[0]
    tm = _pick(n, (256, 128, 64, 32, 16, 8))
    if t >= tm:
        assert t % tm == 0
        tabs = _rope_tables(pos, t)
        nt = t // tm
        tab_spec = pl.BlockSpec((tm, LANES), lambda i: (i % nt, 0))
    else:
        assert tm % t == 0
        tabs = _rope_tables(pos, tm)
        tab_spec = _const_spec((tm, LANES))
    row = lambda w: pl.BlockSpec((tm, w), lambda i: (i, 0))
    rows_of = lambda w, dt: jax.ShapeDtypeStruct((n, w), dt)
    in_specs = [row(D_MODEL), _const_spec((1, D_MODEL)), _const_spec((D_MODEL, _PROJ_W)),
                tab_spec, tab_spec, tab_spec]
    args = [x2d, g, w_bf, *tabs]
    prompt = t >= tm
    if prompt:
        assert t // L_SEL <= HEAD_DIM
        blk = np.zeros((t, LANES), np.float32)
        blk[np.arange(t), HEAD_DIM + np.arange(t) // L_SEL] = 1.0
        in_specs.append(tab_spec)
        args.append(jnp.asarray(blk))
        tile = lambda rows: pl.BlockSpec((1, rows, tm), lambda i: (i, 0, 0))
        tiles = lambda rows, dt: jax.ShapeDtypeStruct((n // tm, rows, tm), dt)
        out_specs = [tile(Q_DIM), tile(Q_DIM), tile(_VT_ROWS), tile(LANES)]
        out_shape = [tiles(Q_DIM, BF16), tiles(Q_DIM, BF16), tiles(_VT_ROWS, BF16), tiles(LANES, F32)]
        out_specs += [pl.BlockSpec((1, 2 * KV_DIM, tm), lambda i: (i // nt, 0, i % nt))] * 3
        out_shape += [jax.ShapeDtypeStruct((n // t, 2 * KV_DIM, t), F32)] * 3
        out_specs += [pl.BlockSpec((N_KV, tm, LANES), lambda i: (0, i, 0))] * 2
        out_shape += [jax.ShapeDtypeStruct((N_KV, n, LANES), BF16)] * 2
    else:
        out_specs = [row(Q_DIM), row(Q_DIM), row(2 * KV_DIM), row(2 * KV_DIM), row(2 * KV_DIM), row(LANES)]
        out_shape = [rows_of(Q_DIM, BF16), rows_of(Q_DIM, BF16), rows_of(2 * KV_DIM, F32),
                     rows_of(2 * KV_DIM, F32), rows_of(2 * KV_DIM, F32), rows_of(LANES, F32)]
    return pl.pallas_call(
        functools.partial(_nsa_proj_body, prompt),
        grid=(n // tm,),
        in_specs=in_specs,
        out_specs=out_specs,
        out_shape=out_shape,
        compiler_params=_params(("parallel",)),
        name="nsa_proj",
    )(*args)


_CHUNK_W = STRIDE * 2 * KV_DIM
_AB_W = 2 * N_KV * CMP_HID


_S_PER_DOT = 4


def _first_layer(get_tile, w_ref, rows):
    half = LANES // 2
    low = lax.broadcasted_iota(jnp.int32, (rows, LANES), 1) < half
    out = []
    for j in range(KV_DIM // LANES):
        acc = [jnp.zeros((rows, 2 * CMP_HID), F32) for _ in range(2)]
        for sq in range(STRIDE // _S_PER_DOT):
            even, odd = [], []
            for s in range(sq * _S_PER_DOT, (sq + 1) * _S_PER_DOT, 2):
                a, b = get_tile(s, j), get_tile(s + 1, j)
                even.append(jnp.where(low, a, pltpu.roll(b, half, 1)))
                odd.append(jnp.where(low, pltpu.roll(a, half, 1), b))
            acc[0] = acc[0] + _dot(jnp.concatenate(even, axis=1).astype(BF16), w_ref[sq])
            acc[1] = acc[1] + _dot(jnp.concatenate(odd, axis=1).astype(BF16), w_ref[sq])
        out += acc
    return out


def _store_ab(ab_ref, rows_slice, accs):
    for g, acc in enumerate(accs):
        ab_ref[rows_slice, g * CMP_HID:(g + 1) * CMP_HID] = acc[:, :CMP_HID]
        ab_ref[rows_slice, (N_KV + g) * CMP_HID:(N_KV + g + 1) * CMP_HID] = acc[:, CMP_HID:]


def _chunk_rows_tile(src_ref, kv):
    def get(s, j):
        lo = s * 2 * KV_DIM + kv * KV_DIM + j * LANES
        return src_ref[0, :, lo:lo + LANES]
    return get


def _compress_finish(xl_ref, per_kv, n_rows):
    half = _AB_W // 2
    for kv, (ab_ref, w_ref, pe_ref, w1_ref, w2_ref, out_ref) in enumerate(per_kv):
        _store_ab(ab_ref, slice(n_rows, n_rows + 8), _first_layer(_chunk_rows_tile(xl_ref, kv), w_ref, 8))
        c = jnp.sum(pe_ref[...] * w1_ref[...], axis=0, keepdims=True)
        c = jnp.concatenate([c] * N_KV, axis=1)
        pre = ab_ref[0:n_rows, 0:half] + ab_ref[1:n_rows + 1, half:] + c
        hid = (pre * _sigmoid(pre)).astype(BF16)
        out_ref[0] = _dot(hid, w2_ref[...])


def _pages_first_layer(page_of, n_pages, perm_ref, xs_ref, ab_ref, w_ref, r0):
    per_page = perm_ref.shape[0] // STRIDE
    rows = n_pages * per_page
    for p in range(n_pages):
        t = _dot_nt(perm_ref[...], page_of(p).astype(BF16))
        for s in range(STRIDE):
            xs_ref[s, p * per_page:(p + 1) * per_page, :] = t[s * per_page:(s + 1) * per_page]
    tile = lambda s, j: xs_ref[s, :, j * LANES:(j + 1) * LANES]
    _store_ab(ab_ref, pl.ds(r0, rows), _first_layer(tile, w_ref, rows))


def _chunk_sort_matrix(page):
    per_page = page // STRIDE
    assert per_page % 8 == 0
    tok = np.arange(page)
    perm = np.zeros((page, page), np.float32)
    perm[(tok % STRIDE) * per_page + tok // STRIDE, tok] = 1.0
    return jnp.asarray(perm, dtype=BF16)


def _compress_body(perm_ref, x_ref, xl_ref, wk_ref, wv_ref, pek_ref, pev_ref, w1k_ref, w1v_ref, w2k_ref, w2v_ref,
                   kcc_ref, vcc_ref, xs_ref, abk_ref, abv_ref):
    page = perm_ref.shape[0]
    n_pages = x_ref.shape[2] // page
    c = pl.program_id(1)
    rows = n_pages * page // STRIDE
    r0 = pl.multiple_of(c * rows, rows)
    for kv, (ab_ref, w_ref) in enumerate(((abk_ref, wk_ref), (abv_ref, wv_ref))):
        page_of = lambda p: x_ref[0, kv * KV_DIM:(kv + 1) * KV_DIM, p * page:(p + 1) * page]
        _pages_first_layer(page_of, n_pages, perm_ref, xs_ref, ab_ref, w_ref, r0)

    @pl.when(c == pl.num_programs(1) - 1)
    def _():
        _compress_finish(xl_ref, ((abk_ref, wk_ref, pek_ref, w1k_ref, w2k_ref, kcc_ref),
                                  (abv_ref, wv_ref, pev_ref, w1v_ref, w2v_ref, vcc_ref)), kcc_ref.shape[1])


def _compress(x_fm, x_last, wk, wv, pek, pev, w1k, w1v, w2k, w2v):
    b, _, t = x_fm.shape
    page = LANES
    tb = _pick(t, (2048, 1024, 512, 256, 128))
    n = t // STRIDE
    flat = L_CMP * HEAD_DIM
    out = jax.ShapeDtypeStruct((b, n, KV_DIM), F32)
    return pl.pallas_call(
        _compress_body,
        grid=(b, t // tb),
        in_specs=[_const_spec((page, page)),
                  pl.BlockSpec((1, 2 * KV_DIM, tb), lambda bi, i: (bi, 0, i)),
                  pl.BlockSpec((1, 8, _CHUNK_W), lambda bi, i: (bi, 0, 0)),
                  _const_spec(_W1_SHAPE), _const_spec(_W1_SHAPE),
                  _const_spec((flat, 1)), _const_spec((flat, 1)),
                  _const_spec((flat, CMP_HID)), _const_spec((flat, CMP_HID)),
                  _const_spec((N_KV * CMP_HID, KV_DIM)), _const_spec((N_KV * CMP_HID, KV_DIM))],
        out_specs=[pl.BlockSpec((1, n, KV_DIM), lambda bi, i: (bi, 0, 0))] * 2,
        out_shape=[out, out],
        scratch_shapes=[pltpu.VMEM((STRIDE, tb // STRIDE, KV_DIM), F32),
                        pltpu.VMEM((n + 8, _AB_W), F32), pltpu.VMEM((n + 8, _AB_W), F32)],
        compiler_params=_params(("parallel", "arbitrary")),
        name="compress",
    )(_chunk_sort_matrix(page), x_fm, x_last, wk, wv, pek, pev, w1k, w1v, w2k, w2v)


_W1_SHAPE = (STRIDE // _S_PER_DOT, _S_PER_DOT * HEAD_DIM, 2 * CMP_HID)


def _compress_weights(cmp_pe, cmp_w1, cmp_w2):
    eye = jnp.eye(N_KV, dtype=F32)
    outs = []
    for kv in range(2):
        w1 = cmp_w1[kv].reshape(2, STRIDE // _S_PER_DOT, _S_PER_DOT, HEAD_DIM, CMP_HID)
        stacked = w1.transpose(1, 2, 3, 0, 4).reshape(_W1_SHAPE)
        w2 = jnp.einsum('gk,hd->ghkd', eye, cmp_w2[kv]).reshape(N_KV * CMP_HID, KV_DIM)
        outs.append((stacked.astype(BF16), cmp_pe[kv].reshape(L_CMP * HEAD_DIM, 1),
                     cmp_w1[kv].reshape(L_CMP * HEAD_DIM, CMP_HID), w2.astype(BF16)))
    return outs


def _select_blocks(imp_t, blk, qpos, n_real, k):
    cur = jnp.right_shift(qpos, int(math.log2(L_SEL)))
    forced = (blk == 0) | (blk == cur) | (blk == cur - 1)
    future = blk * L_SEL > qpos
    score = jnp.where(future, -jnp.inf, jnp.where(forced, FORCE, imp_t))
    score = jnp.where(blk < n_real, score, -jnp.inf)
    sub = 8
    n_groups = -(-n_real // sub)
    rows = [score[i * sub:(i + 1) * sub] for i in range(n_groups)]
    local = lax.broadcasted_iota(jnp.int32, rows[0].shape, 0)
    rank = [jnp.zeros(r.shape, jnp.int32) for r in rows]
    for j in range(n_real):
        sj = score[j:j + 1, :]
        for n, r in enumerate(rows):
            if n * sub > j:
                beats = sj >= r
            elif n * sub + sub - 1 <= j:
                beats = sj > r
            else:
                beats = jnp.where(local > j - n * sub, jnp.where(sj >= r, 1, 0), jnp.where(sj > r, 1, 0)) > 0
            rank[n] = rank[n] + jnp.where(beats, 1, 0)
    pad = score.shape[0] - n_groups * sub
    if pad:
        rank.append(jnp.full((pad, score.shape[1]), k, jnp.int32))
    return (jnp.concatenate(rank, axis=0) < k) & (blk < n_real)


def _overlap_t(n_sel_pad, n_cmp_pad, n_sel, n_cmp):
    ci = np.arange(n_cmp_pad)[None, :] * STRIDE
    sj = np.arange(n_sel_pad)[:, None] * L_SEL
    ov = (ci < sj + L_SEL) & (ci + L_CMP > sj)
    ov &= (np.arange(n_cmp_pad)[None, :] < n_cmp) & (np.arange(n_sel_pad)[:, None] < n_sel)
    return jnp.asarray(ov, dtype=BF16)


def _head_rows(h):
    return slice(h * HEAD_DIM, (h + 1) * HEAD_DIM)


def _cmp_topk_body(n_sel, qt_ref, kcc_ref, vcct_ref, gt_ref, ovt_ref, ocg_ref, bias_ref):
    tq = qt_ref.shape[2]
    nc = kcc_ref.shape[2]
    nsp = ovt_ref.shape[0]
    t0 = pl.program_id(1) * tq
    wide = N_REP * tq
    qpos = t0 + (lax.broadcasted_iota(jnp.int32, (1, wide), 1) & (tq - 1))
    cmp_end = lax.broadcasted_iota(jnp.int32, (nc, 1), 0) * STRIDE + (L_CMP - 1)
    cmask = cmp_end <= qpos
    blk = lax.broadcasted_iota(jnp.int32, (nsp, tq), 0)
    qpos_t = t0 + lax.broadcasted_iota(jnp.int32, (nsp, tq), 1)
    ovt = ovt_ref[...]
    for g in range(N_KV):
        heads = range(g * N_REP, (g + 1) * N_REP)
        q = jnp.concatenate([qt_ref[0, _head_rows(h), :] for h in heads], axis=1)
        lm = jnp.where(cmask, _dot(kcc_ref[0, g], q), NEG)
        e = jnp.exp2(lm - jnp.max(lm, axis=0, keepdims=True))
        pc = jnp.where(cmask, e * (1.0 / jnp.sum(e, axis=0, keepdims=True)), 0.0)
        o = _dot(vcct_ref[0, _head_rows(g), :], pc.astype(BF16))
        pg = jnp.zeros((nc, tq), F32)
        for r, h in enumerate(heads):
            ocg_ref[0, _head_rows(h), :] = o[:, r * tq:(r + 1) * tq] * gt_ref[0, 3 * h:3 * h + 1, :]
            pg = pg + pc[:, r * tq:(r + 1) * tq]
        hi, lo = _split_bf16(pg)
        imp_t = _dot(ovt, hi) + _dot(ovt, lo)
        sel = _select_blocks(imp_t, blk, qpos_t, n_sel, min(N_TOP, n_sel))
        bias_ref[0, g] = jnp.where(sel, 0.0, NEG).astype(BF16)


def _cmp_topk(q_t, kcc_gm, vcc_t, gates_t, b, n_cmp, n_sel, nsp):
    tq = q_t.shape[2]
    n_t = q_t.shape[0] // b
    nc = kcc_gm.shape[2]
    ovt = _overlap_t(nsp, nc, n_sel, n_cmp)
    tile = lambda rows: pl.BlockSpec((1, rows, tq), lambda bi, i: (bi * n_t + i, 0, 0))
    return pl.pallas_call(
        functools.partial(_cmp_topk_body, n_sel),
        grid=(b, n_t),
        in_specs=[tile(Q_DIM),
                  pl.BlockSpec((1, N_KV, nc, HEAD_DIM), lambda bi, i: (bi, 0, 0, 0)),
                  pl.BlockSpec((1, KV_DIM, nc), lambda bi, i: (bi, 0, 0)),
                  tile(LANES), _const_spec((nsp, nc))],
        out_specs=[tile(Q_DIM), pl.BlockSpec((1, N_KV, nsp, tq), lambda bi, i: (bi, 0, 0, i))],
        out_shape=[jax.ShapeDtypeStruct((b * n_t, Q_DIM, tq), F32),
                   jax.ShapeDtypeStruct((b, N_KV, nsp, n_t * tq), BF16)],
        compiler_params=_params(("parallel", "parallel")),
        name="cmp_topk",
    )(q_t, kcc_gm, vcc_t, gates_t, ovt)


def _flash_init(m_ref, l_ref, acc_ref):
    m_ref[...] = jnp.full(m_ref.shape, NEG, F32)
    l_ref[...] = jnp.zeros(l_ref.shape, F32)
    acc_ref[...] = jnp.zeros(acc_ref.shape, F32)


def _flash_t(state, s, vt):
    m, acc = state
    m_new = jnp.maximum(m, jnp.max(s, axis=0, keepdims=True))
    return m_new, jnp.exp2(m - m_new) * acc + _dot(vt, jnp.exp2(s - m_new).astype(BF16))


def _v_rows(branch, g):
    base = (branch * N_KV + g) * V_ROWS
    return slice(base, base + V_ROWS)


def _sel_win_body(qrt_ref, bias_ref, kp_ref, kw_ref, vt_ref, ocg_ref, gt_ref, w_ref, x_ref, g1_ref,
                  o_ref, ot_ref):
    tq = qrt_ref.shape[2]
    kc = tq
    qt = pl.program_id(1)
    wide = N_REP * tq
    krow = lax.broadcasted_iota(jnp.int32, (kc, wide), 0)
    qcol = lax.broadcasted_iota(jnp.int32, (kc, wide), 1) & (tq - 1)
    n_back = WINDOW // kc
    fresh = (jnp.full((1, wide), NEG, F32), jnp.zeros((V_ROWS, wide), F32))

    for g in range(N_KV):
        heads = range(g * N_REP, (g + 1) * N_REP)
        qw = jnp.concatenate([qrt_ref[0, _head_rows(h), :] for h in heads], axis=1)
        qs = jnp.concatenate([qw, jnp.concatenate([bias_ref[0, g]] * N_REP, axis=1)], axis=0)
        qw = jnp.concatenate([qw, jnp.zeros_like(qw)], axis=0)

        def scores(j):
            return _dot(kp_ref[g, pl.ds(pl.multiple_of(j * kc, kc), kc), :], qs)

        def sel_chunk(j, state, diagonal):
            s = scores(j)
            return _flash_t(state, jnp.where(krow <= qcol, s, NEG) if diagonal else s, vt_ref[j, _v_rows(0, g), :])

        def sel_run(j0, n, state):
            ahead = [scores(j0), scores(j0 + 1)] if n > 1 else [scores(j0)]
            for k in range(n):
                s = ahead.pop(0)
                state = _flash_t(state, s, vt_ref[j0 + k, _v_rows(0, g), :])
                if k + 2 < n:
                    ahead.append(scores(j0 + k + 2))
            return state

        unroll = 4
        state = lax.fori_loop(0, qt // unroll, lambda i, st: sel_run(unroll * i, unroll, st), fresh)
        rest = qt % unroll
        done = qt - rest
        state = lax.cond(rest >= 2, lambda st: sel_run(done, 2, st), lambda st: st, state)
        state = lax.cond(rest % 2 == 1, lambda st: sel_chunk(qt - 1, st, False), lambda st: st, state)

        def win_scores(back):
            s = _dot(kw_ref[g, pl.ds(pl.multiple_of((qt - back) * kc, kc), kc), :], qw)
            if back == 0:
                return jnp.where(krow <= qcol, s, NEG)
            return jnp.where(krow > qcol, s, NEG) if back == n_back else s

        def win_seq(backs, state):
            for back, s in [(back, win_scores(back)) for back in backs]:
                state = _flash_t(state, s, vt_ref[qt - back, _v_rows(1, g), :])
            return state

        s_sel, s_win = scores(qt), win_scores(0)
        _, acc_sel = _flash_t(state, jnp.where(krow <= qcol, s_sel, NEG), vt_ref[qt, _v_rows(0, g), :])
        state = _flash_t(fresh, s_win, vt_ref[qt, _v_rows(1, g), :])
        older = lambda st: st
        for k in range(1, n_back + 1):
            older = functools.partial(lambda k, fewer, st: lax.cond(
                qt >= k, functools.partial(win_seq, range(1, k + 1)), fewer, st), k, older)
        _, acc_win = older(state)

        den = slice(HEAD_DIM, HEAD_DIM + 1)
        for r, h in enumerate(heads):
            cols = slice(r * tq, (r + 1) * tq)
            ot_ref[_head_rows(h), :] = (
                ocg_ref[0, _head_rows(h), :]
                + acc_sel[:HEAD_DIM, cols] * (gt_ref[0, 3 * h + 1:3 * h + 2, :] * (1.0 / acc_sel[den, cols]))
                + acc_win[:HEAD_DIM, cols] * (gt_ref[0, 3 * h + 2:3 * h + 3, :] * (1.0 / acc_win[den, cols])))

    m = _dot(ot_ref[...].T.astype(BF16), w_ref[...])
    o_ref[...] = x_ref[...] + _rms(m, g1_ref[...])


def _sel_win_out(qr_t, bias_t, k_sel, k_win, v_t, ocg_t, gates_t, w_out_bf, x2d, g1, b):
    tq = qr_t.shape[2]
    n_t = qr_t.shape[0] // b
    t = n_t * tq
    nsp = bias_t.shape[2]
    assert WINDOW % tq == 0 and HEAD_DIM + nsp == LANES
    tile = lambda rows: pl.BlockSpec((1, rows, tq), lambda bi, i: (bi * n_t + i, 0, 0))
    keys = pl.BlockSpec((N_KV, t, LANES), lambda bi, i: (0, bi, 0))
    xrow = pl.BlockSpec((tq, D_MODEL), lambda bi, i: (bi * n_t + i, 0))
    return pl.pallas_call(
        _sel_win_body,
        grid=(b, n_t),
        in_specs=[tile(Q_DIM),
                  pl.BlockSpec((1, N_KV, nsp, tq), lambda bi, i: (bi, 0, 0, i)),
                  keys, keys,
                  pl.BlockSpec((n_t, _VT_ROWS, tq), lambda bi, i: (bi, 0, 0)),
                  tile(Q_DIM), tile(LANES),
                  _const_spec((Q_DIM, D_MODEL)), xrow, _const_spec((1, D_MODEL))],
        out_specs=xrow,
        out_shape=jax.ShapeDtypeStruct((b * t, D_MODEL), F32),
        scratch_shapes=[pltpu.VMEM((Q_DIM, tq), F32)],
        compiler_params=_params(("parallel", "parallel")),
        name="sel_win_out",
    )(qr_t, bias_t, k_sel, k_win, v_t, ocg_t, gates_t, w_out_bf, x2d, g1)


def _nsa_out_body(oc_ref, os_ref, ow_ref, gate_ref, e_ref, w_ref, x_ref, g1_ref, o_ref):
    hi, lo = _split_bf16(gate_ref[...])
    o = jnp.zeros(oc_ref.shape, F32)
    for c, src in enumerate((oc_ref, os_ref, ow_ref)):
        o = o + (_dot(hi, e_ref[c]) + _dot(lo, e_ref[c])) * src[...]
    m = _dot(o.astype(BF16), w_ref[...])
    o_ref[...] = x_ref[...] + _rms(m, g1_ref[...])


def _gate_expand():
    e = np.zeros((3, LANES, Q_DIM), np.float32)
    for h in range(N_HEADS):
        for c in range(3):
            e[c, h * 3 + c, h * HEAD_DIM:(h + 1) * HEAD_DIM] = 1.0
    return jnp.asarray(e, dtype=BF16)


def _nsa_out(o_cmp, o_sel, o_win, gates, w_bf, x2d, g1):
    n = x2d.shape[0]
    tm = _pick(n, (256, 128, 64, 32, 16, 8))
    row = lambda w: pl.BlockSpec((tm, w), lambda i: (i, 0))
    return pl.pallas_call(
        _nsa_out_body,
        grid=(n // tm,),
        in_specs=[row(Q_DIM), row(Q_DIM), row(Q_DIM), row(LANES), _const_spec((3, LANES, Q_DIM)),
                  _const_spec((Q_DIM, D_MODEL)), row(D_MODEL), _const_spec((1, D_MODEL))],
        out_specs=row(D_MODEL),
        out_shape=jax.ShapeDtypeStruct((n, D_MODEL), F32),
        compiler_params=_params(("parallel",)),
        name="nsa_out",
    )(o_cmp, o_sel, o_win, gates, _gate_expand(), w_bf, x2d, g1)


class _PageRing:
    def __init__(self, pt_ref, cache_hbm, buf_ref, sem_ref, n_b, n_c):
        self.pt, self.cache, self.buf, self.sem = pt_ref, cache_hbm, buf_ref, sem_ref
        self.n_b, self.n_c, self.pg = n_b, n_c, buf_ref.shape[1]

    def _copies(self, step):
        b, c, slot = step // self.n_c, step % self.n_c, step % 2
        return [pltpu.make_async_copy(self.cache.at[self.pt[b, c * self.pg + p]], self.buf.at[slot, p],
                                      self.sem.at[slot]) for p in range(self.pg)]

    def acquire(self, b, c):
        step = b * self.n_c + c

        @pl.when(step == 0)
        def _():
            for cp in self._copies(step):
                cp.start()

        @pl.when(step + 1 < self.n_b * self.n_c)
        def _():
            for cp in self._copies(step + 1):
                cp.start()

        for cp in self._copies(step):
            cp.wait()
        return step % 2


def _compress_paged_body(n_b, n_c, pt_ref, cache_hbm, perm_ref, xl_ref, wk_ref, wv_ref, pek_ref, pev_ref,
                         w1k_ref, w1v_ref, w2k_ref, w2v_ref, kcc_ref, vcc_ref, buf_ref, sem_ref, xs_ref,
                         abk_ref, abv_ref):
    pg, page = buf_ref.shape[1], buf_ref.shape[4]
    rows = pg * page // STRIDE
    b, c = pl.program_id(0), pl.program_id(1)
    slot = _PageRing(pt_ref, cache_hbm, buf_ref, sem_ref, n_b, n_c).acquire(b, c)
    r0 = pl.multiple_of(c * rows, rows)
    for kv, (ab_ref, w_ref) in enumerate(((abk_ref, wk_ref), (abv_ref, wv_ref))):
        _pages_first_layer(lambda p: buf_ref[slot, p, kv], pg, perm_ref, xs_ref, ab_ref, w_ref, r0)

    @pl.when(c == n_c - 1)
    def _():
        _compress_finish(xl_ref, ((abk_ref, wk_ref, pek_ref, w1k_ref, w2k_ref, kcc_ref),
                                  (abv_ref, wv_ref, pev_ref, w1v_ref, w2v_ref, vcc_ref)), kcc_ref.shape[1])


def _compress_paged(page_table, cache_fm, x_last, wk, wv, pek, pev, w1k, w1v, w2k, w2v):
    b, n_pages = page_table.shape
    page = cache_fm.shape[3]
    pg = _pick(n_pages, (16, 8, 4, 2, 1))
    n_c = n_pages // pg
    n = n_pages * page // STRIDE
    flat = L_CMP * HEAD_DIM
    out = jax.ShapeDtypeStruct((b, n, KV_DIM), F32)
    const = _const_spec
    per_page = page // STRIDE
    return pl.pallas_call(
        functools.partial(_compress_paged_body, b, n_c),
        grid_spec=pltpu.PrefetchScalarGridSpec(
            num_scalar_prefetch=1,
            grid=(b, n_c),
            in_specs=[pl.BlockSpec(memory_space=pl.ANY),
                      const((page, page)),
                      pl.BlockSpec((1, 8, _CHUNK_W), lambda bi, c, pt: (bi, 0, 0)),
                      const(_W1_SHAPE), const(_W1_SHAPE),
                      const((flat, 1)), const((flat, 1)), const((flat, CMP_HID)), const((flat, CMP_HID)),
                      const((N_KV * CMP_HID, KV_DIM)), const((N_KV * CMP_HID, KV_DIM))],
            out_specs=[pl.BlockSpec((1, n, KV_DIM), lambda bi, c, pt: (bi, 0, 0))] * 2,
            scratch_shapes=[pltpu.VMEM((2, pg, 2, KV_DIM, page), F32), pltpu.SemaphoreType.DMA((2,)),
                            pltpu.VMEM((STRIDE, pg * per_page, KV_DIM), F32),
                            pltpu.VMEM((n + 8, _AB_W), F32), pltpu.VMEM((n + 8, _AB_W), F32)]),
        out_shape=[out, out],
        compiler_params=_params(("arbitrary", "arbitrary")),
        name="compress_paged",
    )(page_table, cache_fm, _chunk_sort_matrix(page), x_last, wk, wv, pek, pev, w1k, w1v, w2k, w2v)


def _softmax_rows(s):
    e = jnp.exp(s - jnp.max(s, axis=-1, keepdims=True))
    return e / jnp.sum(e, axis=-1, keepdims=True)


def _sample_cmp_body(n_sel, past_len, t_new, q_ref, kcc_ref, vcc_ref, ovt_ref, rep_ref, o_ref, sel_ref):
    rows = q_ref.shape[1]
    nc = kcc_ref.shape[1]
    nsp = ovt_ref.shape[0]
    gq = N_KV * t_new
    tq_col = lax.broadcasted_iota(jnp.int32, (rows, 1), 0) & (t_new - 1)
    cmp_end = lax.broadcasted_iota(jnp.int32, (1, nc), 1) * STRIDE + (L_CMP - 1)
    cmask = cmp_end <= past_len + tq_col
    lm = jnp.where(cmask, _dot_nt(q_ref[0], kcc_ref[0].astype(BF16)), NEG)
    pc = jnp.where(cmask, _softmax_rows(lm), 0.0)
    o_ref[0] = _dot(pc.astype(BF16), vcc_ref[0].astype(BF16))
    pg = pc.reshape(N_KV, N_REP, t_new, nc).sum(axis=1).reshape(gq, nc)
    hi, lo = _split_bf16(pg)
    imp_t = _dot_nt(ovt_ref[...], hi) + _dot_nt(ovt_ref[...], lo)
    blk = lax.broadcasted_iota(jnp.int32, (nsp, gq), 0)
    qpos = past_len + (lax.broadcasted_iota(jnp.int32, (nsp, gq), 1) & (t_new - 1))
    sel = _select_blocks(imp_t, blk, qpos, n_sel, min(N_TOP, n_sel))
    sel_ref[0] = _dot_nt(rep_ref[...], jnp.where(sel, 1.0, 0.0).astype(BF16)).astype(BF16)


def _sample_cmp(q_bd, kcc, vcc, n_cmp, n_sel, nsp, past_len, t_new):
    b, rows, _ = q_bd.shape
    nc = kcc.shape[1]
    gq = N_KV * t_new
    ovt = _overlap_t(nsp, nc, n_sel, n_cmp)
    rep = np.zeros((rows, gq), np.float32)
    for h in range(N_HEADS):
        for t in range(t_new):
            rep[h * t_new + t, (h // N_REP) * t_new + t] = 1.0
    return pl.pallas_call(
        functools.partial(_sample_cmp_body, n_sel, past_len, t_new),
        grid=(b,),
        in_specs=[pl.BlockSpec((1, rows, KV_DIM), lambda bi: (bi, 0, 0)),
                  pl.BlockSpec((1, nc, KV_DIM), lambda bi: (bi, 0, 0)),
                  pl.BlockSpec((1, nc, KV_DIM), lambda bi: (bi, 0, 0)),
                  _const_spec((nsp, nc)), _const_spec((rows, gq))],
        out_specs=[pl.BlockSpec((1, rows, KV_DIM), lambda bi: (bi, 0, 0)),
                   pl.BlockSpec((1, rows, nsp), lambda bi: (bi, 0, 0))],
        out_shape=[jax.ShapeDtypeStruct((b, rows, KV_DIM), F32), jax.ShapeDtypeStruct((b, rows, nsp), BF16)],
        compiler_params=_params(("parallel",)),
        name="sample_cmp",
    )(q_bd, kcc, vcc, ovt, jnp.asarray(rep, dtype=BF16))


def _flash_update(s, pv_fn, m_ref, l_ref, acc_ref):
    m_prev = m_ref[...]
    m_new = jnp.maximum(m_prev, jnp.max(s, axis=-1, keepdims=True))
    alpha = jnp.exp(m_prev - m_new)
    p = jnp.exp(s - m_new)
    l_ref[...] = alpha * l_ref[...] + jnp.sum(p, axis=-1, keepdims=True)
    acc_ref[...] = alpha * acc_ref[...] + pv_fn(p.astype(BF16))
    m_ref[...] = m_new


def _sample_sel_body(t_new, n_b, n_c, pt_ref, cache_hbm, q_ref, sel_ref, e_ref, et_ref, tail_ref, o_ref,
                     buf_ref, sem_ref, m_ref, l_ref, acc_ref):
    rows = q_ref.shape[1]
    pg, page = buf_ref.shape[1], buf_ref.shape[4]
    b, c = pl.program_id(0), pl.program_id(1)
    q = q_ref[0]

    @pl.when(c == 0)
    def _():
        _flash_init(m_ref, l_ref, acc_ref)

    @pl.when(c < n_c)
    def _():
        slot = _PageRing(pt_ref, cache_hbm, buf_ref, sem_ref, n_b, n_c).acquire(b, c)
        halves = [range(0, pg // 2), range(pg // 2, pg)] if pg > 1 else [range(pg)]

        def scores(pages):
            s = jnp.concatenate([_dot(q, buf_ref[slot, p, 0].astype(BF16)) for p in pages], axis=1)
            ok = _dot(sel_ref[0], e_ref[:, pages.start * page:pages.stop * page]) > 0.5
            return jnp.where(ok, s, NEG)

        for pages, s in [(pages, scores(pages)) for pages in halves]:
            def pv(p_bf, pages=pages):
                return sum(_dot_nt(p_bf[:, i * page:(i + 1) * page], buf_ref[slot, p, 1].astype(BF16))
                           for i, p in enumerate(pages))

            _flash_update(s, pv, m_ref, l_ref, acc_ref)

    @pl.when(c == n_c)
    def _():
        kv = tail_ref[0]
        nk = kv.shape[0]
        s = _dot_nt(q, kv[:, :KV_DIM].astype(BF16))
        tq = lax.broadcasted_iota(jnp.int32, (rows, 1), 0) & (t_new - 1)
        ok = (_dot(sel_ref[0], et_ref[...]) > 0.5) & (lax.broadcasted_iota(jnp.int32, (1, nk), 1) <= tq)
        _flash_update(jnp.where(ok, s, NEG), lambda p_bf: _dot(p_bf, kv[:, KV_DIM:].astype(BF16)),
                      m_ref, l_ref, acc_ref)
        o_ref[0] = acc_ref[...] / l_ref[...]


def _sample_sel(page_table, cache_fm, q_bd, sel01, tail_sel, n_sel, t_new):
    b, rows, _ = q_bd.shape
    nsp = sel01.shape[2]
    n_pages = page_table.shape[1]
    page = cache_fm.shape[3]
    pg = _pick(n_pages, (16, 8, 4, 2, 1))
    n_c = n_pages // pg
    kc = pg * page
    nk = tail_sel.shape[1]
    key_blk = np.arange(n_pages * page) // L_SEL
    e = (np.arange(nsp)[:, None] == key_blk[None, :]).astype(np.float32)
    et = np.zeros((nsp, nk), np.float32)
    et[n_sel - 1, :] = 1.0
    return pl.pallas_call(
        functools.partial(_sample_sel_body, t_new, b, n_c),
        grid_spec=pltpu.PrefetchScalarGridSpec(
            num_scalar_prefetch=1,
            grid=(b, n_c + 1),
            in_specs=[pl.BlockSpec(memory_space=pl.ANY),
                      pl.BlockSpec((1, rows, KV_DIM), lambda bi, c, pt: (bi, 0, 0)),
                      pl.BlockSpec((1, rows, nsp), lambda bi, c, pt: (bi, 0, 0)),
                      pl.BlockSpec((nsp, kc), lambda bi, c, pt: (0, jnp.minimum(c, n_c - 1))),
                      _const_spec((nsp, nk)),
                      pl.BlockSpec((1, nk, 2 * KV_DIM), lambda bi, c, pt: (bi, 0, 0))],
            out_specs=pl.BlockSpec((1, rows, KV_DIM), lambda bi, c, pt: (bi, 0, 0)),
            scratch_shapes=[pltpu.VMEM((2, pg, 2, KV_DIM, page), F32), pltpu.SemaphoreType.DMA((2,)),
                            pltpu.VMEM((rows, 1), F32), pltpu.VMEM((rows, 1), F32),
                            pltpu.VMEM((rows, KV_DIM), F32)]),
        out_shape=jax.ShapeDtypeStruct((b, rows, KV_DIM), F32),
        compiler_params=_params(("arbitrary", "arbitrary")),
        name="sample_sel",
    )(page_table, cache_fm, q_bd, sel01, jnp.asarray(e, dtype=BF16), jnp.asarray(et, dtype=BF16), tail_sel)


def _sample_win_body(t_new, q_ref, kv_ref, tail_ref, o_ref):
    rows = q_ref.shape[1]
    wb = kv_ref.shape[3]
    nk = tail_ref.shape[1]
    q = q_ref[0]
    tq = lax.broadcasted_iota(jnp.int32, (rows, 1), 0) & (t_new - 1)
    tail = tail_ref[0]
    s_old = jnp.where(lax.broadcasted_iota(jnp.int32, (1, wb), 1) > tq - WINDOW + wb,
                      _dot(q, kv_ref[0, 0].astype(BF16)), NEG)
    s_new = jnp.where(lax.broadcasted_iota(jnp.int32, (1, nk), 1) <= tq,
                      _dot_nt(q, tail[:, :KV_DIM].astype(BF16)), NEG)
    m = jnp.maximum(jnp.max(s_old, axis=-1, keepdims=True), jnp.max(s_new, axis=-1, keepdims=True))
    p_old = jnp.exp(s_old - m)
    p_new = jnp.exp(s_new - m)
    den = jnp.sum(p_old, axis=-1, keepdims=True) + jnp.sum(p_new, axis=-1, keepdims=True)
    o = _dot_nt(p_old.astype(BF16), kv_ref[0, 1].astype(BF16)) + _dot(p_new.astype(BF16), tail[:, KV_DIM:].astype(BF16))
    o_ref[0] = o / den


def _sample_win(q_bd, win_fm, tail_win, t_new):
    b, rows, _ = q_bd.shape
    wb = win_fm.shape[3]
    nk = tail_win.shape[1]
    return pl.pallas_call(
        functools.partial(_sample_win_body, t_new),
        grid=(b,),
        in_specs=[pl.BlockSpec((1, rows, KV_DIM), lambda bi: (bi, 0, 0)),
                  pl.BlockSpec((1, 2, KV_DIM, wb), lambda bi: (bi, 0, 0, 0)),
                  pl.BlockSpec((1, nk, 2 * KV_DIM), lambda bi: (bi, 0, 0))],
        out_specs=pl.BlockSpec((1, rows, KV_DIM), lambda bi: (bi, 0, 0)),
        out_shape=jax.ShapeDtypeStruct((b, rows, KV_DIM), F32),
        compiler_params=_params(("parallel",)),
        name="sample_win",
    )(q_bd, win_fm, tail_win)


def _heads_major(a2d, b, t, n):
    return a2d.reshape(b, t, n, HEAD_DIM).transpose(0, 2, 1, 3)


def _tokens_major(a_hm):
    b, n, t, d = a_hm.shape
    return a_hm.transpose(0, 2, 1, 3).reshape(b * t, n * d)


def _block_diag_q(q2d, b, t):
    q_hm = _heads_major(q2d, b, t, N_HEADS)
    onehot = jnp.asarray(np.eye(N_KV)[np.arange(N_HEADS) // N_REP], dtype=q2d.dtype)
    return jnp.einsum('bhtd,hg->bhtgd', q_hm, onehot).reshape(b, N_HEADS * t, KV_DIM)


def _own_group(o_bd, b, t):
    o = o_bd.reshape(b, N_KV, N_REP, t, N_KV, HEAD_DIM)
    o = jnp.stack([o[:, g, :, :, g, :] for g in range(N_KV)], axis=1)
    return o.transpose(0, 3, 1, 2, 4).reshape(b * t, Q_DIM)


def _nsa_prompt(x2d, b, t, g0, g1, w_in_bf, w_out_bf, cw):
    pos = jnp.arange(t)
    q_t, qr_t, v_t, gates_t, cmp_fm, sel_fm, win_fm, k_sel, k_win = _nsa_proj(x2d, g0, w_in_bf, pos)
    assert t % STRIDE == 0 and t % L_SEL == 0
    n_ch = t // STRIDE
    n_cmp = n_ch - 1
    n_sel = t // L_SEL
    nsp = -(-n_sel // HEAD_DIM) * HEAD_DIM
    x_last = jnp.zeros((b, 8, _CHUNK_W), F32)
    (wk, pek, w1k, w2k), (wv, pev, w1v, w2v) = cw
    kcc, vcc = _compress(cmp_fm, x_last, wk, wv, pek, pev, w1k, w1v, w2k, w2v)
    gm = lambda a: a.reshape(b, -1, N_KV, HEAD_DIM).transpose(0, 2, 1, 3).astype(BF16)
    ocg_t, bias_t = _cmp_topk(q_t, gm(kcc), vcc.transpose(0, 2, 1).astype(BF16), gates_t, b, n_cmp, n_sel, nsp)
    x1 = _sel_win_out(qr_t, bias_t, k_sel, k_win, v_t, ocg_t, gates_t, w_out_bf, x2d, g1, b)
    rows5 = lambda a: a.reshape(b, 2, N_KV, HEAD_DIM, -1).transpose(0, 4, 1, 2, 3)[None]
    n_win = min(WINDOW, t)
    return x1, (rows5(cmp_fm), rows5(sel_fm), rows5(win_fm[:, :, t - n_win:]))


def _nsa_sample(x2d, b, t, g0, w_in_bf, cw, cache_cmp_l, cache_sel_l, cache_win_l, page_table):
    n_pages = page_table.shape[1]
    page = cache_cmp_l.shape[1]
    past_len = n_pages * page
    assert page % L_SEL == 0 and page % STRIDE == 0 and t <= STRIDE and t & (t - 1) == 0
    pos = past_len + jnp.arange(t)
    q, qr, cmp_rows, sel_rows, win_rows, gates = _nsa_proj(x2d, g0, w_in_bf, pos)
    row_w = 2 * KV_DIM
    fm = lambda a: a.transpose(0, 2, 3, 4, 1).reshape(a.shape[0], 2, KV_DIM, a.shape[1])
    n_past_ch = past_len // STRIDE
    n_cmp = n_past_ch
    n_sel = past_len // L_SEL + 1
    nsp = -(-n_sel // LANES) * LANES
    new3 = lambda a: a.reshape(b, t, row_w)
    x_last = jnp.pad(new3(cmp_rows), ((0, 0), (0, STRIDE - t), (0, 0))).reshape(b, 1, _CHUNK_W)
    x_last = jnp.pad(x_last, ((0, 0), (0, 7), (0, 0)))
    (wk, pek, w1k, w2k), (wv, pev, w1v, w2v) = cw
    kcc, vcc = _compress_paged(page_table, fm(cache_cmp_l), x_last, wk, wv, pek, pev, w1k, w1v, w2k, w2v)
    o_cmp_bd, sel01 = _sample_cmp(_block_diag_q(q, b, t), kcc, vcc, n_cmp, n_sel, nsp, past_len, t)
    qr_bd = _block_diag_q(qr, b, t)
    tail = lambda a: jnp.pad(new3(a), ((0, 0), (0, LANES - t), (0, 0)))
    o_sel_bd = _sample_sel(page_table, fm(cache_sel_l), qr_bd, sel01, tail(sel_rows), n_sel, t)
    win_fm = fm(cache_win_l)
    o_win_bd = _sample_win(qr_bd, win_fm, tail(win_rows), t)
    w_buf = win_fm.shape[3]
    new_fm = new3(win_rows).reshape(b, t, 2, KV_DIM).transpose(0, 2, 3, 1)
    new_win = jnp.concatenate([win_fm, new_fm], axis=3)[..., -w_buf:]
    new_win = new_win.reshape(b, 2, N_KV, HEAD_DIM, w_buf).transpose(0, 4, 1, 2, 3)
    rows5 = lambda a: a.reshape(1, b, -1, 2, N_KV, HEAD_DIM)
    caches = (rows5(cmp_rows), rows5(sel_rows), new_win[None])
    return _own_group(o_cmp_bd, b, t), _own_group(o_sel_bd, b, t), _own_group(o_win_bd, b, t), gates, caches


def kernel(x_prompt, x_sample, cache_cmp, cache_sel, cache_win, state_conv, page_table, p_prompt, p_sample,
           norm_g, w_ff1, w_ff2, w_ple, w_ple_gate, b_ple_gate,
           conv_w_pw1, conv_b_pw1, conv_w_dw, conv_b_dw, conv_ln_g, conv_ln_b, conv_w_pw2, conv_b_pw2,
           nsa_w_in, nsa_w_out, nsa_cmp_pe, nsa_cmp_w1, nsa_cmp_w2):
    depth = norm_g.shape[0]
    bf = lambda a: a.astype(BF16)
    row = lambda a: a.reshape(1, -1)
    w_ff1_bf, w_ff2_bf, w_ple_bf, w_gate_bf = bf(w_ff1), bf(w_ff2), bf(w_ple), bf(w_ple_gate)
    w_pw1_bf, w_pw2_bf, w_out_bf = bf(conv_w_pw1), bf(conv_w_pw2), bf(nsa_w_out)
    w_in_bf = bf(jnp.pad(nsa_w_in, ((0, 0), (0, 0), (0, _PROJ_W - nsa_w_in.shape[2]))))
    w_dw = jnp.pad(conv_w_dw, ((0, 0), (0, CONV_HALO - CONV_W), (0, 0)))

    def run(x, p, sample):
        b, t, _ = x.shape
        n = b * t
        x2d = x.reshape(n, D_MODEL)
        cmp_o, sel_o, win_o, conv_o = [], [], [], []
        for i in range(depth):
            g = lambda j: row(norm_g[i, j])
            if i % 2 == 0:
                c = i // 2
                u = _conv_front(x2d, g(0), w_pw1_bf[c], row(conv_b_pw1[c])).reshape(b, t, D_MODEL)
                if sample:
                    hist = jnp.pad(state_conv[c], ((0, 0), (CONV_HALO - (CONV_W - 1), 0), (0, 0)))
                else:
                    hist = jnp.zeros((b, CONV_HALO, D_MODEL), F32)
                tp = -(-t // CONV_HALO) * CONV_HALO
                pad_t = lambda a: jnp.pad(a, ((0, 0), (0, tp - t), (0, 0))) if tp > t else a
                x1 = _conv_back(hist, pad_t(u), pad_t(x2d.reshape(b, t, D_MODEL)), w_dw[c], row(conv_b_dw[c]),
                                row(conv_ln_g[c]), row(conv_ln_b[c]), w_pw2_bf[c], row(conv_b_pw2[c]),
                                g(1))[:, :t].reshape(n, D_MODEL)
                keep = CONV_W - 1
                conv_o.append(jnp.concatenate([hist, u], axis=1)[:, CONV_HALO + t - keep:][None])
            else:
                a = i // 2
                cw = _compress_weights(nsa_cmp_pe[a], nsa_cmp_w1[a], nsa_cmp_w2[a])
                if sample:
                    oc, osel, ow, gates, caches = _nsa_sample(x2d, b, t, g(0), w_in_bf[a], cw, cache_cmp[a],
                                                              cache_sel[a], cache_win[a], page_table)
                    x1 = _nsa_out(oc, osel, ow, gates, w_out_bf[a], x2d, g(1))
                else:
                    x1, caches = _nsa_prompt(x2d, b, t, g(0), g(1), w_in_bf[a], w_out_bf[a], cw)
                for dst, rows_ in zip((cmp_o, sel_o, win_o), caches):
                    dst.append(rows_)
            x2d = _ffn(x1, g(2), w_ff1_bf[i], w_ff2_bf[i], g(3), w_gate_bf[i], row(b_ple_gate[i]),
                       p[i].reshape(n, -1), w_ple_bf[i])
        cat = lambda parts: jnp.concatenate(parts, axis=0)
        return x2d.reshape(b, t, D_MODEL), cat(cmp_o), cat(sel_o), cat(win_o), cat(conv_o)

    y_p, cmp_p, sel_p, win_p, conv_p = run(x_prompt, p_prompt, False)
    y_s, cmp_s, sel_s, win_s, conv_s = run(x_sample, p_sample, True)
    return (y_p, y_s, cmp_p, cmp_s, sel_p, sel_s, win_p, win_s, conv_p, conv_s)
```

```python
import functools
import math

import numpy as np
import jax
import jax.numpy as jnp
from jax import lax
from jax.experimental import pallas as pl
from jax.experimental.pallas import tpu as pltpu

F32 = jnp.float32
BF16 = jnp.bfloat16

D_MODEL = 1024
N_HEADS = 16
N_KV = 4
N_REP = N_HEADS // N_KV
HEAD_DIM = 64
ROT_DIM = HEAD_DIM // 4
ROPE_THETA = 500000.0
L_CMP = 32
STRIDE = 16
CMP_HID = 2 * HEAD_DIM
L_SEL = 64
N_TOP = 16
WINDOW = 512
CONV_W = 31
Q_DIM = N_HEADS * HEAD_DIM
KV_DIM = N_KV * HEAD_DIM
GATE_DIM = 3 * N_HEADS
D_FF = 4 * D_MODEL
EPS = 1e-6
NEG = -1e30
FORCE = 1e6
SCALE = HEAD_DIM ** -0.5

LANES = 128
CONV_HALO = 32
VMEM_LIMIT = 56 * 1024 * 1024


def _pick(n, cands):
    for c in cands:
        if n % c == 0:
            return c
    raise ValueError(f"no tile in {cands} divides {n}")


def _const_spec(shape):
    nd = len(shape)
    return pl.BlockSpec(shape, lambda *_: (0,) * nd, pipeline_mode=pl.Buffered(1))


def _params(sem):
    return pltpu.CompilerParams(dimension_semantics=sem, vmem_limit_bytes=VMEM_LIMIT)


def _sigmoid(x):
    return 1.0 / (1.0 + jnp.exp(-x))


def _rms(x, g):
    return x * lax.rsqrt(jnp.mean(x * x, axis=-1, keepdims=True) + EPS) * g


def _dot(a, b):
    return jnp.dot(a, b, preferred_element_type=F32)


def _dot_nt(a, b):
    return lax.dot_general(a, b, (((1,), (1,)), ((), ())), preferred_element_type=F32)


def _split_bf16(x):
    hi = x.astype(BF16)
    lo = (x - hi.astype(F32)).astype(BF16)
    return hi, lo


def _conv_front_body(x_ref, g_ref, w_ref, b_ref, u_ref):
    h = _rms(x_ref[...], g_ref[...]).astype(BF16)
    z = _dot(h, w_ref[...]) + b_ref[...]
    u_ref[...] = z[:, :D_MODEL] * _sigmoid(z[:, D_MODEL:])


def _conv_front(x2d, g, w_bf, b):
    n = x2d.shape[0]
    tm = _pick(n, (512, 256, 128, 64, 32, 16, 8))
    return pl.pallas_call(
        _conv_front_body,
        grid=(n // tm,),
        in_specs=[pl.BlockSpec((tm, D_MODEL), lambda i: (i, 0)),
                  _const_spec((1, D_MODEL)),
                  _const_spec((D_MODEL, 2 * D_MODEL)),
                  _const_spec((1, 2 * D_MODEL))],
        out_specs=pl.BlockSpec((tm, D_MODEL), lambda i: (i, 0)),
        out_shape=jax.ShapeDtypeStruct((n, D_MODEL), F32),
        compiler_params=_params(("parallel",)),
        name="conv_front",
    )(x2d, g, w_bf, b)


_CONV_ROWS = 32
_CONV_COLS = 256


def _conv_back_body(hist_ref, prev_ref, main_ref, wdw_ref, bdw_ref, lng_ref, lnb_ref, w2_ref, b2_ref, x_ref, g1_ref,
                    o_ref, win_ref, y_ref):
    tt = main_ref.shape[1]
    @pl.when(pl.program_id(1) == 0)
    def _():
        win_ref[0:CONV_HALO, :] = hist_ref[0]

    @pl.when(pl.program_id(1) > 0)
    def _():
        win_ref[0:CONV_HALO, :] = prev_ref[0]

    win_ref[CONV_HALO:CONV_HALO + tt, :] = main_ref[0]
    first = CONV_HALO - (CONV_W - 1)

    sub = 8
    for r0 in range(0, tt, _CONV_ROWS):
        for c0 in range(0, D_MODEL, _CONV_COLS):
            cols = slice(c0, c0 + _CONV_COLS)
            acc = jnp.zeros((_CONV_ROWS, _CONV_COLS), F32)
            for s in range(sub):
                part = None
                for k in range(CONV_W):
                    if (first + k) % sub != s:
                        continue
                    base = r0 + first + k - s
                    term = win_ref[base:base + _CONV_ROWS + (sub if s else 0), cols] * wdw_ref[k:k + 1, cols]
                    part = term if part is None else part + term
                if part is not None:
                    acc = acc + part[s:s + _CONV_ROWS]
            y_ref[r0:r0 + _CONV_ROWS, cols] = acc + bdw_ref[:, cols]
    y = y_ref[...]
    yc = y - jnp.mean(y, axis=-1, keepdims=True)
    var = jnp.mean(yc * yc, axis=-1, keepdims=True)
    ln = yc * lax.rsqrt(var + EPS) * lng_ref[...] + lnb_ref[...]
    act = (ln * _sigmoid(ln)).astype(BF16)
    m = _dot(act, w2_ref[...]) + b2_ref[...]
    o_ref[0] = x_ref[0] + _rms(m, g1_ref[...])


def _conv_back(hist, u3d, x3d, wdw, bdw, lng, lnb, w2_bf, b2, g1):
    b, t, _ = x3d.shape
    tt = _pick(t, (256, 128, 64, 32))
    halo_blocks = tt // CONV_HALO
    return pl.pallas_call(
        _conv_back_body,
        grid=(b, t // tt),
        in_specs=[pl.BlockSpec((1, CONV_HALO, D_MODEL), lambda bi, i: (bi, 0, 0)),
                  pl.BlockSpec((1, CONV_HALO, D_MODEL), lambda bi, i: (bi, jnp.maximum(i * halo_blocks - 1, 0), 0)),
                  pl.BlockSpec((1, tt, D_MODEL), lambda bi, i: (bi, i, 0)),
                  _const_spec((CONV_HALO, D_MODEL)),
                  _const_spec((1, D_MODEL)), _const_spec((1, D_MODEL)), _const_spec((1, D_MODEL)),
                  _const_spec((D_MODEL, D_MODEL)), _const_spec((1, D_MODEL)),
                  pl.BlockSpec((1, tt, D_MODEL), lambda bi, i: (bi, i, 0)),
                  _const_spec((1, D_MODEL))],
        out_specs=pl.BlockSpec((1, tt, D_MODEL), lambda bi, i: (bi, i, 0)),
        out_shape=jax.ShapeDtypeStruct((b, t, D_MODEL), F32),
        scratch_shapes=[pltpu.VMEM((tt + CONV_HALO, D_MODEL), F32), pltpu.VMEM((tt, D_MODEL), F32)],
        compiler_params=_params(("parallel", "parallel")),
        name="conv_back",
    )(hist, u3d, u3d, wdw, bdw, lng, lnb, w2_bf, b2, x3d, g1)


_FF_CHUNK = 1024


def _ffn_body(x_ref, g2_ref, w1_ref, w2_ref, g3_ref, wg_ref, bg_ref, p_ref, wp_ref, o_ref):
    x = x_ref[...]
    h = _rms(x, g2_ref[...]).astype(BF16)
    f = jnp.zeros(x.shape, F32)
    for c in range(D_FF // _FF_CHUNK):
        a = jnp.maximum(_dot(h, w1_ref[:, c * _FF_CHUNK:(c + 1) * _FF_CHUNK]), 0.0)
        f = f + _dot((a * a).astype(BF16), w2_ref[c * _FF_CHUNK:(c + 1) * _FF_CHUNK, :])
    x2 = x + _rms(f, g3_ref[...])
    gate = _sigmoid(_dot(x2.astype(BF16), wg_ref[...]) + bg_ref[...])
    o_ref[...] = x2 + gate * _dot(p_ref[0].astype(BF16), wp_ref[...])


def _ffn(x2d, g2, w1_bf, w2_bf, g3, wg_bf, bg, p_all, layer, wp_bf):
    n = x2d.shape[0]
    d_ple = p_all.shape[2]
    tm = _pick(n, (512, 256, 128, 64, 32, 16, 8))
    return pl.pallas_call(
        _ffn_body,
        grid=(n // tm,),
        in_specs=[pl.BlockSpec((tm, D_MODEL), lambda i: (i, 0)),
                  _const_spec((1, D_MODEL)),
                  _const_spec((D_MODEL, D_FF)), _const_spec((D_FF, D_MODEL)),
                  _const_spec((1, D_MODEL)),
                  _const_spec((D_MODEL, D_MODEL)), _const_spec((1, D_MODEL)),
                  pl.BlockSpec((1, tm, d_ple), lambda i: (layer, i, 0)),
                  _const_spec((d_ple, D_MODEL))],
        out_specs=pl.BlockSpec((tm, D_MODEL), lambda i: (i, 0)),
        out_shape=jax.ShapeDtypeStruct((n, D_MODEL), F32),
        compiler_params=_params(("parallel",)),
        name="ffn_ple",
    )(x2d, g2, w1_bf, w2_bf, g3, wg_bf, bg, p_all, wp_bf)


_PROJ_W = Q_DIM + 6 * KV_DIM + LANES
LOG2E = 1.4426950408889634
V_ROWS = HEAD_DIM + 16
_VT_ROWS = 2 * N_KV * V_ROWS


def _rope_block(blk, c, su, sd):
    return blk * c + pltpu.roll(blk, ROT_DIM // 2, 1) * su + pltpu.roll(blk, LANES - ROT_DIM // 2, 1) * sd


def _nsa_proj_body(prompt, x_ref, g_ref, w_ref, c_ref, su_ref, sd_ref, *refs):
    if prompt:
        blk_ref, qt_ref, qrt_ref, vt_ref, gatet_ref = refs[:5]
        cache_t_refs, key_refs = refs[5:8], refs[8:10]
        low = lax.broadcasted_iota(jnp.int32, (x_ref.shape[0], LANES), 1) < HEAD_DIM
        ones = jnp.ones((V_ROWS - HEAD_DIM, x_ref.shape[0]), BF16)
    else:
        q_ref, qr_ref, cmp_ref, sel_ref, win_ref, gate_ref = refs
    h = _rms(x_ref[...], g_ref[...]).astype(BF16)
    z = _dot(h, w_ref[...])
    c, su, sd = c_ref[...], su_ref[...], sd_ref[...]
    for i in range(Q_DIM // LANES):
        cols = slice(i * LANES, (i + 1) * LANES)
        qs = z[:, cols] * SCALE
        qrs = _rope_block(z[:, cols], c, su, sd) * SCALE
        if prompt:
            qt_ref[0, cols, :] = (qs * LOG2E).T.astype(BF16)
            qrt_ref[0, cols, :] = (qrs * LOG2E).T.astype(BF16)
        else:
            q_ref[:, cols] = qs.astype(BF16)
            qr_ref[:, cols] = qrs.astype(BF16)
    o = Q_DIM
    if prompt:
        for i in range(2 * KV_DIM // LANES):
            cache_t_refs[0][0, i * LANES:(i + 1) * LANES, :] = z[:, o + i * LANES:o + (i + 1) * LANES].T
    else:
        cmp_ref[...] = z[:, o:o + 2 * KV_DIM]
    o += 2 * KV_DIM
    for n in range(2):
        for i in range(KV_DIM // LANES):
            k_rot = _rope_block(z[:, o + i * LANES:o + (i + 1) * LANES], c, su, sd)
            if not prompt:
                (sel_ref, win_ref)[n][:, i * LANES:(i + 1) * LANES] = k_rot
                continue
            vt = z[:, o + KV_DIM + i * LANES:o + KV_DIM + (i + 1) * LANES].T
            cache_t_refs[1 + n][0, i * LANES:(i + 1) * LANES, :] = k_rot.T
            cache_t_refs[1 + n][0, KV_DIM + i * LANES:KV_DIM + (i + 1) * LANES, :] = vt
            fill = blk_ref[...] if n == 0 else 0.0
            per_tile = LANES // HEAD_DIM
            key_refs[n][per_tile * i] = jnp.where(low, k_rot, fill).astype(BF16)
            key_refs[n][per_tile * i + 1] = jnp.where(low, pltpu.roll(k_rot, HEAD_DIM, 1), fill).astype(BF16)
            for k in range(per_tile):
                base = (n * N_KV + i * per_tile + k) * V_ROWS
                vt_ref[0, base:base + HEAD_DIM, :] = vt[k * HEAD_DIM:(k + 1) * HEAD_DIM].astype(BF16)
                vt_ref[0, base + HEAD_DIM:base + V_ROWS, :] = ones
        if not prompt:
            (sel_ref, win_ref)[n][:, KV_DIM:] = z[:, o + KV_DIM:o + 2 * KV_DIM]
        o += 2 * KV_DIM
    gates = _sigmoid(z[:, o:o + LANES])
    if prompt:
        gatet_ref[0] = gates.T
    else:
        gate_ref[...] = gates


def _rope_tables(pos, rows):
    half = ROT_DIM // 2
    inv = jnp.float32(ROPE_THETA) ** (-jnp.arange(half, dtype=F32) * (2.0 / ROT_DIM))
    ang = pos.astype(F32)[:, None] * inv[None, :]
    lane = np.arange(LANES)
    within = lane % HEAD_DIM
    cos = jnp.cos(ang)[:, lane % half]
    sin = jnp.sin(ang)[:, lane % half]
    c = jnp.where(within[None, :] < ROT_DIM, cos, 1.0)
    su = jnp.where((within[None, :] >= half) & (within[None, :] < ROT_DIM), sin, 0.0)
    sd = jnp.where(within[None, :] < half, -sin, 0.0)
    reps = rows // pos.shape[0]
    return tuple(jnp.tile(a, (reps, 1)) for a in (c, su, sd))


def _nsa_proj(x2d, g, w_bf, pos):
    n = x2d.shape[0]
    t = pos.shape[0]
    tm = _pick(n, (256, 128, 64, 32, 16, 8))
    if t >= tm:
        assert t % tm == 0
        tabs = _rope_tables(pos, t)
        nt = t // tm
        tab_spec = pl.BlockSpec((tm, LANES), lambda i: (i % nt, 0))
    else:
        assert tm % t == 0
        tabs = _rope_tables(pos, tm)
        tab_spec = _const_spec((tm, LANES))
    row = lambda w: pl.BlockSpec((tm, w), lambda i: (i, 0))
    rows_of = lambda w, dt: jax.ShapeDtypeStruct((n, w), dt)
    in_specs = [row(D_MODEL), _const_spec((1, D_MODEL)), _const_spec((D_MODEL, _PROJ_W)),
                tab_spec, tab_spec, tab_spec]
    args = [x2d, g, w_bf, *tabs]
    prompt = t >= tm
    if prompt:
        assert t // L_SEL <= HEAD_DIM
        blk = np.zeros((t, LANES), np.float32)
        blk[np.arange(t), HEAD_DIM + np.arange(t) // L_SEL] = 1.0
        in_specs.append(tab_spec)
        args.append(jnp.asarray(blk))
        tile = lambda rows: pl.BlockSpec((1, rows, tm), lambda i: (i, 0, 0))
        tiles = lambda rows, dt: jax.ShapeDtypeStruct((n // tm, rows, tm), dt)
        out_specs = [tile(Q_DIM), tile(Q_DIM), tile(_VT_ROWS), tile(LANES)]
        out_shape = [tiles(Q_DIM, BF16), tiles(Q_DIM, BF16), tiles(_VT_ROWS, BF16), tiles(LANES, F32)]
        out_specs += [pl.BlockSpec((1, 2 * KV_DIM, tm), lambda i: (i // nt, 0, i % nt))] * 3
        out_shape += [jax.ShapeDtypeStruct((n // t, 2 * KV_DIM, t), F32)] * 3
        out_specs += [pl.BlockSpec((N_KV, tm, LANES), lambda i: (0, i, 0))] * 2
        out_shape += [jax.ShapeDtypeStruct((N_KV, n, LANES), BF16)] * 2
    else:
        out_specs = [row(Q_DIM), row(Q_DIM), row(2 * KV_DIM), row(2 * KV_DIM), row(2 * KV_DIM), row(LANES)]
        out_shape = [rows_of(Q_DIM, BF16), rows_of(Q_DIM, BF16), rows_of(2 * KV_DIM, F32),
                     rows_of(2 * KV_DIM, F32), rows_of(2 * KV_DIM, F32), rows_of(LANES, F32)]
    return pl.pallas_call(
        functools.partial(_nsa_proj_body, prompt),
        grid=(n // tm,),
        in_specs=in_specs,
        out_specs=out_specs,
        out_shape=out_shape,
        compiler_params=_params(("parallel",)),
        name="nsa_proj",
    )(*args)


_CHUNK_W = STRIDE * 2 * KV_DIM
_AB_W = 2 * N_KV * CMP_HID


_S_PER_DOT = 4


def _first_layer(get_tile, w_ref, rows):
    half = LANES // 2
    low = lax.broadcasted_iota(jnp.int32, (rows, LANES), 1) < half
    out = []
    for j in range(KV_DIM // LANES):
        acc = [jnp.zeros((rows, 2 * CMP_HID), F32) for _ in range(2)]
        for sq in range(STRIDE // _S_PER_DOT):
            even, odd = [], []
            for s in range(sq * _S_PER_DOT, (sq + 1) * _S_PER_DOT, 2):
                a, b = get_tile(s, j), get_tile(s + 1, j)
                even.append(jnp.where(low, a, pltpu.roll(b, half, 1)))
                odd.append(jnp.where(low, pltpu.roll(a, half, 1), b))
            acc[0] = acc[0] + _dot(jnp.concatenate(even, axis=1).astype(BF16), w_ref[sq])
            acc[1] = acc[1] + _dot(jnp.concatenate(odd, axis=1).astype(BF16), w_ref[sq])
        out += acc
    return out


def _store_ab(ab_ref, rows_slice, accs):
    for g, acc in enumerate(accs):
        ab_ref[rows_slice, g * CMP_HID:(g + 1) * CMP_HID] = acc[:, :CMP_HID]
        ab_ref[rows_slice, (N_KV + g) * CMP_HID:(N_KV + g + 1) * CMP_HID] = acc[:, CMP_HID:]


def _chunk_rows_tile(src_ref, kv):
    def get(s, j):
        lo = s * 2 * KV_DIM + kv * KV_DIM + j * LANES
        return src_ref[0, :, lo:lo + LANES]
    return get


def _compress_finish(xl_ref, per_kv, n_rows):
    half = _AB_W // 2
    for kv, (ab_ref, w_ref, pe_ref, w1_ref, w2_ref, out_ref) in enumerate(per_kv):
        _store_ab(ab_ref, slice(n_rows, n_rows + 8), _first_layer(_chunk_rows_tile(xl_ref, kv), w_ref, 8))
        c = jnp.sum(pe_ref[...] * w1_ref[...], axis=0, keepdims=True)
        c = jnp.concatenate([c] * N_KV, axis=1)
        pre = ab_ref[0:n_rows, 0:half] + ab_ref[1:n_rows + 1, half:] + c
        hid = (pre * _sigmoid(pre)).astype(BF16)
        out_ref[0] = _dot(hid, w2_ref[...])


def _pages_first_layer(page_of, n_pages, perm_ref, xs_ref, ab_ref, w_ref, r0):
    per_page = perm_ref.shape[0] // STRIDE
    rows = n_pages * per_page
    for p in range(n_pages):
        t = _dot_nt(perm_ref[...], page_of(p).astype(BF16))
        for s in range(STRIDE):
            xs_ref[s, p * per_page:(p + 1) * per_page, :] = t[s * per_page:(s + 1) * per_page]
    tile = lambda s, j: xs_ref[s, :, j * LANES:(j + 1) * LANES]
    _store_ab(ab_ref, pl.ds(r0, rows), _first_layer(tile, w_ref, rows))


def _chunk_sort_matrix(page):
    per_page = page // STRIDE
    assert per_page % 8 == 0
    tok = np.arange(page)
    perm = np.zeros((page, page), np.float32)
    perm[(tok % STRIDE) * per_page + tok // STRIDE, tok] = 1.0
    return jnp.asarray(perm, dtype=BF16)


def _compress_body(perm_ref, x_ref, xl_ref, wk_ref, wv_ref, pek_ref, pev_ref, w1k_ref, w1v_ref, w2k_ref, w2v_ref,
                   kcc_ref, vcc_ref, xs_ref, abk_ref, abv_ref):
    page = perm_ref.shape[0]
    n_pages = x_ref.shape[2] // page
    c = pl.program_id(1)
    rows = n_pages * page // STRIDE
    r0 = pl.multiple_of(c * rows, rows)
    for kv, (ab_ref, w_ref) in enumerate(((abk_ref, wk_ref), (abv_ref, wv_ref))):
        page_of = lambda p: x_ref[0, kv * KV_DIM:(kv + 1) * KV_DIM, p * page:(p + 1) * page]
        _pages_first_layer(page_of, n_pages, perm_ref, xs_ref, ab_ref, w_ref, r0)

    @pl.when(c == pl.num_programs(1) - 1)
    def _():
        _compress_finish(xl_ref, ((abk_ref, wk_ref, pek_ref, w1k_ref, w2k_ref, kcc_ref),
                                  (abv_ref, wv_ref, pev_ref, w1v_ref, w2v_ref, vcc_ref)), kcc_ref.shape[1])


def _compress(x_fm, x_last, wk, wv, pek, pev, w1k, w1v, w2k, w2v):
    b, _, t = x_fm.shape
    page = LANES
    tb = _pick(t, (2048, 1024, 512, 256, 128))
    n = t // STRIDE
    flat = L_CMP * HEAD_DIM
    out = jax.ShapeDtypeStruct((b, n, KV_DIM), F32)
    return pl.pallas_call(
        _compress_body,
        grid=(b, t // tb),
        in_specs=[_const_spec((page, page)),
                  pl.BlockSpec((1, 2 * KV_DIM, tb), lambda bi, i: (bi, 0, i)),
                  pl.BlockSpec((1, 8, _CHUNK_W), lambda bi, i: (bi, 0, 0)),
                  _const_spec(_W1_SHAPE), _const_spec(_W1_SHAPE),
                  _const_spec((flat, 1)), _const_spec((flat, 1)),
                  _const_spec((flat, CMP_HID)), _const_spec((flat, CMP_HID)),
                  _const_spec((N_KV * CMP_HID, KV_DIM)), _const_spec((N_KV * CMP_HID, KV_DIM))],
        out_specs=[pl.BlockSpec((1, n, KV_DIM), lambda bi, i: (bi, 0, 0))] * 2,
        out_shape=[out, out],
        scratch_shapes=[pltpu.VMEM((STRIDE, tb // STRIDE, KV_DIM), F32),
                        pltpu.VMEM((n + 8, _AB_W), F32), pltpu.VMEM((n + 8, _AB_W), F32)],
        compiler_params=_params(("parallel", "arbitrary")),
        name="compress",
    )(_chunk_sort_matrix(page), x_fm, x_last, wk, wv, pek, pev, w1k, w1v, w2k, w2v)


_W1_SHAPE = (STRIDE // _S_PER_DOT, _S_PER_DOT * HEAD_DIM, 2 * CMP_HID)


def _compress_weights(cmp_pe, cmp_w1, cmp_w2):
    eye = jnp.eye(N_KV, dtype=F32)
    outs = []
    for kv in range(2):
        w1 = cmp_w1[kv].reshape(2, STRIDE // _S_PER_DOT, _S_PER_DOT, HEAD_DIM, CMP_HID)
        stacked = w1.transpose(1, 2, 3, 0, 4).reshape(_W1_SHAPE)
        w2 = jnp.einsum('gk,hd->ghkd', eye, cmp_w2[kv]).reshape(N_KV * CMP_HID, KV_DIM)
        outs.append((stacked.astype(BF16), cmp_pe[kv].reshape(L_CMP * HEAD_DIM, 1),
                     cmp_w1[kv].reshape(L_CMP * HEAD_DIM, CMP_HID), w2.astype(BF16)))
    return outs


def _select_blocks(imp_t, blk, qpos, n_real, k):
    cur = jnp.right_shift(qpos, int(math.log2(L_SEL)))
    forced = (blk == 0) | (blk == cur) | (blk == cur - 1)
    future = blk * L_SEL > qpos
    score = jnp.where(future, -jnp.inf, jnp.where(forced, FORCE, imp_t))
    score = jnp.where(blk < n_real, score, -jnp.inf)
    sub = 8
    n_groups = -(-n_real // sub)
    rows = [score[i * sub:(i + 1) * sub] for i in range(n_groups)]
    local = lax.broadcasted_iota(jnp.int32, rows[0].shape, 0)
    rank = [jnp.zeros(r.shape, jnp.int32) for r in rows]
    for j in range(n_real):
        sj = score[j:j + 1, :]
        for n, r in enumerate(rows):
            if n * sub > j:
                beats = sj >= r
            elif n * sub + sub - 1 <= j:
                beats = sj > r
            else:
                beats = jnp.where(local > j - n * sub, jnp.where(sj >= r, 1, 0), jnp.where(sj > r, 1, 0)) > 0
            rank[n] = rank[n] + jnp.where(beats, 1, 0)
    pad = score.shape[0] - n_groups * sub
    if pad:
        rank.append(jnp.full((pad, score.shape[1]), k, jnp.int32))
    return (jnp.concatenate(rank, axis=0) < k) & (blk < n_real)


def _overlap_t(n_sel_pad, n_cmp_pad, n_sel, n_cmp):
    ci = np.arange(n_cmp_pad)[None, :] * STRIDE
    sj = np.arange(n_sel_pad)[:, None] * L_SEL
    ov = (ci < sj + L_SEL) & (ci + L_CMP > sj)
    ov &= (np.arange(n_cmp_pad)[None, :] < n_cmp) & (np.arange(n_sel_pad)[:, None] < n_sel)
    return jnp.asarray(ov, dtype=BF16)


def _head_rows(h):
    return slice(h * HEAD_DIM, (h + 1) * HEAD_DIM)


def _cmp_topk_body(n_sel, qt_ref, kcc_ref, vcct_ref, gt_ref, ovt_ref, ocg_ref, bias_ref):
    tq = qt_ref.shape[2]
    nc = kcc_ref.shape[2]
    nsp = ovt_ref.shape[0]
    t0 = pl.program_id(1) * tq
    wide = N_REP * tq
    qpos = t0 + (lax.broadcasted_iota(jnp.int32, (1, wide), 1) & (tq - 1))
    cmp_end = lax.broadcasted_iota(jnp.int32, (nc, 1), 0) * STRIDE + (L_CMP - 1)
    cmask = cmp_end <= qpos
    blk = lax.broadcasted_iota(jnp.int32, (nsp, tq), 0)
    qpos_t = t0 + lax.broadcasted_iota(jnp.int32, (nsp, tq), 1)
    ovt = ovt_ref[...]
    for g in range(N_KV):
        heads = range(g * N_REP, (g + 1) * N_REP)
        q = jnp.concatenate([qt_ref[0, _head_rows(h), :] for h in heads], axis=1)
        lm = jnp.where(cmask, _dot(kcc_ref[0, g], q), NEG)
        e = jnp.exp2(lm - jnp.max(lm, axis=0, keepdims=True))
        pc = jnp.where(cmask, e * (1.0 / jnp.sum(e, axis=0, keepdims=True)), 0.0)
        o = _dot(vcct_ref[0, _head_rows(g), :], pc.astype(BF16))
        pg = jnp.zeros((nc, tq), F32)
        for r, h in enumerate(heads):
            ocg_ref[0, _head_rows(h), :] = o[:, r * tq:(r + 1) * tq] * gt_ref[0, 3 * h:3 * h + 1, :]
            pg = pg + pc[:, r * tq:(r + 1) * tq]
        hi, lo = _split_bf16(pg)
        imp_t = _dot(ovt, hi) + _dot(ovt, lo)
        sel = _select_blocks(imp_t, blk, qpos_t, n_sel, min(N_TOP, n_sel))
        bias_ref[0, g] = jnp.where(sel, 0.0, NEG).astype(BF16)


def _cmp_topk(q_t, kcc_gm, vcc_t, gates_t, b, n_cmp, n_sel, nsp):
    tq = q_t.shape[2]
    n_t = q_t.shape[0] // b
    nc = kcc_gm.shape[2]
    ovt = _overlap_t(nsp, nc, n_sel, n_cmp)
    tile = lambda rows: pl.BlockSpec((1, rows, tq), lambda bi, i: (bi * n_t + i, 0, 0))
    return pl.pallas_call(
        functools.partial(_cmp_topk_body, n_sel),
        grid=(b, n_t),
        in_specs=[tile(Q_DIM),
                  pl.BlockSpec((1, N_KV, nc, HEAD_DIM), lambda bi, i: (bi, 0, 0, 0)),
                  pl.BlockSpec((1, KV_DIM, nc), lambda bi, i: (bi, 0, 0)),
                  tile(LANES), _const_spec((nsp, nc))],
        out_specs=[tile(Q_DIM), pl.BlockSpec((1, N_KV, nsp, tq), lambda bi, i: (bi, 0, 0, i))],
        out_shape=[jax.ShapeDtypeStruct((b * n_t, Q_DIM, tq), F32),
                   jax.ShapeDtypeStruct((b, N_KV, nsp, n_t * tq), BF16)],
        compiler_params=_params(("parallel", "parallel")),
        name="cmp_topk",
    )(q_t, kcc_gm, vcc_t, gates_t, ovt)


def _flash_init(m_ref, l_ref, acc_ref):
    m_ref[...] = jnp.full(m_ref.shape, NEG, F32)
    l_ref[...] = jnp.zeros(l_ref.shape, F32)
    acc_ref[...] = jnp.zeros(acc_ref.shape, F32)


def _flash_t(state, s, vt):
    m, acc = state
    m_new = jnp.maximum(m, jnp.max(s, axis=0, keepdims=True))
    return m_new, jnp.exp2(m - m_new) * acc + _dot(vt, jnp.exp2(s - m_new).astype(BF16))


def _v_rows(branch, g):
    base = (branch * N_KV + g) * V_ROWS
    return slice(base, base + V_ROWS)


def _sel_win_body(qrt_ref, bias_ref, kp_ref, kw_ref, vt_ref, ocg_ref, gt_ref, w_ref, x_ref, g1_ref,
                  o_ref, ot_ref):
    tq = qrt_ref.shape[2]
    kc = tq
    qt = pl.program_id(1)
    wide = N_REP * tq
    krow = lax.broadcasted_iota(jnp.int32, (kc, wide), 0)
    qcol = lax.broadcasted_iota(jnp.int32, (kc, wide), 1) & (tq - 1)
    n_back = WINDOW // kc
    fresh = (jnp.full((1, wide), NEG, F32), jnp.zeros((V_ROWS, wide), F32))

    for g in range(N_KV):
        heads = range(g * N_REP, (g + 1) * N_REP)
        qw = jnp.concatenate([qrt_ref[0, _head_rows(h), :] for h in heads], axis=1)
        qs = jnp.concatenate([qw, jnp.concatenate([bias_ref[0, g]] * N_REP, axis=1)], axis=0)
        qw = jnp.concatenate([qw, jnp.zeros_like(qw)], axis=0)

        def scores(j):
            return _dot(kp_ref[g, pl.ds(pl.multiple_of(j * kc, kc), kc), :], qs)

        def sel_chunk(j, state, diagonal):
            s = scores(j)
            return _flash_t(state, jnp.where(krow <= qcol, s, NEG) if diagonal else s, vt_ref[j, _v_rows(0, g), :])

        def sel_run(j0, n, state):
            ahead = [scores(j0), scores(j0 + 1)] if n > 1 else [scores(j0)]
            for k in range(n):
                s = ahead.pop(0)
                state = _flash_t(state, s, vt_ref[j0 + k, _v_rows(0, g), :])
                if k + 2 < n:
                    ahead.append(scores(j0 + k + 2))
            return state

        unroll = 4
        state = lax.fori_loop(0, qt // unroll, lambda i, st: sel_run(unroll * i, unroll, st), fresh)
        rest = qt % unroll
        done = qt - rest
        state = lax.cond(rest >= 2, lambda st: sel_run(done, 2, st), lambda st: st, state)
        state = lax.cond(rest % 2 == 1, lambda st: sel_chunk(qt - 1, st, False), lambda st: st, state)

        def win_scores(back):
            s = _dot(kw_ref[g, pl.ds(pl.multiple_of((qt - back) * kc, kc), kc), :], qw)
            if back == 0:
                return jnp.where(krow <= qcol, s, NEG)
            return jnp.where(krow > qcol, s, NEG) if back == n_back else s

        def tail(k, sel_state):
            ahead = [jnp.where(krow <= qcol, scores(qt), NEG), win_scores(0)]
            _, acc_s = _flash_t(sel_state, ahead.pop(0), vt_ref[qt, _v_rows(0, g), :])
            win_state = fresh
            for back in range(k + 1):
                if back + 1 <= k:
                    ahead.append(win_scores(back + 1))
                win_state = _flash_t(win_state, ahead.pop(0), vt_ref[qt - back, _v_rows(1, g), :])
            return acc_s, win_state[1]

        run = functools.partial(tail, 0)
        for k in range(1, n_back + 1):
            run = functools.partial(lambda k, fewer, st: lax.cond(qt >= k, functools.partial(tail, k), fewer, st),
                                    k, run)
        acc_sel, acc_win = run(state)

        den = slice(HEAD_DIM, HEAD_DIM + 1)
        for r, h in enumerate(heads):
            cols = slice(r * tq, (r + 1) * tq)
            ot_ref[_head_rows(h), :] = (
                ocg_ref[0, _head_rows(h), :]
                + acc_sel[:HEAD_DIM, cols] * (gt_ref[0, 3 * h + 1:3 * h + 2, :] * (1.0 / acc_sel[den, cols]))
                + acc_win[:HEAD_DIM, cols] * (gt_ref[0, 3 * h + 2:3 * h + 3, :] * (1.0 / acc_win[den, cols])))

    m = _dot(ot_ref[...].T.astype(BF16), w_ref[...])
    o_ref[...] = x_ref[...] + _rms(m, g1_ref[...])


def _sel_win_out(qr_t, bias_t, k_sel, k_win, v_t, ocg_t, gates_t, w_out_bf, x2d, g1, b):
    tq = qr_t.shape[2]
    n_t = qr_t.shape[0] // b
    t = n_t * tq
    nsp = bias_t.shape[2]
    assert WINDOW % tq == 0 and HEAD_DIM + nsp == LANES
    tile = lambda rows: pl.BlockSpec((1, rows, tq), lambda bi, i: (bi * n_t + i, 0, 0))
    keys = pl.BlockSpec((N_KV, t, LANES), lambda bi, i: (0, bi, 0))
    xrow = pl.BlockSpec((tq, D_MODEL), lambda bi, i: (bi * n_t + i, 0))
    return pl.pallas_call(
        _sel_win_body,
        grid=(b, n_t),
        in_specs=[tile(Q_DIM),
                  pl.BlockSpec((1, N_KV, nsp, tq), lambda bi, i: (bi, 0, 0, i)),
                  keys, keys,
                  pl.BlockSpec((n_t, _VT_ROWS, tq), lambda bi, i: (bi, 0, 0)),
                  tile(Q_DIM), tile(LANES),
                  _const_spec((Q_DIM, D_MODEL)), xrow, _const_spec((1, D_MODEL))],
        out_specs=xrow,
        out_shape=jax.ShapeDtypeStruct((b * t, D_MODEL), F32),
        scratch_shapes=[pltpu.VMEM((Q_DIM, tq), F32)],
        compiler_params=_params(("parallel", "parallel")),
        name="sel_win_out",
    )(qr_t, bias_t, k_sel, k_win, v_t, ocg_t, gates_t, w_out_bf, x2d, g1)


def _nsa_out_body(oc_ref, os_ref, ow_ref, gate_ref, e_ref, w_ref, x_ref, g1_ref, o_ref):
    hi, lo = _split_bf16(gate_ref[...])
    o = jnp.zeros(oc_ref.shape, F32)
    for c, src in enumerate((oc_ref, os_ref, ow_ref)):
        o = o + (_dot(hi, e_ref[c]) + _dot(lo, e_ref[c])) * src[...]
    m = _dot(o.astype(BF16), w_ref[...])
    o_ref[...] = x_ref[...] + _rms(m, g1_ref[...])


def _gate_expand():
    e = np.zeros((3, LANES, Q_DIM), np.float32)
    for h in range(N_HEADS):
        for c in range(3):
            e[c, h * 3 + c, h * HEAD_DIM:(h + 1) * HEAD_DIM] = 1.0
    return jnp.asarray(e, dtype=BF16)


def _nsa_out(o_cmp, o_sel, o_win, gates, w_bf, x2d, g1):
    n = x2d.shape[0]
    tm = _pick(n, (256, 128, 64, 32, 16, 8))
    row = lambda w: pl.BlockSpec((tm, w), lambda i: (i, 0))
    return pl.pallas_call(
        _nsa_out_body,
        grid=(n // tm,),
        in_specs=[row(Q_DIM), row(Q_DIM), row(Q_DIM), row(LANES), _const_spec((3, LANES, Q_DIM)),
                  _const_spec((Q_DIM, D_MODEL)), row(D_MODEL), _const_spec((1, D_MODEL))],
        out_specs=row(D_MODEL),
        out_shape=jax.ShapeDtypeStruct((n, D_MODEL), F32),
        compiler_params=_params(("parallel",)),
        name="nsa_out",
    )(o_cmp, o_sel, o_win, gates, _gate_expand(), w_bf, x2d, g1)


class _PageRing:
    def __init__(self, pt_ref, cache_hbm, buf_ref, sem_ref, n_b, n_c):
        self.pt, self.cache, self.buf, self.sem = pt_ref, cache_hbm, buf_ref, sem_ref
        self.n_b, self.n_c, self.pg = n_b, n_c, buf_ref.shape[1]

    def _copies(self, step):
        b, c, slot = step // self.n_c, step % self.n_c, step % 2
        return [pltpu.make_async_copy(self.cache.at[self.pt[b, c * self.pg + p]], self.buf.at[slot, p],
                                      self.sem.at[slot]) for p in range(self.pg)]

    def acquire(self, b, c):
        step = b * self.n_c + c

        @pl.when(step == 0)
        def _():
            for cp in self._copies(step):
                cp.start()

        @pl.when(step + 1 < self.n_b * self.n_c)
        def _():
            for cp in self._copies(step + 1):
                cp.start()

        for cp in self._copies(step):
            cp.wait()
        return step % 2


def _compress_paged_body(n_b, n_c, pt_ref, cache_hbm, perm_ref, xl_ref, wk_ref, wv_ref, pek_ref, pev_ref,
                         w1k_ref, w1v_ref, w2k_ref, w2v_ref, kcc_ref, vcc_ref, buf_ref, sem_ref, xs_ref,
                         abk_ref, abv_ref):
    pg, page = buf_ref.shape[1], buf_ref.shape[4]
    rows = pg * page // STRIDE
    b, c = pl.program_id(0), pl.program_id(1)
    slot = _PageRing(pt_ref, cache_hbm, buf_ref, sem_ref, n_b, n_c).acquire(b, c)
    r0 = pl.multiple_of(c * rows, rows)
    for kv, (ab_ref, w_ref) in enumerate(((abk_ref, wk_ref), (abv_ref, wv_ref))):
        _pages_first_layer(lambda p: buf_ref[slot, p, kv], pg, perm_ref, xs_ref, ab_ref, w_ref, r0)

    @pl.when(c == n_c - 1)
    def _():
        _compress_finish(xl_ref, ((abk_ref, wk_ref, pek_ref, w1k_ref, w2k_ref, kcc_ref),
                                  (abv_ref, wv_ref, pev_ref, w1v_ref, w2v_ref, vcc_ref)), kcc_ref.shape[1])


def _compress_paged(page_table, cache_fm, x_last, wk, wv, pek, pev, w1k, w1v, w2k, w2v):
    b, n_pages = page_table.shape
    page = cache_fm.shape[3]
    pg = _pick(n_pages, (16, 8, 4, 2, 1))
    n_c = n_pages // pg
    n = n_pages * page // STRIDE
    flat = L_CMP * HEAD_DIM
    out = jax.ShapeDtypeStruct((b, n, KV_DIM), F32)
    const = _const_spec
    per_page = page // STRIDE
    return pl.pallas_call(
        functools.partial(_compress_paged_body, b, n_c),
        grid_spec=pltpu.PrefetchScalarGridSpec(
            num_scalar_prefetch=1,
            grid=(b, n_c),
            in_specs=[pl.BlockSpec(memory_space=pl.ANY),
                      const((page, page)),
                      pl.BlockSpec((1, 8, _CHUNK_W), lambda bi, c, pt: (bi, 0, 0)),
                      const(_W1_SHAPE), const(_W1_SHAPE),
                      const((flat, 1)), const((flat, 1)), const((flat, CMP_HID)), const((flat, CMP_HID)),
                      const((N_KV * CMP_HID, KV_DIM)), const((N_KV * CMP_HID, KV_DIM))],
            out_specs=[pl.BlockSpec((1, n, KV_DIM), lambda bi, c, pt: (bi, 0, 0))] * 2,
            scratch_shapes=[pltpu.VMEM((2, pg, 2, KV_DIM, page), F32), pltpu.SemaphoreType.DMA((2,)),
                            pltpu.VMEM((STRIDE, pg * per_page, KV_DIM), F32),
                            pltpu.VMEM((n + 8, _AB_W), F32), pltpu.VMEM((n + 8, _AB_W), F32)]),
        out_shape=[out, out],
        compiler_params=_params(("arbitrary", "arbitrary")),
        name="compress_paged",
    )(page_table, cache_fm, _chunk_sort_matrix(page), x_last, wk, wv, pek, pev, w1k, w1v, w2k, w2v)


def _softmax_rows(s):
    e = jnp.exp(s - jnp.max(s, axis=-1, keepdims=True))
    return e / jnp.sum(e, axis=-1, keepdims=True)


def _sample_cmp_body(n_sel, past_len, t_new, q_ref, kcc_ref, vcc_ref, ovt_ref, rep_ref, o_ref, sel_ref):
    rows = q_ref.shape[1]
    nc = kcc_ref.shape[1]
    nsp = ovt_ref.shape[0]
    gq = N_KV * t_new
    tq_col = lax.broadcasted_iota(jnp.int32, (rows, 1), 0) & (t_new - 1)
    cmp_end = lax.broadcasted_iota(jnp.int32, (1, nc), 1) * STRIDE + (L_CMP - 1)
    cmask = cmp_end <= past_len + tq_col
    lm = jnp.where(cmask, _dot_nt(q_ref[0], kcc_ref[0].astype(BF16)), NEG)
    pc = jnp.where(cmask, _softmax_rows(lm), 0.0)
    o_ref[0] = _dot(pc.astype(BF16), vcc_ref[0].astype(BF16))
    pg = pc.reshape(N_KV, N_REP, t_new, nc).sum(axis=1).reshape(gq, nc)
    hi, lo = _split_bf16(pg)
    imp_t = _dot_nt(ovt_ref[...], hi) + _dot_nt(ovt_ref[...], lo)
    blk = lax.broadcasted_iota(jnp.int32, (nsp, gq), 0)
    qpos = past_len + (lax.broadcasted_iota(jnp.int32, (nsp, gq), 1) & (t_new - 1))
    sel = _select_blocks(imp_t, blk, qpos, n_sel, min(N_TOP, n_sel))
    sel_ref[0] = _dot_nt(rep_ref[...], jnp.where(sel, 1.0, 0.0).astype(BF16)).astype(BF16)


def _sample_cmp(q_bd, kcc, vcc, n_cmp, n_sel, nsp, past_len, t_new):
    b, rows, _ = q_bd.shape
    nc = kcc.shape[1]
    gq = N_KV * t_new
    ovt = _overlap_t(nsp, nc, n_sel, n_cmp)
    rep = np.zeros((rows, gq), np.float32)
    for h in range(N_HEADS):
        for t in range(t_new):
            rep[h * t_new + t, (h // N_REP) * t_new + t] = 1.0
    return pl.pallas_call(
        functools.partial(_sample_cmp_body, n_sel, past_len, t_new),
        grid=(b,),
        in_specs=[pl.BlockSpec((1, rows, KV_DIM), lambda bi: (bi, 0, 0)),
                  pl.BlockSpec((1, nc, KV_DIM), lambda bi: (bi, 0, 0)),
                  pl.BlockSpec((1, nc, KV_DIM), lambda bi: (bi, 0, 0)),
                  _const_spec((nsp, nc)), _const_spec((rows, gq))],
        out_specs=[pl.BlockSpec((1, rows, KV_DIM), lambda bi: (bi, 0, 0)),
                   pl.BlockSpec((1, rows, nsp), lambda bi: (bi, 0, 0))],
        out_shape=[jax.ShapeDtypeStruct((b, rows, KV_DIM), F32), jax.ShapeDtypeStruct((b, rows, nsp), BF16)],
        compiler_params=_params(("parallel",)),
        name="sample_cmp",
    )(q_bd, kcc, vcc, ovt, jnp.asarray(rep, dtype=BF16))


def _flash_update(s, pv_fn, m_ref, l_ref, acc_ref):
    m_prev = m_ref[...]
    m_new = jnp.maximum(m_prev, jnp.max(s, axis=-1, keepdims=True))
    alpha = jnp.exp(m_prev - m_new)
    p = jnp.exp(s - m_new)
    l_ref[...] = alpha * l_ref[...] + jnp.sum(p, axis=-1, keepdims=True)
    acc_ref[...] = alpha * acc_ref[...] + pv_fn(p.astype(BF16))
    m_ref[...] = m_new


def _sample_sel_body(t_new, n_b, n_c, pt_ref, cache_hbm, q_ref, sel_ref, e_ref, et_ref, tail_ref, o_ref,
                     buf_ref, sem_ref, m_ref, l_ref, acc_ref):
    rows = q_ref.shape[1]
    pg, page = buf_ref.shape[1], buf_ref.shape[4]
    b, c = pl.program_id(0), pl.program_id(1)
    q = q_ref[0]

    @pl.when(c == 0)
    def _():
        _flash_init(m_ref, l_ref, acc_ref)

    @pl.when(c < n_c)
    def _():
        slot = _PageRing(pt_ref, cache_hbm, buf_ref, sem_ref, n_b, n_c).acquire(b, c)
        halves = [range(0, pg // 2), range(pg // 2, pg)] if pg > 1 else [range(pg)]

        def scores(pages):
            s = jnp.concatenate([_dot(q, buf_ref[slot, p, 0].astype(BF16)) for p in pages], axis=1)
            ok = _dot(sel_ref[0], e_ref[:, pages.start * page:pages.stop * page]) > 0.5
            return jnp.where(ok, s, NEG)

        for pages, s in [(pages, scores(pages)) for pages in halves]:
            def pv(p_bf, pages=pages):
                return sum(_dot_nt(p_bf[:, i * page:(i + 1) * page], buf_ref[slot, p, 1].astype(BF16))
                           for i, p in enumerate(pages))

            _flash_update(s, pv, m_ref, l_ref, acc_ref)

    @pl.when(c == n_c)
    def _():
        kv = tail_ref[0]
        nk = kv.shape[0]
        s = _dot_nt(q, kv[:, :KV_DIM].astype(BF16))
        tq = lax.broadcasted_iota(jnp.int32, (rows, 1), 0) & (t_new - 1)
        ok = (_dot(sel_ref[0], et_ref[...]) > 0.5) & (lax.broadcasted_iota(jnp.int32, (1, nk), 1) <= tq)
        _flash_update(jnp.where(ok, s, NEG), lambda p_bf: _dot(p_bf, kv[:, KV_DIM:].astype(BF16)),
                      m_ref, l_ref, acc_ref)
        o_ref[0] = acc_ref[...] / l_ref[...]


def _sample_sel(page_table, cache_fm, q_bd, sel01, tail_sel, n_sel, t_new):
    b, rows, _ = q_bd.shape
    nsp = sel01.shape[2]
    n_pages = page_table.shape[1]
    page = cache_fm.shape[3]
    pg = _pick(n_pages, (16, 8, 4, 2, 1))
    n_c = n_pages // pg
    kc = pg * page
    nk = tail_sel.shape[1]
    key_blk = np.arange(n_pages * page) // L_SEL
    e = (np.arange(nsp)[:, None] == key_blk[None, :]).astype(np.float32)
    et = np.zeros((nsp, nk), np.float32)
    et[n_sel - 1, :] = 1.0
    return pl.pallas_call(
        functools.partial(_sample_sel_body, t_new, b, n_c),
        grid_spec=pltpu.PrefetchScalarGridSpec(
            num_scalar_prefetch=1,
            grid=(b, n_c + 1),
            in_specs=[pl.BlockSpec(memory_space=pl.ANY),
                      pl.BlockSpec((1, rows, KV_DIM), lambda bi, c, pt: (bi, 0, 0)),
                      pl.BlockSpec((1, rows, nsp), lambda bi, c, pt: (bi, 0, 0)),
                      pl.BlockSpec((nsp, kc), lambda bi, c, pt: (0, jnp.minimum(c, n_c - 1))),
                      _const_spec((nsp, nk)),
                      pl.BlockSpec((1, nk, 2 * KV_DIM), lambda bi, c, pt: (bi, 0, 0))],
            out_specs=pl.BlockSpec((1, rows, KV_DIM), lambda bi, c, pt: (bi, 0, 0)),
            scratch_shapes=[pltpu.VMEM((2, pg, 2, KV_DIM, page), F32), pltpu.SemaphoreType.DMA((2,)),
                            pltpu.VMEM((rows, 1), F32), pltpu.VMEM((rows, 1), F32),
                            pltpu.VMEM((rows, KV_DIM), F32)]),
        out_shape=jax.ShapeDtypeStruct((b, rows, KV_DIM), F32),
        compiler_params=_params(("arbitrary", "arbitrary")),
        name="sample_sel",
    )(page_table, cache_fm, q_bd, sel01, jnp.asarray(e, dtype=BF16), jnp.asarray(et, dtype=BF16), tail_sel)


def _sample_win_body(t_new, q_ref, kv_ref, tail_ref, o_ref):
    rows = q_ref.shape[1]
    wb = kv_ref.shape[3]
    nk = tail_ref.shape[1]
    q = q_ref[0]
    tq = lax.broadcasted_iota(jnp.int32, (rows, 1), 0) & (t_new - 1)
    tail = tail_ref[0]
    s_old = jnp.where(lax.broadcasted_iota(jnp.int32, (1, wb), 1) > tq - WINDOW + wb,
                      _dot(q, kv_ref[0, 0].astype(BF16)), NEG)
    s_new = jnp.where(lax.broadcasted_iota(jnp.int32, (1, nk), 1) <= tq,
                      _dot_nt(q, tail[:, :KV_DIM].astype(BF16)), NEG)
    m = jnp.maximum(jnp.max(s_old, axis=-1, keepdims=True), jnp.max(s_new, axis=-1, keepdims=True))
    p_old = jnp.exp(s_old - m)
    p_new = jnp.exp(s_new - m)
    den = jnp.sum(p_old, axis=-1, keepdims=True) + jnp.sum(p_new, axis=-1, keepdims=True)
    o = _dot_nt(p_old.astype(BF16), kv_ref[0, 1].astype(BF16)) + _dot(p_new.astype(BF16), tail[:, KV_DIM:].astype(BF16))
    o_ref[0] = o / den


def _sample_win(q_bd, win_fm, tail_win, t_new):
    b, rows, _ = q_bd.shape
    wb = win_fm.shape[3]
    nk = tail_win.shape[1]
    return pl.pallas_call(
        functools.partial(_sample_win_body, t_new),
        grid=(b,),
        in_specs=[pl.BlockSpec((1, rows, KV_DIM), lambda bi: (bi, 0, 0)),
                  pl.BlockSpec((1, 2, KV_DIM, wb), lambda bi: (bi, 0, 0, 0)),
                  pl.BlockSpec((1, nk, 2 * KV_DIM), lambda bi: (bi, 0, 0))],
        out_specs=pl.BlockSpec((1, rows, KV_DIM), lambda bi: (bi, 0, 0)),
        out_shape=jax.ShapeDtypeStruct((b, rows, KV_DIM), F32),
        compiler_params=_params(("parallel",)),
        name="sample_win",
    )(q_bd, win_fm, tail_win)


def _heads_major(a2d, b, t, n):
    return a2d.reshape(b, t, n, HEAD_DIM).transpose(0, 2, 1, 3)


def _tokens_major(a_hm):
    b, n, t, d = a_hm.shape
    return a_hm.transpose(0, 2, 1, 3).reshape(b * t, n * d)


def _block_diag_q(q2d, b, t):
    q_hm = _heads_major(q2d, b, t, N_HEADS)
    onehot = jnp.asarray(np.eye(N_KV)[np.arange(N_HEADS) // N_REP], dtype=q2d.dtype)
    return jnp.einsum('bhtd,hg->bhtgd', q_hm, onehot).reshape(b, N_HEADS * t, KV_DIM)


def _own_group(o_bd, b, t):
    o = o_bd.reshape(b, N_KV, N_REP, t, N_KV, HEAD_DIM)
    o = jnp.stack([o[:, g, :, :, g, :] for g in range(N_KV)], axis=1)
    return o.transpose(0, 3, 1, 2, 4).reshape(b * t, Q_DIM)


def _nsa_prompt(x2d, b, t, g0, g1, w_in_bf, w_out_bf, cw):
    pos = jnp.arange(t)
    q_t, qr_t, v_t, gates_t, cmp_fm, sel_fm, win_fm, k_sel, k_win = _nsa_proj(x2d, g0, w_in_bf, pos)
    assert t % STRIDE == 0 and t % L_SEL == 0
    n_ch = t // STRIDE
    n_cmp = n_ch - 1
    n_sel = t // L_SEL
    nsp = -(-n_sel // HEAD_DIM) * HEAD_DIM
    x_last = jnp.zeros((b, 8, _CHUNK_W), F32)
    (wk, pek, w1k, w2k), (wv, pev, w1v, w2v) = cw
    kcc, vcc = _compress(cmp_fm, x_last, wk, wv, pek, pev, w1k, w1v, w2k, w2v)
    gm = lambda a: a.reshape(b, -1, N_KV, HEAD_DIM).transpose(0, 2, 1, 3).astype(BF16)
    ocg_t, bias_t = _cmp_topk(q_t, gm(kcc), vcc.transpose(0, 2, 1).astype(BF16), gates_t, b, n_cmp, n_sel, nsp)
    x1 = _sel_win_out(qr_t, bias_t, k_sel, k_win, v_t, ocg_t, gates_t, w_out_bf, x2d, g1, b)
    rows5 = lambda a: a.reshape(b, 2, N_KV, HEAD_DIM, -1).transpose(0, 4, 1, 2, 3)[None]
    n_win = min(WINDOW, t)
    return x1, (rows5(cmp_fm), rows5(sel_fm), rows5(win_fm[:, :, t - n_win:]))


def _nsa_sample(x2d, b, t, g0, w_in_bf, cw, cache_cmp_l, cache_sel_l, cache_win_l, page_table):
    n_pages = page_table.shape[1]
    page = cache_cmp_l.shape[1]
    past_len = n_pages * page
    assert page % L_SEL == 0 and page % STRIDE == 0 and t <= STRIDE and t & (t - 1) == 0
    pos = past_len + jnp.arange(t)
    q, qr, cmp_rows, sel_rows, win_rows, gates = _nsa_proj(x2d, g0, w_in_bf, pos)
    row_w = 2 * KV_DIM
    fm = lambda a: a.transpose(0, 2, 3, 4, 1).reshape(a.shape[0], 2, KV_DIM, a.shape[1])
    n_past_ch = past_len // STRIDE
    n_cmp = n_past_ch
    n_sel = past_len // L_SEL + 1
    nsp = -(-n_sel // LANES) * LANES
    new3 = lambda a: a.reshape(b, t, row_w)
    x_last = jnp.pad(new3(cmp_rows), ((0, 0), (0, STRIDE - t), (0, 0))).reshape(b, 1, _CHUNK_W)
    x_last = jnp.pad(x_last, ((0, 0), (0, 7), (0, 0)))
    (wk, pek, w1k, w2k), (wv, pev, w1v, w2v) = cw
    kcc, vcc = _compress_paged(page_table, fm(cache_cmp_l), x_last, wk, wv, pek, pev, w1k, w1v, w2k, w2v)
    o_cmp_bd, sel01 = _sample_cmp(_block_diag_q(q, b, t), kcc, vcc, n_cmp, n_sel, nsp, past_len, t)
    qr_bd = _block_diag_q(qr, b, t)
    tail = lambda a: jnp.pad(new3(a), ((0, 0), (0, LANES - t), (0, 0)))
    o_sel_bd = _sample_sel(page_table, fm(cache_sel_l), qr_bd, sel01, tail(sel_rows), n_sel, t)
    win_fm = fm(cache_win_l)
    o_win_bd = _sample_win(qr_bd, win_fm, tail(win_rows), t)
    w_buf = win_fm.shape[3]
    new_fm = new3(win_rows).reshape(b, t, 2, KV_DIM).transpose(0, 2, 3, 1)
    new_win = jnp.concatenate([win_fm, new_fm], axis=3)[..., -w_buf:]
    new_win = new_win.reshape(b, 2, N_KV, HEAD_DIM, w_buf).transpose(0, 4, 1, 2, 3)
    rows5 = lambda a: a.reshape(1, b, -1, 2, N_KV, HEAD_DIM)
    caches = (rows5(cmp_rows), rows5(sel_rows), new_win[None])
    return _own_group(o_cmp_bd, b, t), _own_group(o_sel_bd, b, t), _own_group(o_win_bd, b, t), gates, caches


def kernel(x_prompt, x_sample, cache_cmp, cache_sel, cache_win, state_conv, page_table, p_prompt, p_sample,
           norm_g, w_ff1, w_ff2, w_ple, w_ple_gate, b_ple_gate,
           conv_w_pw1, conv_b_pw1, conv_w_dw, conv_b_dw, conv_ln_g, conv_ln_b, conv_w_pw2, conv_b_pw2,
           nsa_w_in, nsa_w_out, nsa_cmp_pe, nsa_cmp_w1, nsa_cmp_w2):
    depth = norm_g.shape[0]
    bf = lambda a: a.astype(BF16)
    row = lambda a: a.reshape(1, -1)
    w_ff1_bf, w_ff2_bf, w_ple_bf, w_gate_bf = bf(w_ff1), bf(w_ff2), bf(w_ple), bf(w_ple_gate)
    w_pw1_bf, w_pw2_bf, w_out_bf = bf(conv_w_pw1), bf(conv_w_pw2), bf(nsa_w_out)
    w_in_bf = bf(jnp.pad(nsa_w_in, ((0, 0), (0, 0), (0, _PROJ_W - nsa_w_in.shape[2]))))
    w_dw = jnp.pad(conv_w_dw, ((0, 0), (0, CONV_HALO - CONV_W), (0, 0)))

    def run(x, p, sample):
        b, t, _ = x.shape
        n = b * t
        x2d = x.reshape(n, D_MODEL)
        cmp_o, sel_o, win_o, conv_o = [], [], [], []
        for i in range(depth):
            g = lambda j: row(norm_g[i, j])
            if i % 2 == 0:
                c = i // 2
                u = _conv_front(x2d, g(0), w_pw1_bf[c], row(conv_b_pw1[c])).reshape(b, t, D_MODEL)
                if sample:
                    hist = jnp.pad(state_conv[c], ((0, 0), (CONV_HALO - (CONV_W - 1), 0), (0, 0)))
                else:
                    hist = jnp.zeros((b, CONV_HALO, D_MODEL), F32)
                tp = -(-t // CONV_HALO) * CONV_HALO
                pad_t = lambda a: jnp.pad(a, ((0, 0), (0, tp - t), (0, 0))) if tp > t else a
                x1 = _conv_back(hist, pad_t(u), pad_t(x2d.reshape(b, t, D_MODEL)), w_dw[c], row(conv_b_dw[c]),
                                row(conv_ln_g[c]), row(conv_ln_b[c]), w_pw2_bf[c], row(conv_b_pw2[c]),
                                g(1))[:, :t].reshape(n, D_MODEL)
                keep = CONV_W - 1
                conv_o.append(jnp.concatenate([hist, u], axis=1)[:, CONV_HALO + t - keep:][None])
            else:
                a = i // 2
                cw = _compress_weights(nsa_cmp_pe[a], nsa_cmp_w1[a], nsa_cmp_w2[a])
                if sample:
                    oc, osel, ow, gates, caches = _nsa_sample(x2d, b, t, g(0), w_in_bf[a], cw, cache_cmp[a],
                                                              cache_sel[a], cache_win[a], page_table)
                    x1 = _nsa_out(oc, osel, ow, gates, w_out_bf[a], x2d, g(1))
                else:
                    x1, caches = _nsa_prompt(x2d, b, t, g(0), g(1), w_in_bf[a], w_out_bf[a], cw)
                for dst, rows_ in zip((cmp_o, sel_o, win_o), caches):
                    dst.append(rows_)
            x2d = _ffn(x1, g(2), w_ff1_bf[i], w_ff2_bf[i], g(3), w_gate_bf[i], row(b_ple_gate[i]),
                       p.reshape(depth, n, -1), i, w_ple_bf[i])
        cat = lambda parts: jnp.concatenate(parts, axis=0)
        return x2d.reshape(b, t, D_MODEL), cat(cmp_o), cat(sel_o), cat(win_o), cat(conv_o)

    y_p, cmp_p, sel_p, win_p, conv_p = run(x_prompt, p_prompt, False)
    y_s, cmp_s, sel_s, win_s, conv_s = run(x_sample, p_sample, True)
    return (y_p, y_s, cmp_p, cmp_s, sel_p, sel_s, win_p, win_s, conv_p, conv_s)
```

```python
import functools
import math

import numpy as np
import jax
import jax.numpy as jnp
from jax import lax
from jax.experimental import pallas as pl
from jax.experimental.pallas import tpu as pltpu

F32 = jnp.float32
BF16 = jnp.bfloat16

D_MODEL = 1024
N_HEADS = 16
N_KV = 4
N_REP = N_HEADS // N_KV
HEAD_DIM = 64
ROT_DIM = HEAD_DIM // 4
ROPE_THETA = 500000.0
L_CMP = 32
STRIDE = 16
CMP_HID = 2 * HEAD_DIM
L_SEL = 64
N_TOP = 16
WINDOW = 512
CONV_W = 31
Q_DIM = N_HEADS * HEAD_DIM
KV_DIM = N_KV * HEAD_DIM
GATE_DIM = 3 * N_HEADS
D_FF = 4 * D_MODEL
EPS = 1e-6
NEG = -1e30
FORCE = 1e6
SCALE = HEAD_DIM ** -0.5

LANES = 128
CONV_HALO = 32
VMEM_LIMIT = 56 * 1024 * 1024


def _pick(n, cands):
    for c in cands:
        if n % c == 0:
            return c
    raise ValueError(f"no tile in {cands} divides {n}")


def _const_spec(shape):
    nd = len(shape)
    return pl.BlockSpec(shape, lambda *_: (0,) * nd, pipeline_mode=pl.Buffered(1))


def _params(sem):
    return pltpu.CompilerParams(dimension_semantics=sem, vmem_limit_bytes=VMEM_LIMIT)


def _sigmoid(x):
    return 1.0 / (1.0 + jnp.exp(-x))


def _rms(x, g):
    return x * lax.rsqrt(jnp.mean(x * x, axis=-1, keepdims=True) + EPS) * g


def _dot(a, b):
    return jnp.dot(a, b, preferred_element_type=F32)


def _dot_nt(a, b):
    return lax.dot_general(a, b, (((1,), (1,)), ((), ())), preferred_element_type=F32)


def _split_bf16(x):
    hi = x.astype(BF16)
    lo = (x - hi.astype(F32)).astype(BF16)
    return hi, lo


def _conv_front_body(x_ref, g_ref, w_ref, b_ref, u_ref):
    h = _rms(x_ref[...], g_ref[...]).astype(BF16)
    z = _dot(h, w_ref[...]) + b_ref[...]
    u_ref[...] = z[:, :D_MODEL] * _sigmoid(z[:, D_MODEL:])


def _conv_front(x2d, g, w_bf, b):
    n = x2d.shape[0]
    tm = _pick(n, (512, 256, 128, 64, 32, 16, 8))
    return pl.pallas_call(
        _conv_front_body,
        grid=(n // tm,),
        in_specs=[pl.BlockSpec((tm, D_MODEL), lambda i: (i, 0)),
                  _const_spec((1, D_MODEL)),
                  _const_spec((D_MODEL, 2 * D_MODEL)),
                  _const_spec((1, 2 * D_MODEL))],
        out_specs=pl.BlockSpec((tm, D_MODEL), lambda i: (i, 0)),
        out_shape=jax.ShapeDtypeStruct((n, D_MODEL), F32),
        compiler_params=_params(("parallel",)),
        name="conv_front",
    )(x2d, g, w_bf, b)


_CONV_ROWS = 32
_CONV_COLS = 256


def _conv_back_body(hist_ref, prev_ref, main_ref, wdw_ref, bdw_ref, lng_ref, lnb_ref, w2_ref, b2_ref, x_ref, g1_ref,
                    o_ref, win_ref, y_ref):
    tt = main_ref.shape[1]
    @pl.when(pl.program_id(1) == 0)
    def _():
        win_ref[0:CONV_HALO, :] = hist_ref[0]

    @pl.when(pl.program_id(1) > 0)
    def _():
        win_ref[0:CONV_HALO, :] = prev_ref[0]

    win_ref[CONV_HALO:CONV_HALO + tt, :] = main_ref[0]
    first = CONV_HALO - (CONV_W - 1)

    sub = 8
    for r0 in range(0, tt, _CONV_ROWS):
        for c0 in range(0, D_MODEL, _CONV_COLS):
            cols = slice(c0, c0 + _CONV_COLS)
            acc = jnp.zeros((_CONV_ROWS, _CONV_COLS), F32)
            for s in range(sub):
                part = None
                for k in range(CONV_W):
                    if (first + k) % sub != s:
                        continue
                    base = r0 + first + k - s
                    term = win_ref[base:base + _CONV_ROWS + (sub if s else 0), cols] * wdw_ref[k:k + 1, cols]
                    part = term if part is None else part + term
                if part is not None:
                    acc = acc + part[s:s + _CONV_ROWS]
            y_ref[r0:r0 + _CONV_ROWS, cols] = acc + bdw_ref[:, cols]
    y = y_ref[...]
    yc = y - jnp.mean(y, axis=-1, keepdims=True)
    var = jnp.mean(yc * yc, axis=-1, keepdims=True)
    ln = yc * lax.rsqrt(var + EPS) * lng_ref[...] + lnb_ref[...]
    act = (ln * _sigmoid(ln)).astype(BF16)
    m = _dot(act, w2_ref[...]) + b2_ref[...]
    o_ref[0] = x_ref[0] + _rms(m, g1_ref[...])


def _conv_back(hist, u3d, x3d, wdw, bdw, lng, lnb, w2_bf, b2, g1):
    b, t, _ = x3d.shape
    tt = _pick(t, (256, 128, 64, 32))
    halo_blocks = tt // CONV_HALO
    return pl.pallas_call(
        _conv_back_body,
        grid=(b, t // tt),
        in_specs=[pl.BlockSpec((1, CONV_HALO, D_MODEL), lambda bi, i: (bi, 0, 0)),
                  pl.BlockSpec((1, CONV_HALO, D_MODEL), lambda bi, i: (bi, jnp.maximum(i * halo_blocks - 1, 0), 0)),
                  pl.BlockSpec((1, tt, D_MODEL), lambda bi, i: (bi, i, 0)),
                  _const_spec((CONV_HALO, D_MODEL)),
                  _const_spec((1, D_MODEL)), _const_spec((1, D_MODEL)), _const_spec((1, D_MODEL)),
                  _const_spec((D_MODEL, D_MODEL)), _const_spec((1, D_MODEL)),
                  pl.BlockSpec((1, tt, D_MODEL), lambda bi, i: (bi, i, 0)),
                  _const_spec((1, D_MODEL))],
        out_specs=pl.BlockSpec((1, tt, D_MODEL), lambda bi, i: (bi, i, 0)),
        out_shape=jax.ShapeDtypeStruct((b, t, D_MODEL), F32),
        scratch_shapes=[pltpu.VMEM((tt + CONV_HALO, D_MODEL), F32), pltpu.VMEM((tt, D_MODEL), F32)],
        compiler_params=_params(("parallel", "parallel")),
        name="conv_back",
    )(hist, u3d, u3d, wdw, bdw, lng, lnb, w2_bf, b2, x3d, g1)


_FF_CHUNK = 1024


def _ffn_body(x_ref, g2_ref, w1_ref, w2_ref, g3_ref, wg_ref, bg_ref, p_ref, wp_ref, o_ref):
    x = x_ref[...]
    h = _rms(x, g2_ref[...]).astype(BF16)
    f = jnp.zeros(x.shape, F32)
    for c in range(D_FF // _FF_CHUNK):
        a = jnp.maximum(_dot(h, w1_ref[:, c * _FF_CHUNK:(c + 1) * _FF_CHUNK]), 0.0)
        f = f + _dot((a * a).astype(BF16), w2_ref[c * _FF_CHUNK:(c + 1) * _FF_CHUNK, :])
    x2 = x + _rms(f, g3_ref[...])
    gate = _sigmoid(_dot(x2.astype(BF16), wg_ref[...]) + bg_ref[...])
    o_ref[...] = x2 + gate * _dot(p_ref[0].astype(BF16), wp_ref[...])


def _ffn(x2d, g2, w1_bf, w2_bf, g3, wg_bf, bg, p_all, layer, wp_bf):
    n = x2d.shape[0]
    d_ple = p_all.shape[2]
    tm = _pick(n, (512, 256, 128, 64, 32, 16, 8))
    return pl.pallas_call(
        _ffn_body,
        grid=(n // tm,),
        in_specs=[pl.BlockSpec((tm, D_MODEL), lambda i: (i, 0)),
                  _const_spec((1, D_MODEL)),
                  _const_spec((D_MODEL, D_FF)), _const_spec((D_FF, D_MODEL)),
                  _const_spec((1, D_MODEL)),
                  _const_spec((D_MODEL, D_MODEL)), _const_spec((1, D_MODEL)),
                  pl.BlockSpec((1, tm, d_ple), lambda i: (layer, i, 0)),
                  _const_spec((d_ple, D_MODEL))],
        out_specs=pl.BlockSpec((tm, D_MODEL), lambda i: (i, 0)),
        out_shape=jax.ShapeDtypeStruct((n, D_MODEL), F32),
        compiler_params=_params(("parallel",)),
        name="ffn_ple",
    )(x2d, g2, w1_bf, w2_bf, g3, wg_bf, bg, p_all, wp_bf)


_PROJ_W = Q_DIM + 6 * KV_DIM + LANES
LOG2E = 1.4426950408889634
V_ROWS = HEAD_DIM + 16
_VT_ROWS = 2 * N_KV * V_ROWS


def _rope_block(blk, c, su, sd):
    return blk * c + pltpu.roll(blk, ROT_DIM // 2, 1) * su + pltpu.roll(blk, LANES - ROT_DIM // 2, 1) * sd


def _nsa_proj_body(prompt, x_ref, g_ref, w_ref, c_ref, su_ref, sd_ref, *refs):
    if prompt:
        blk_ref, qt_ref, qrt_ref, vt_ref, gatet_ref = refs[:5]
        cache_t_refs, key_refs = refs[5:8], refs[8:10]
        low = lax.broadcasted_iota(jnp.int32, (x_ref.shape[0], LANES), 1) < HEAD_DIM
        ones = jnp.ones((V_ROWS - HEAD_DIM, x_ref.shape[0]), BF16)
    else:
        q_ref, qr_ref, cmp_ref, sel_ref, win_ref, gate_ref = refs
    h = _rms(x_ref[...], g_ref[...]).astype(BF16)
    z = _dot(h, w_ref[...])
    c, su, sd = c_ref[...], su_ref[...], sd_ref[...]
    for i in range(Q_DIM // LANES):
        cols = slice(i * LANES, (i + 1) * LANES)
        qs = z[:, cols] * SCALE
        qrs = _rope_block(z[:, cols], c, su, sd) * SCALE
        if prompt:
            qt_ref[0, cols, :] = (qs * LOG2E).T.astype(BF16)
            qrt_ref[0, cols, :] = (qrs * LOG2E).T.astype(BF16)
        else:
            q_ref[:, cols] = qs.astype(BF16)
            qr_ref[:, cols] = qrs.astype(BF16)
    o = Q_DIM
    if prompt:
        for i in range(2 * KV_DIM // LANES):
            cache_t_refs[0][0, i * LANES:(i + 1) * LANES, :] = z[:, o + i * LANES:o + (i + 1) * LANES].T
    else:
        cmp_ref[...] = z[:, o:o + 2 * KV_DIM]
    o += 2 * KV_DIM
    for n in range(2):
        for i in range(KV_DIM // LANES):
            k_rot = _rope_block(z[:, o + i * LANES:o + (i + 1) * LANES], c, su, sd)
            if not prompt:
                (sel_ref, win_ref)[n][:, i * LANES:(i + 1) * LANES] = k_rot
                continue
            vt = z[:, o + KV_DIM + i * LANES:o + KV_DIM + (i + 1) * LANES].T
            cache_t_refs[1 + n][0, i * LANES:(i + 1) * LANES, :] = k_rot.T
            cache_t_refs[1 + n][0, KV_DIM + i * LANES:KV_DIM + (i + 1) * LANES, :] = vt
            fill = blk_ref[...] if n == 0 else 0.0
            per_tile = LANES // HEAD_DIM
            key_refs[n][per_tile * i] = jnp.where(low, k_rot, fill).astype(BF16)
            key_refs[n][per_tile * i + 1] = jnp.where(low, pltpu.roll(k_rot, HEAD_DIM, 1), fill).astype(BF16)
            for k in range(per_tile):
                base = (n * N_KV + i * per_tile + k) * V_ROWS
                vt_ref[0, base:base + HEAD_DIM, :] = vt[k * HEAD_DIM:(k + 1) * HEAD_DIM].astype(BF16)
                vt_ref[0, base + HEAD_DIM:base + V_ROWS, :] = ones
        if not prompt:
            (sel_ref, win_ref)[n][:, KV_DIM:] = z[:, o + KV_DIM:o + 2 * KV_DIM]
        o += 2 * KV_DIM
    gates = _sigmoid(z[:, o:o + LANES])
    if prompt:
        gatet_ref[0] = gates.T
    else:
        gate_ref[...] = gates


def _rope_tables(pos, rows):
    half = ROT_DIM // 2
    inv = jnp.float32(ROPE_THETA) ** (-jnp.arange(half, dtype=F32) * (2.0 / ROT_DIM))
    ang = pos.astype(F32)[:, None] * inv[None, :]
    lane = np.arange(LANES)
    within = lane % HEAD_DIM
    cos = jnp.cos(ang)[:, lane % half]
    sin = jnp.sin(ang)[:, lane % half]
    c = jnp.where(within[None, :] < ROT_DIM, cos, 1.0)
    su = jnp.where((within[None, :] >= half) & (within[None, :] < ROT_DIM), sin, 0.0)
    sd = jnp.where(within[None, :] < half, -sin, 0.0)
    reps = rows // pos.shape[0]
    return tuple(jnp.tile(a, (reps, 1)) for a in (c, su, sd))


def _nsa_proj(x2d, g, w_bf, pos):
    n = x2d.shape[0]
    t = pos.shape[0]
    tm = _pick(n, (256, 128, 64, 32, 16, 8))
    if t >= tm:
        assert t % tm == 0
        tabs = _rope_tables(pos, t)
        nt = t // tm
        tab_spec = pl.BlockSpec((tm, LANES), lambda i: (i % nt, 0))
    else:
        assert tm % t == 0
        tabs = _rope_tables(pos, tm)
        tab_spec = _const_spec((tm, LANES))
    row = lambda w: pl.BlockSpec((tm, w), lambda i: (i, 0))
    rows_of = lambda w, dt: jax.ShapeDtypeStruct((n, w), dt)
    in_specs = [row(D_MODEL), _const_spec((1, D_MODEL)), _const_spec((D_MODEL, _PROJ_W)),
                tab_spec, tab_spec, tab_spec]
    args = [x2d, g, w_bf, *tabs]
    prompt = t >= tm
    if prompt:
        assert t // L_SEL <= HEAD_DIM
        blk = np.zeros((t, LANES), np.float32)
        blk[np.arange(t), HEAD_DIM + np.arange(t) // L_SEL] = 1.0
        in_specs.append(tab_spec)
        args.append(jnp.asarray(blk))
        tile = lambda rows: pl.BlockSpec((1, rows, tm), lambda i: (i, 0, 0))
        tiles = lambda rows, dt: jax.ShapeDtypeStruct((n // tm, rows, tm), dt)
        out_specs = [tile(Q_DIM), tile(Q_DIM), tile(_VT_ROWS), tile(LANES)]
        out_shape = [tiles(Q_DIM, BF16), tiles(Q_DIM, BF16), tiles(_VT_ROWS, BF16), tiles(LANES, F32)]
        out_specs += [pl.BlockSpec((1, 2 * KV_DIM, tm), lambda i: (i // nt, 0, i % nt))] * 3
        out_shape += [jax.ShapeDtypeStruct((n // t, 2 * KV_DIM, t), F32)] * 3
        out_specs += [pl.BlockSpec((N_KV, tm, LANES), lambda i: (0, i, 0))] * 2
        out_shape += [jax.ShapeDtypeStruct((N_KV, n, LANES), BF16)] * 2
    else:
        out_specs = [row(Q_DIM), row(Q_DIM), row(2 * KV_DIM), row(2 * KV_DIM), row(2 * KV_DIM), row(LANES)]
        out_shape = [rows_of(Q_DIM, BF16), rows_of(Q_DIM, BF16), rows_of(2 * KV_DIM, F32),
                     rows_of(2 * KV_DIM, F32), rows_of(2 * KV_DIM, F32), rows_of(LANES, F32)]
    return pl.pallas_call(
        functools.partial(_nsa_proj_body, prompt),
        grid=(n // tm,),
        in_specs=in_specs,
        out_specs=out_specs,
        out_shape=out_shape,
        compiler_params=_params(("parallel",)),
        name="nsa_proj",
    )(*args)


_CHUNK_W = STRIDE * 2 * KV_DIM
_AB_W = 2 * N_KV * CMP_HID


_S_PER_DOT = 4


def _first_layer(get_tile, w_ref, rows):
    half = LANES // 2
    low = lax.broadcasted_iota(jnp.int32, (rows, LANES), 1) < half
    out = []
    for j in range(KV_DIM // LANES):
        acc = [jnp.zeros((rows, 2 * CMP_HID), F32) for _ in range(2)]
        for sq in range(STRIDE // _S_PER_DOT):
            even, odd = [], []
            for s in range(sq * _S_PER_DOT, (sq + 1) * _S_PER_DOT, 2):
                a, b = get_tile(s, j), get_tile(s + 1, j)
                even.append(jnp.where(low, a, pltpu.roll(b, half, 1)))
                odd.append(jnp.where(low, pltpu.roll(a, half, 1), b))
            acc[0] = acc[0] + _dot(jnp.concatenate(even, axis=1).astype(BF16), w_ref[sq])
            acc[1] = acc[1] + _dot(jnp.concatenate(odd, axis=1).astype(BF16), w_ref[sq])
        out += acc
    return out


def _store_ab(ab_ref, rows_slice, accs):
    for g, acc in enumerate(accs):
        ab_ref[rows_slice, g * CMP_HID:(g + 1) * CMP_HID] = acc[:, :CMP_HID]
        ab_ref[rows_slice, (N_KV + g) * CMP_HID:(N_KV + g + 1) * CMP_HID] = acc[:, CMP_HID:]


def _chunk_rows_tile(src_ref, kv):
    def get(s, j):
        lo = s * 2 * KV_DIM + kv * KV_DIM + j * LANES
        return src_ref[0, :, lo:lo + LANES]
    return get


def _compress_finish(xl_ref, per_kv, n_rows):
    half = _AB_W // 2
    for kv, (ab_ref, w_ref, pe_ref, w1_ref, w2_ref, out_ref) in enumerate(per_kv):
        _store_ab(ab_ref, slice(n_rows, n_rows + 8), _first_layer(_chunk_rows_tile(xl_ref, kv), w_ref, 8))
        c = jnp.sum(pe_ref[...] * w1_ref[...], axis=0, keepdims=True)
        c = jnp.concatenate([c] * N_KV, axis=1)
        pre = ab_ref[0:n_rows, 0:half] + ab_ref[1:n_rows + 1, half:] + c
        hid = (pre * _sigmoid(pre)).astype(BF16)
        out_ref[0] = _dot(hid, w2_ref[...])


def _pages_first_layer(page_of, n_pages, perm_ref, xs_ref, ab_ref, w_ref, r0):
    per_page = perm_ref.shape[0] // STRIDE
    rows = n_pages * per_page
    for p in range(n_pages):
        t = _dot_nt(perm_ref[...], page_of(p).astype(BF16))
        for s in range(STRIDE):
            xs_ref[s, p * per_page:(p + 1) * per_page, :] = t[s * per_page:(s + 1) * per_page]
    tile = lambda s, j: xs_ref[s, :, j * LANES:(j + 1) * LANES]
    _store_ab(ab_ref, pl.ds(r0, rows), _first_layer(tile, w_ref, rows))


def _chunk_sort_matrix(page):
    per_page = page // STRIDE
    assert per_page % 8 == 0
    tok = np.arange(page)
    perm = np.zeros((page, page), np.float32)
    perm[(tok % STRIDE) * per_page + tok // STRIDE, tok] = 1.0
    return jnp.asarray(perm, dtype=BF16)


def _compress_body(perm_ref, x_ref, xl_ref, wk_ref, wv_ref, pek_ref, pev_ref, w1k_ref, w1v_ref, w2k_ref, w2v_ref,
                   kcc_ref, vcc_ref, xs_ref, abk_ref, abv_ref):
    page = perm_ref.shape[0]
    n_pages = x_ref.shape[2] // page
    c = pl.program_id(1)
    rows = n_pages * page // STRIDE
    r0 = pl.multiple_of(c * rows, rows)
    for kv, (ab_ref, w_ref) in enumerate(((abk_ref, wk_ref), (abv_ref, wv_ref))):
        page_of = lambda p: x_ref[0, kv * KV_DIM:(kv + 1) * KV_DIM, p * page:(p + 1) * page]
        _pages_first_layer(page_of, n_pages, perm_ref, xs_ref, ab_ref, w_ref, r0)

    @pl.when(c == pl.num_programs(1) - 1)
    def _():
        _compress_finish(xl_ref, ((abk_ref, wk_ref, pek_ref, w1k_ref, w2k_ref, kcc_ref),
                                  (abv_ref, wv_ref, pev_ref, w1v_ref, w2v_ref, vcc_ref)), kcc_ref.shape[1])


def _compress(x_fm, x_last, wk, wv, pek, pev, w1k, w1v, w2k, w2v):
    b, _, t = x_fm.shape
    page = LANES
    tb = _pick(t, (2048, 1024, 512, 256, 128))
    n = t // STRIDE
    flat = L_CMP * HEAD_DIM
    out = jax.ShapeDtypeStruct((b, n, KV_DIM), F32)
    return pl.pallas_call(
        _compress_body,
        grid=(b, t // tb),
        in_specs=[_const_spec((page, page)),
                  pl.BlockSpec((1, 2 * KV_DIM, tb), lambda bi, i: (bi, 0, i)),
                  pl.BlockSpec((1, 8, _CHUNK_W), lambda bi, i: (bi, 0, 0)),
                  _const_spec(_W1_SHAPE), _const_spec(_W1_SHAPE),
                  _const_spec((flat, 1)), _const_spec((flat, 1)),
                  _const_spec((flat, CMP_HID)), _const_spec((flat, CMP_HID)),
                  _const_spec((N_KV * CMP_HID, KV_DIM)), _const_spec((N_KV * CMP_HID, KV_DIM))],
        out_specs=[pl.BlockSpec((1, n, KV_DIM), lambda bi, i: (bi, 0, 0))] * 2,
        out_shape=[out, out],
        scratch_shapes=[pltpu.VMEM((STRIDE, tb // STRIDE, KV_DIM), F32),
                        pltpu.VMEM((n + 8, _AB_W), F32), pltpu.VMEM((n + 8, _AB_W), F32)],
        compiler_params=_params(("parallel", "arbitrary")),
        name="compress",
    )(_chunk_sort_matrix(page), x_fm, x_last, wk, wv, pek, pev, w1k, w1v, w2k, w2v)


_W1_SHAPE = (STRIDE // _S_PER_DOT, _S_PER_DOT * HEAD_DIM, 2 * CMP_HID)


def _compress_weights(cmp_pe, cmp_w1, cmp_w2):
    eye = jnp.eye(N_KV, dtype=F32)
    outs = []
    for kv in range(2):
        w1 = cmp_w1[kv].reshape(2, STRIDE // _S_PER_DOT, _S_PER_DOT, HEAD_DIM, CMP_HID)
        stacked = w1.transpose(1, 2, 3, 0, 4).reshape(_W1_SHAPE)
        w2 = jnp.einsum('gk,hd->ghkd', eye, cmp_w2[kv]).reshape(N_KV * CMP_HID, KV_DIM)
        outs.append((stacked.astype(BF16), cmp_pe[kv].reshape(L_CMP * HEAD_DIM, 1),
                     cmp_w1[kv].reshape(L_CMP * HEAD_DIM, CMP_HID), w2.astype(BF16)))
    return outs


def _select_blocks(imp_t, blk, qpos, n_real, k):
    cur = jnp.right_shift(qpos, int(math.log2(L_SEL)))
    forced = (blk == 0) | (blk == cur) | (blk == cur - 1)
    future = blk * L_SEL > qpos
    score = jnp.where(future, -jnp.inf, jnp.where(forced, FORCE, imp_t))
    score = jnp.where(blk < n_real, score, -jnp.inf)
    sub = 8
    n_groups = -(-n_real // sub)
    rows = [score[i * sub:(i + 1) * sub] for i in range(n_groups)]
    local = lax.broadcasted_iota(jnp.int32, rows[0].shape, 0)
    rank = [jnp.zeros(r.shape, jnp.int32) for r in rows]
    for j in range(n_real):
        sj = score[j:j + 1, :]
        for n, r in enumerate(rows):
            if n * sub > j:
                beats = sj >= r
            elif n * sub + sub - 1 <= j:
                beats = sj > r
            else:
                beats = jnp.where(local > j - n * sub, jnp.where(sj >= r, 1, 0), jnp.where(sj > r, 1, 0)) > 0
            rank[n] = rank[n] + jnp.where(beats, 1, 0)
    pad = score.shape[0] - n_groups * sub
    if pad:
        rank.append(jnp.full((pad, score.shape[1]), k, jnp.int32))
    return (jnp.concatenate(rank, axis=0) < k) & (blk < n_real)


def _overlap_t(n_sel_pad, n_cmp_pad, n_sel, n_cmp):
    ci = np.arange(n_cmp_pad)[None, :] * STRIDE
    sj = np.arange(n_sel_pad)[:, None] * L_SEL
    ov = (ci < sj + L_SEL) & (ci + L_CMP > sj)
    ov &= (np.arange(n_cmp_pad)[None, :] < n_cmp) & (np.arange(n_sel_pad)[:, None] < n_sel)
    return jnp.asarray(ov, dtype=BF16)


def _head_rows(h):
    return slice(h * HEAD_DIM, (h + 1) * HEAD_DIM)


def _cmp_topk_body(n_sel, qt_ref, kcc_ref, vcct_ref, gt_ref, ovt_ref, ocg_ref, bias_ref):
    tq = qt_ref.shape[2]
    nc = kcc_ref.shape[2]
    nsp = ovt_ref.shape[0]
    t0 = pl.program_id(1) * tq
    wide = N_REP * tq
    qpos = t0 + (lax.broadcasted_iota(jnp.int32, (1, wide), 1) & (tq - 1))
    cmp_end = lax.broadcasted_iota(jnp.int32, (nc, 1), 0) * STRIDE + (L_CMP - 1)
    cmask = cmp_end <= qpos
    blk = lax.broadcasted_iota(jnp.int32, (nsp, tq), 0)
    qpos_t = t0 + lax.broadcasted_iota(jnp.int32, (nsp, tq), 1)
    ovt = ovt_ref[...]
    for g in range(N_KV):
        heads = range(g * N_REP, (g + 1) * N_REP)
        q = jnp.concatenate([qt_ref[0, _head_rows(h), :] for h in heads], axis=1)
        lm = jnp.where(cmask, _dot(kcc_ref[0, g], q), NEG)
        e = jnp.exp2(lm - jnp.max(lm, axis=0, keepdims=True))
        pc = jnp.where(cmask, e * (1.0 / jnp.sum(e, axis=0, keepdims=True)), 0.0)
        o = _dot(vcct_ref[0, _head_rows(g), :], pc.astype(BF16))
        pg = jnp.zeros((nc, tq), F32)
        for r, h in enumerate(heads):
            ocg_ref[0, _head_rows(h), :] = o[:, r * tq:(r + 1) * tq] * gt_ref[0, 3 * h:3 * h + 1, :]
            pg = pg + pc[:, r * tq:(r + 1) * tq]
        hi, lo = _split_bf16(pg)
        imp_t = _dot(ovt, hi) + _dot(ovt, lo)
        sel = _select_blocks(imp_t, blk, qpos_t, n_sel, min(N_TOP, n_sel))
        bias_ref[0, g] = jnp.where(sel, 0.0, NEG).astype(BF16)


def _cmp_topk(q_t, kcc_gm, vcc_t, gates_t, b, n_cmp, n_sel, nsp):
    tq = q_t.shape[2]
    n_t = q_t.shape[0] // b
    nc = kcc_gm.shape[2]
    ovt = _overlap_t(nsp, nc, n_sel, n_cmp)
    tile = lambda rows: pl.BlockSpec((1, rows, tq), lambda bi, i: (bi * n_t + i, 0, 0))
    return pl.pallas_call(
        functools.partial(_cmp_topk_body, n_sel),
        grid=(b, n_t),
        in_specs=[tile(Q_DIM),
                  pl.BlockSpec((1, N_KV, nc, HEAD_DIM), lambda bi, i: (bi, 0, 0, 0)),
                  pl.BlockSpec((1, KV_DIM, nc), lambda bi, i: (bi, 0, 0)),
                  tile(LANES), _const_spec((nsp, nc))],
        out_specs=[tile(Q_DIM), pl.BlockSpec((1, N_KV, nsp, tq), lambda bi, i: (bi, 0, 0, i))],
        out_shape=[jax.ShapeDtypeStruct((b * n_t, Q_DIM, tq), F32),
                   jax.ShapeDtypeStruct((b, N_KV, nsp, n_t * tq), BF16)],
        compiler_params=_params(("parallel", "parallel")),
        name="cmp_topk",
    )(q_t, kcc_gm, vcc_t, gates_t, ovt)


def _flash_init(m_ref, l_ref, acc_ref):
    m_ref[...] = jnp.full(m_ref.shape, NEG, F32)
    l_ref[...] = jnp.zeros(l_ref.shape, F32)
    acc_ref[...] = jnp.zeros(acc_ref.shape, F32)


def _flash_t(state, s, vt):
    m, acc = state
    m_new = jnp.maximum(m, jnp.max(s, axis=0, keepdims=True))
    return m_new, jnp.exp2(m - m_new) * acc + _dot(vt, jnp.exp2(s - m_new).astype(BF16))


def _v_rows(branch, g):
    base = (branch * N_KV + g) * V_ROWS
    return slice(base, base + V_ROWS)


def _sel_win_body(qrt_ref, bias_ref, kp_ref, kw_ref, vt_ref, ocg_ref, gt_ref, w_ref, x_ref, g1_ref,
                  o_ref, ot_ref):
    tq = qrt_ref.shape[2]
    kc = tq
    qt = pl.program_id(1)
    wide = N_REP * tq
    krow = lax.broadcasted_iota(jnp.int32, (kc, wide), 0)
    qcol = lax.broadcasted_iota(jnp.int32, (kc, wide), 1) & (tq - 1)
    n_back = WINDOW // kc
    fresh = (jnp.full((1, wide), NEG, F32), jnp.zeros((V_ROWS, wide), F32))

    for g in range(N_KV):
        heads = range(g * N_REP, (g + 1) * N_REP)
        qw = jnp.concatenate([qrt_ref[0, _head_rows(h), :] for h in heads], axis=1)
        qs = jnp.concatenate([qw, jnp.concatenate([bias_ref[0, g]] * N_REP, axis=1)], axis=0)
        qw = jnp.concatenate([qw, jnp.zeros_like(qw)], axis=0)

        def scores(j):
            return _dot(kp_ref[g, pl.ds(pl.multiple_of(j * kc, kc), kc), :], qs)

        def sel_chunk(j, state, diagonal):
            s = scores(j)
            return _flash_t(state, jnp.where(krow <= qcol, s, NEG) if diagonal else s, vt_ref[j, _v_rows(0, g), :])

        def sel_run(j0, n, state):
            ahead = [scores(j0), scores(j0 + 1)] if n > 1 else [scores(j0)]
            for k in range(n):
                s = ahead.pop(0)
                state = _flash_t(state, s, vt_ref[j0 + k, _v_rows(0, g), :])
                if k + 2 < n:
                    ahead.append(scores(j0 + k + 2))
            return state

        unroll = 4
        state = lax.fori_loop(0, qt // unroll, lambda i, st: sel_run(unroll * i, unroll, st), fresh)
        rest = qt % unroll
        done = qt - rest
        state = lax.cond(rest >= 2, lambda st: sel_run(done, 2, st), lambda st: st, state)
        state = lax.cond(rest % 2 == 1, lambda st: sel_chunk(qt - 1, st, False), lambda st: st, state)

        def win_scores(back):
            s = _dot(kw_ref[g, pl.ds(pl.multiple_of((qt - back) * kc, kc), kc), :], qw)
            if back == 0:
                return jnp.where(krow <= qcol, s, NEG)
            return jnp.where(krow > qcol, s, NEG) if back == n_back else s

        def tail(k, sel_state):
            ahead = [jnp.where(krow <= qcol, scores(qt), NEG), win_scores(0)]
            _, acc_s = _flash_t(sel_state, ahead.pop(0), vt_ref[qt, _v_rows(0, g), :])
            win_state = fresh
            for back in range(k + 1):
                if back + 1 <= k:
                    ahead.append(win_scores(back + 1))
                win_state = _flash_t(win_state, ahead.pop(0), vt_ref[qt - back, _v_rows(1, g), :])
            return acc_s, win_state[1]

        run = functools.partial(tail, 0)
        for k in range(1, n_back + 1):
            run = functools.partial(lambda k, fewer, st: lax.cond(qt >= k, functools.partial(tail, k), fewer, st),
                                    k, run)
        acc_sel, acc_win = run(state)

        den = slice(HEAD_DIM, HEAD_DIM + 1)
        for r, h in enumerate(heads):
            cols = slice(r * tq, (r + 1) * tq)
            ot_ref[_head_rows(h), :] = (
                ocg_ref[0, _head_rows(h), :]
                + acc_sel[:HEAD_DIM, cols] * (gt_ref[0, 3 * h + 1:3 * h + 2, :] * (1.0 / acc_sel[den, cols]))
                + acc_win[:HEAD_DIM, cols] * (gt_ref[0, 3 * h + 2:3 * h + 3, :] * (1.0 / acc_win[den, cols])))

    m = _dot(ot_ref[...].T.astype(BF16), w_ref[...])
    o_ref[...] = x_ref[...] + _rms(m, g1_ref[...])


def _sel_win_out(qr_t, bias_t, k_sel, k_win, v_t, ocg_t, gates_t, w_out_bf, x2d, g1, b):
    tq = qr_t.shape[2]
    n_t = qr_t.shape[0] // b
    t = n_t * tq
    nsp = bias_t.shape[2]
    assert WINDOW % tq == 0 and HEAD_DIM + nsp == LANES
    tile = lambda rows: pl.BlockSpec((1, rows, tq), lambda bi, i: (bi * n_t + i, 0, 0))
    keys = pl.BlockSpec((N_KV, t, LANES), lambda bi, i: (0, bi, 0))
    xrow = pl.BlockSpec((tq, D_MODEL), lambda bi, i: (bi * n_t + i, 0))
    return pl.pallas_call(
        _sel_win_body,
        grid=(b, n_t),
        in_specs=[tile(Q_DIM),
                  pl.BlockSpec((1, N_KV, nsp, tq), lambda bi, i: (bi, 0, 0, i)),
                  keys, keys,
                  pl.BlockSpec((n_t, _VT_ROWS, tq), lambda bi, i: (bi, 0, 0)),
                  tile(Q_DIM), tile(LANES),
                  _const_spec((Q_DIM, D_MODEL)), xrow, _const_spec((1, D_MODEL))],
        out_specs=xrow,
        out_shape=jax.ShapeDtypeStruct((b * t, D_MODEL), F32),
        scratch_shapes=[pltpu.VMEM((Q_DIM, tq), F32)],
        compiler_params=_params(("parallel", "parallel")),
        name="sel_win_out",
    )(qr_t, bias_t, k_sel, k_win, v_t, ocg_t, gates_t, w_out_bf, x2d, g1)


def _nsa_out_body(oc_ref, os_ref, ow_ref, gate_ref, e_ref, w_ref, x_ref, g1_ref, o_ref):
    hi, lo = _split_bf16(gate_ref[...])
    o = jnp.zeros(oc_ref.shape, F32)
    for c, src in enumerate((oc_ref, os_ref, ow_ref)):
        o = o + (_dot(hi, e_ref[c]) + _dot(lo, e_ref[c])) * src[...]
    m = _dot(o.astype(BF16), w_ref[...])
    o_ref[...] = x_ref[...] + _rms(m, g1_ref[...])


def _gate_expand():
    e = np.zeros((3, LANES, Q_DIM), np.float32)
    for h in range(N_HEADS):
        for c in range(3):
            e[c, h * 3 + c, h * HEAD_DIM:(h + 1) * HEAD_DIM] = 1.0
    return jnp.asarray(e, dtype=BF16)


def _nsa_out(o_cmp, o_sel, o_win, gates, w_bf, x2d, g1):
    n = x2d.shape[0]
    tm = _pick(n, (256, 128, 64, 32, 16, 8))
    row = lambda w: pl.BlockSpec((tm, w), lambda i: (i, 0))
    return pl.pallas_call(
        _nsa_out_body,
        grid=(n // tm,),
        in_specs=[row(Q_DIM), row(Q_DIM), row(Q_DIM), row(LANES), _const_spec((3, LANES, Q_DIM)),
                  _const_spec((Q_DIM, D_MODEL)), row(D_MODEL), _const_spec((1, D_MODEL))],
        out_specs=row(D_MODEL),
        out_shape=jax.ShapeDtypeStruct((n, D_MODEL), F32),
        compiler_params=_params(("parallel",)),
        name="nsa_out",
    )(o_cmp, o_sel, o_win, gates, _gate_expand(), w_bf, x2d, g1)


_RING_SLOTS = 3


class _PageRing:
    def __init__(self, pt_ref, cache_hbm, buf_ref, sem_ref, n_b, n_c):
        self.pt, self.cache, self.buf, self.sem = pt_ref, cache_hbm, buf_ref, sem_ref
        self.n_b, self.n_c, self.pg, self.slots = n_b, n_c, buf_ref.shape[1], buf_ref.shape[0]

    def _copies(self, step):
        b, c, slot = step // self.n_c, step % self.n_c, step % self.slots
        return [pltpu.make_async_copy(self.cache.at[self.pt[b, c * self.pg + p]], self.buf.at[slot, p],
                                      self.sem.at[slot]) for p in range(self.pg)]

    def acquire(self, b, c):
        step = b * self.n_c + c
        total = self.n_b * self.n_c
        ahead = self.slots - 1

        @pl.when(step == 0)
        def _():
            for first in range(min(ahead, total)):
                for cp in self._copies(first):
                    cp.start()

        @pl.when(step + ahead < total)
        def _():
            for cp in self._copies(step + ahead):
                cp.start()

        for cp in self._copies(step):
            cp.wait()
        return step % self.slots


def _compress_paged_body(n_b, n_c, pt_ref, cache_hbm, perm_ref, xl_ref, wk_ref, wv_ref, pek_ref, pev_ref,
                         w1k_ref, w1v_ref, w2k_ref, w2v_ref, kcc_ref, vcc_ref, buf_ref, sem_ref, xs_ref,
                         abk_ref, abv_ref):
    pg, page = buf_ref.shape[1], buf_ref.shape[4]
    rows = pg * page // STRIDE
    b, c = pl.program_id(0), pl.program_id(1)
    slot = _PageRing(pt_ref, cache_hbm, buf_ref, sem_ref, n_b, n_c).acquire(b, c)
    r0 = pl.multiple_of(c * rows, rows)
    for kv, (ab_ref, w_ref) in enumerate(((abk_ref, wk_ref), (abv_ref, wv_ref))):
        _pages_first_layer(lambda p: buf_ref[slot, p, kv], pg, perm_ref, xs_ref, ab_ref, w_ref, r0)

    @pl.when(c == n_c - 1)
    def _():
        _compress_finish(xl_ref, ((abk_ref, wk_ref, pek_ref, w1k_ref, w2k_ref, kcc_ref),
                                  (abv_ref, wv_ref, pev_ref, w1v_ref, w2v_ref, vcc_ref)), kcc_ref.shape[1])


def _compress_paged(page_table, cache_fm, x_last, wk, wv, pek, pev, w1k, w1v, w2k, w2v):
    b, n_pages = page_table.shape
    page = cache_fm.shape[3]
    pg = _pick(n_pages, (16, 8, 4, 2, 1))
    n_c = n_pages // pg
    n = n_pages * page // STRIDE
    flat = L_CMP * HEAD_DIM
    out = jax.ShapeDtypeStruct((b, n, KV_DIM), F32)
    const = _const_spec
    per_page = page // STRIDE
    return pl.pallas_call(
        functools.partial(_compress_paged_body, b, n_c),
        grid_spec=pltpu.PrefetchScalarGridSpec(
            num_scalar_prefetch=1,
            grid=(b, n_c),
            in_specs=[pl.BlockSpec(memory_space=pl.ANY),
                      const((page, page)),
                      pl.BlockSpec((1, 8, _CHUNK_W), lambda bi, c, pt: (bi, 0, 0)),
                      const(_W1_SHAPE), const(_W1_SHAPE),
                      const((flat, 1)), const((flat, 1)), const((flat, CMP_HID)), const((flat, CMP_HID)),
                      const((N_KV * CMP_HID, KV_DIM)), const((N_KV * CMP_HID, KV_DIM))],
            out_specs=[pl.BlockSpec((1, n, KV_DIM), lambda bi, c, pt: (bi, 0, 0))] * 2,
            scratch_shapes=[pltpu.VMEM((_RING_SLOTS, pg, 2, KV_DIM, page), F32), pltpu.SemaphoreType.DMA((_RING_SLOTS,)),
                            pltpu.VMEM((STRIDE, pg * per_page, KV_DIM), F32),
                            pltpu.VMEM((n + 8, _AB_W), F32), pltpu.VMEM((n + 8, _AB_W), F32)]),
        out_shape=[out, out],
        compiler_params=_params(("arbitrary", "arbitrary")),
        name="compress_paged",
    )(page_table, cache_fm, _chunk_sort_matrix(page), x_last, wk, wv, pek, pev, w1k, w1v, w2k, w2v)


def _softmax_rows(s):
    e = jnp.exp(s - jnp.max(s, axis=-1, keepdims=True))
    return e / jnp.sum(e, axis=-1, keepdims=True)


def _sample_cmp_body(n_sel, past_len, t_new, q_ref, kcc_ref, vcc_ref, ovt_ref, rep_ref, o_ref, sel_ref):
    rows = q_ref.shape[1]
    nc = kcc_ref.shape[1]
    nsp = ovt_ref.shape[0]
    gq = N_KV * t_new
    tq_col = lax.broadcasted_iota(jnp.int32, (rows, 1), 0) & (t_new - 1)
    cmp_end = lax.broadcasted_iota(jnp.int32, (1, nc), 1) * STRIDE + (L_CMP - 1)
    cmask = cmp_end <= past_len + tq_col
    lm = jnp.where(cmask, _dot_nt(q_ref[0], kcc_ref[0].astype(BF16)), NEG)
    pc = jnp.where(cmask, _softmax_rows(lm), 0.0)
    o_ref[0] = _dot(pc.astype(BF16), vcc_ref[0].astype(BF16))
    pg = pc.reshape(N_KV, N_REP, t_new, nc).sum(axis=1).reshape(gq, nc)
    hi, lo = _split_bf16(pg)
    imp_t = _dot_nt(ovt_ref[...], hi) + _dot_nt(ovt_ref[...], lo)
    blk = lax.broadcasted_iota(jnp.int32, (nsp, gq), 0)
    qpos = past_len + (lax.broadcasted_iota(jnp.int32, (nsp, gq), 1) & (t_new - 1))
    sel = _select_blocks(imp_t, blk, qpos, n_sel, min(N_TOP, n_sel))
    sel_ref[0] = _dot_nt(rep_ref[...], jnp.where(sel, 1.0, 0.0).astype(BF16)).astype(BF16)


def _sample_cmp(q_bd, kcc, vcc, n_cmp, n_sel, nsp, past_len, t_new):
    b, rows, _ = q_bd.shape
    nc = kcc.shape[1]
    gq = N_KV * t_new
    ovt = _overlap_t(nsp, nc, n_sel, n_cmp)
    rep = np.zeros((rows, gq), np.float32)
    for h in range(N_HEADS):
        for t in range(t_new):
            rep[h * t_new + t, (h // N_REP) * t_new + t] = 1.0
    return pl.pallas_call(
        functools.partial(_sample_cmp_body, n_sel, past_len, t_new),
        grid=(b,),
        in_specs=[pl.BlockSpec((1, rows, KV_DIM), lambda bi: (bi, 0, 0)),
                  pl.BlockSpec((1, nc, KV_DIM), lambda bi: (bi, 0, 0)),
                  pl.BlockSpec((1, nc, KV_DIM), lambda bi: (bi, 0, 0)),
                  _const_spec((nsp, nc)), _const_spec((rows, gq))],
        out_specs=[pl.BlockSpec((1, rows, KV_DIM), lambda bi: (bi, 0, 0)),
                   pl.BlockSpec((1, rows, nsp), lambda bi: (bi, 0, 0))],
        out_shape=[jax.ShapeDtypeStruct((b, rows, KV_DIM), F32), jax.ShapeDtypeStruct((b, rows, nsp), BF16)],
        compiler_params=_params(("parallel",)),
        name="sample_cmp",
    )(q_bd, kcc, vcc, ovt, jnp.asarray(rep, dtype=BF16))


def _flash_update(s, pv_fn, m_ref, l_ref, acc_ref):
    m_prev = m_ref[...]
    m_new = jnp.maximum(m_prev, jnp.max(s, axis=-1, keepdims=True))
    alpha = jnp.exp(m_prev - m_new)
    p = jnp.exp(s - m_new)
    l_ref[...] = alpha * l_ref[...] + jnp.sum(p, axis=-1, keepdims=True)
    acc_ref[...] = alpha * acc_ref[...] + pv_fn(p.astype(BF16))
    m_ref[...] = m_new


def _sample_sel_body(t_new, n_b, n_c, pt_ref, cache_hbm, q_ref, sel_ref, e_ref, et_ref, tail_ref, o_ref,
                     buf_ref, sem_ref, m_ref, l_ref, acc_ref):
    rows = q_ref.shape[1]
    pg, page = buf_ref.shape[1], buf_ref.shape[4]
    b, c = pl.program_id(0), pl.program_id(1)
    q = q_ref[0]

    @pl.when(c == 0)
    def _():
        _flash_init(m_ref, l_ref, acc_ref)

    @pl.when(c < n_c)
    def _():
        slot = _PageRing(pt_ref, cache_hbm, buf_ref, sem_ref, n_b, n_c).acquire(b, c)
        halves = [range(0, pg // 2), range(pg // 2, pg)] if pg > 1 else [range(pg)]

        def scores(pages):
            s = jnp.concatenate([_dot(q, buf_ref[slot, p, 0].astype(BF16)) for p in pages], axis=1)
            ok = _dot(sel_ref[0], e_ref[:, pages.start * page:pages.stop * page]) > 0.5
            return jnp.where(ok, s, NEG)

        for pages, s in [(pages, scores(pages)) for pages in halves]:
            def pv(p_bf, pages=pages):
                return sum(_dot_nt(p_bf[:, i * page:(i + 1) * page], buf_ref[slot, p, 1].astype(BF16))
                           for i, p in enumerate(pages))

            _flash_update(s, pv, m_ref, l_ref, acc_ref)

    @pl.when(c == n_c)
    def _():
        kv = tail_ref[0]
        nk = kv.shape[0]
        s = _dot_nt(q, kv[:, :KV_DIM].astype(BF16))
        tq = lax.broadcasted_iota(jnp.int32, (rows, 1), 0) & (t_new - 1)
        ok = (_dot(sel_ref[0], et_ref[...]) > 0.5) & (lax.broadcasted_iota(jnp.int32, (1, nk), 1) <= tq)
        _flash_update(jnp.where(ok, s, NEG), lambda p_bf: _dot(p_bf, kv[:, KV_DIM:].astype(BF16)),
                      m_ref, l_ref, acc_ref)
        o_ref[0] = acc_ref[...] / l_ref[...]


def _sample_sel(page_table, cache_fm, q_bd, sel01, tail_sel, n_sel, t_new):
    b, rows, _ = q_bd.shape
    nsp = sel01.shape[2]
    n_pages = page_table.shape[1]
    page = cache_fm.shape[3]
    pg = _pick(n_pages, (16, 8, 4, 2, 1))
    n_c = n_pages // pg
    kc = pg * page
    nk = tail_sel.shape[1]
    key_blk = np.arange(n_pages * page) // L_SEL
    e = (np.arange(nsp)[:, None] == key_blk[None, :]).astype(np.float32)
    et = np.zeros((nsp, nk), np.float32)
    et[n_sel - 1, :] = 1.0
    return pl.pallas_call(
        functools.partial(_sample_sel_body, t_new, b, n_c),
        grid_spec=pltpu.PrefetchScalarGridSpec(
            num_scalar_prefetch=1,
            grid=(b, n_c + 1),
            in_specs=[pl.BlockSpec(memory_space=pl.ANY),
                      pl.BlockSpec((1, rows, KV_DIM), lambda bi, c, pt: (bi, 0, 0)),
                      pl.BlockSpec((1, rows, nsp), lambda bi, c, pt: (bi, 0, 0)),
                      pl.BlockSpec((nsp, kc), lambda bi, c, pt: (0, jnp.minimum(c, n_c - 1))),
                      _const_spec((nsp, nk)),
                      pl.BlockSpec((1, nk, 2 * KV_DIM), lambda bi, c, pt: (bi, 0, 0))],
            out_specs=pl.BlockSpec((1, rows, KV_DIM), lambda bi, c, pt: (bi, 0, 0)),
            scratch_shapes=[pltpu.VMEM((_RING_SLOTS, pg, 2, KV_DIM, page), F32), pltpu.SemaphoreType.DMA((_RING_SLOTS,)),
                            pltpu.VMEM((rows, 1), F32), pltpu.VMEM((rows, 1), F32),
                            pltpu.VMEM((rows, KV_DIM), F32)]),
        out_shape=jax.ShapeDtypeStruct((b, rows, KV_DIM), F32),
        compiler_params=_params(("arbitrary", "arbitrary")),
        name="sample_sel",
    )(page_table, cache_fm, q_bd, sel01, jnp.asarray(e, dtype=BF16), jnp.asarray(et, dtype=BF16), tail_sel)


def _sample_win_body(t_new, q_ref, kv_ref, tail_ref, o_ref):
    rows = q_ref.shape[1]
    wb = kv_ref.shape[3]
    nk = tail_ref.shape[1]
    q = q_ref[0]
    tq = lax.broadcasted_iota(jnp.int32, (rows, 1), 0) & (t_new - 1)
    tail = tail_ref[0]
    s_old = jnp.where(lax.broadcasted_iota(jnp.int32, (1, wb), 1) > tq - WINDOW + wb,
                      _dot(q, kv_ref[0, 0].astype(BF16)), NEG)
    s_new = jnp.where(lax.broadcasted_iota(jnp.int32, (1, nk), 1) <= tq,
                      _dot_nt(q, tail[:, :KV_DIM].astype(BF16)), NEG)
    m = jnp.maximum(jnp.max(s_old, axis=-1, keepdims=True), jnp.max(s_new, axis=-1, keepdims=True))
    p_old = jnp.exp(s_old - m)
    p_new = jnp.exp(s_new - m)
    den = jnp.sum(p_old, axis=-1, keepdims=True) + jnp.sum(p_new, axis=-1, keepdims=True)
    o = _dot_nt(p_old.astype(BF16), kv_ref[0, 1].astype(BF16)) + _dot(p_new.astype(BF16), tail[:, KV_DIM:].astype(BF16))
    o_ref[0] = o / den


def _sample_win(q_bd, win_fm, tail_win, t_new):
    b, rows, _ = q_bd.shape
    wb = win_fm.shape[3]
    nk = tail_win.shape[1]
    return pl.pallas_call(
        functools.partial(_sample_win_body, t_new),
        grid=(b,),
        in_specs=[pl.BlockSpec((1, rows, KV_DIM), lambda bi: (bi, 0, 0)),
                  pl.BlockSpec((1, 2, KV_DIM, wb), lambda bi: (bi, 0, 0, 0)),
                  pl.BlockSpec((1, nk, 2 * KV_DIM), lambda bi: (bi, 0, 0))],
        out_specs=pl.BlockSpec((1, rows, KV_DIM), lambda bi: (bi, 0, 0)),
        out_shape=jax.ShapeDtypeStruct((b, rows, KV_DIM), F32),
        compiler_params=_params(("parallel",)),
        name="sample_win",
    )(q_bd, win_fm, tail_win)


def _heads_major(a2d, b, t, n):
    return a2d.reshape(b, t, n, HEAD_DIM).transpose(0, 2, 1, 3)


def _tokens_major(a_hm):
    b, n, t, d = a_hm.shape
    return a_hm.transpose(0, 2, 1, 3).reshape(b * t, n * d)


def _block_diag_q(q2d, b, t):
    q_hm = _heads_major(q2d, b, t, N_HEADS)
    onehot = jnp.asarray(np.eye(N_KV)[np.arange(N_HEADS) // N_REP], dtype=q2d.dtype)
    return jnp.einsum('bhtd,hg->bhtgd', q_hm, onehot).reshape(b, N_HEADS * t, KV_DIM)


def _own_group(o_bd, b, t):
    o = o_bd.reshape(b, N_KV, N_REP, t, N_KV, HEAD_DIM)
    o = jnp.stack([o[:, g, :, :, g, :] for g in range(N_KV)], axis=1)
    return o.transpose(0, 3, 1, 2, 4).reshape(b * t, Q_DIM)


def _nsa_prompt(x2d, b, t, g0, g1, w_in_bf, w_out_bf, cw):
    pos = jnp.arange(t)
    q_t, qr_t, v_t, gates_t, cmp_fm, sel_fm, win_fm, k_sel, k_win = _nsa_proj(x2d, g0, w_in_bf, pos)
    assert t % STRIDE == 0 and t % L_SEL == 0
    n_ch = t // STRIDE
    n_cmp = n_ch - 1
    n_sel = t // L_SEL
    nsp = -(-n_sel // HEAD_DIM) * HEAD_DIM
    x_last = jnp.zeros((b, 8, _CHUNK_W), F32)
    (wk, pek, w1k, w2k), (wv, pev, w1v, w2v) = cw
    kcc, vcc = _compress(cmp_fm, x_last, wk, wv, pek, pev, w1k, w1v, w2k, w2v)
    gm = lambda a: a.reshape(b, -1, N_KV, HEAD_DIM).transpose(0, 2, 1, 3).astype(BF16)
    ocg_t, bias_t = _cmp_topk(q_t, gm(kcc), vcc.transpose(0, 2, 1).astype(BF16), gates_t, b, n_cmp, n_sel, nsp)
    x1 = _sel_win_out(qr_t, bias_t, k_sel, k_win, v_t, ocg_t, gates_t, w_out_bf, x2d, g1, b)
    rows5 = lambda a: a.reshape(b, 2, N_KV, HEAD_DIM, -1).transpose(0, 4, 1, 2, 3)[None]
    n_win = min(WINDOW, t)
    return x1, (rows5(cmp_fm), rows5(sel_fm), rows5(win_fm[:, :, t - n_win:]))


def _nsa_sample(x2d, b, t, g0, w_in_bf, cw, cache_cmp_l, cache_sel_l, cache_win_l, page_table):
    n_pages = page_table.shape[1]
    page = cache_cmp_l.shape[1]
    past_len = n_pages * page
    assert page % L_SEL == 0 and page % STRIDE == 0 and t <= STRIDE and t & (t - 1) == 0
    pos = past_len + jnp.arange(t)
    q, qr, cmp_rows, sel_rows, win_rows, gates = _nsa_proj(x2d, g0, w_in_bf, pos)
    row_w = 2 * KV_DIM
    fm = lambda a: a.transpose(0, 2, 3, 4, 1).reshape(a.shape[0], 2, KV_DIM, a.shape[1])
    n_past_ch = past_len // STRIDE
    n_cmp = n_past_ch
    n_sel = past_len // L_SEL + 1
    nsp = -(-n_sel // LANES) * LANES
    new3 = lambda a: a.reshape(b, t, row_w)
    x_last = jnp.pad(new3(cmp_rows), ((0, 0), (0, STRIDE - t), (0, 0))).reshape(b, 1, _CHUNK_W)
    x_last = jnp.pad(x_last, ((0, 0), (0, 7), (0, 0)))
    (wk, pek, w1k, w2k), (wv, pev, w1v, w2v) = cw
    kcc, vcc = _compress_paged(page_table, fm(cache_cmp_l), x_last, wk, wv, pek, pev, w1k, w1v, w2k, w2v)
    o_cmp_bd, sel01 = _sample_cmp(_block_diag_q(q, b, t), kcc, vcc, n_cmp, n_sel, nsp, past_len, t)
    qr_bd = _block_diag_q(qr, b, t)
    tail = lambda a: jnp.pad(new3(a), ((0, 0), (0, LANES - t), (0, 0)))
    o_sel_bd = _sample_sel(page_table, fm(cache_sel_l), qr_bd, sel01, tail(sel_rows), n_sel, t)
    win_fm = fm(cache_win_l)
    o_win_bd = _sample_win(qr_bd, win_fm, tail(win_rows), t)
    w_buf = win_fm.shape[3]
    new_fm = new3(win_rows).reshape(b, t, 2, KV_DIM).transpose(0, 2, 3, 1)
    new_win = jnp.concatenate([win_fm, new_fm], axis=3)[..., -w_buf:]
    new_win = new_win.reshape(b, 2, N_KV, HEAD_DIM, w_buf).transpose(0, 4, 1, 2, 3)
    rows5 = lambda a: a.reshape(1, b, -1, 2, N_KV, HEAD_DIM)
    caches = (rows5(cmp_rows), rows5(sel_rows), new_win[None])
    return _own_group(o_cmp_bd, b, t), _own_group(o_sel_bd, b, t), _own_group(o_win_bd, b, t), gates, caches


def kernel(x_prompt, x_sample, cache_cmp, cache_sel, cache_win, state_conv, page_table, p_prompt, p_sample,
           norm_g, w_ff1, w_ff2, w_ple, w_ple_gate, b_ple_gate,
           conv_w_pw1, conv_b_pw1, conv_w_dw, conv_b_dw, conv_ln_g, conv_ln_b, conv_w_pw2, conv_b_pw2,
           nsa_w_in, nsa_w_out, nsa_cmp_pe, nsa_cmp_w1, nsa_cmp_w2):
    depth = norm_g.shape[0]
    bf = lambda a: a.astype(BF16)
    row = lambda a: a.reshape(1, -1)
    w_ff1_bf, w_ff2_bf, w_ple_bf, w_gate_bf = bf(w_ff1), bf(w_ff2), bf(w_ple), bf(w_ple_gate)
    w_pw1_bf, w_pw2_bf, w_out_bf = bf(conv_w_pw1), bf(conv_w_pw2), bf(nsa_w_out)
    w_in_bf = bf(jnp.pad(nsa_w_in, ((0, 0), (0, 0), (0, _PROJ_W - nsa_w_in.shape[2]))))
    w_dw = jnp.pad(conv_w_dw, ((0, 0), (0, CONV_HALO - CONV_W), (0, 0)))

    def run(x, p, sample):
        b, t, _ = x.shape
        n = b * t
        x2d = x.reshape(n, D_MODEL)
        cmp_o, sel_o, win_o, conv_o = [], [], [], []
        for i in range(depth):
            g = lambda j: row(norm_g[i, j])
            if i % 2 == 0:
                c = i // 2
                u = _conv_front(x2d, g(0), w_pw1_bf[c], row(conv_b_pw1[c])).reshape(b, t, D_MODEL)
                if sample:
                    hist = jnp.pad(state_conv[c], ((0, 0), (CONV_HALO - (CONV_W - 1), 0), (0, 0)))
                else:
                    hist = jnp.zeros((b, CONV_HALO, D_MODEL), F32)
                tp = -(-t // CONV_HALO) * CONV_HALO
                pad_t = lambda a: jnp.pad(a, ((0, 0), (0, tp - t), (0, 0))) if tp > t else a
                x1 = _conv_back(hist, pad_t(u), pad_t(x2d.reshape(b, t, D_MODEL)), w_dw[c], row(conv_b_dw[c]),
                                row(conv_ln_g[c]), row(conv_ln_b[c]), w_pw2_bf[c], row(conv_b_pw2[c]),
                                g(1))[:, :t].reshape(n, D_MODEL)
                keep = CONV_W - 1
                conv_o.append(jnp.concatenate([hist, u], axis=1)[:, CONV_HALO + t - keep:][None])
            else:
                a = i // 2
                cw = _compress_weights(nsa_cmp_pe[a], nsa_cmp_w1[a], nsa_cmp_w2[a])
                if sample:
                    oc, osel, ow, gates, caches = _nsa_sample(x2d, b, t, g(0), w_in_bf[a], cw, cache_cmp[a],
                                                              cache_sel[a], cache_win[a], page_table)
                    x1 = _nsa_out(oc, osel, ow, gates, w_out_bf[a], x2d, g(1))
                else:
                    x1, caches = _nsa_prompt(x2d, b, t, g(0), g(1), w_in_bf[a], w_out_bf[a], cw)
                for dst, rows_ in zip((cmp_o, sel_o, win_o), caches):
                    dst.append(rows_)
            x2d = _ffn(x1, g(2), w_ff1_bf[i], w_ff2_bf[i], g(3), w_gate_bf[i], row(b_ple_gate[i]),
                       p.reshape(depth, n, -1), i, w_ple_bf[i])
        cat = lambda parts: jnp.concatenate(parts, axis=0)
        return x2d.reshape(b, t, D_MODEL), cat(cmp_o), cat(sel_o), cat(win_o), cat(conv_o)

    y_p, cmp_p, sel_p, win_p, conv_p = run(x_prompt, p_prompt, False)
    y_s, cmp_s, sel_s, win_s, conv_s = run(x_sample, p_sample, True)
    return (y_p, y_s, cmp_p, cmp_s, sel_p, sel_s, win_p, win_s, conv_p, conv_s)
```
